```python
import jax, jax.numpy as jnp
from jax import lax
import numpy as np

D_MODEL = 2048
BATCH = 2
SEQ = 4096
DEPTH = 2
DEC_BATCH = 32
DEC_SEQ = 1
PAST_LEN = 16384
PAGE_SIZE = 128

D_MIX = D_MODEL
M_WIDTH = D_MIX // 2
M_HEADS = 4
M_DK = M_WIDTH // M_HEADS
M_CHUNK = 128
C_WIDTH = D_MIX // 4
CONV_K = 31
CONV_BUF = CONV_K - 1
A_WIDTH = D_MIX - M_WIDTH - C_WIDTH
HEAD_DIM = 64
A_HEADS = A_WIDTH // HEAD_DIM
KV_HEADS = 2
GROUP = A_HEADS // KV_HEADS
WINDOW = 128
A_BLOCK = WINDOW
ALPHA = (2 * DEPTH) ** 0.25
BETA = (8 * DEPTH) ** -0.25
LN_EPS = 1e-5
IN_SPLITS = (M_WIDTH, M_WIDTH, M_WIDTH, M_WIDTH, M_HEADS, M_HEADS, M_WIDTH,
             C_WIDTH, C_WIDTH, C_WIDTH,
             A_WIDTH, KV_HEADS * HEAD_DIM, KV_HEADS * HEAD_DIM, A_WIDTH)
D_IN = sum(IN_SPLITS)

kernel_name = 'hymba_mlstm_conformer_swa_decode_step'


def layer_norm(x, g, b):
    xf = x.astype(jnp.float32)
    mu = jnp.mean(xf, axis=-1, keepdims=True)
    var = jnp.mean(jnp.square(xf - mu), axis=-1, keepdims=True)
    y = (xf - mu) * lax.rsqrt(var + LN_EPS) * g.astype(jnp.float32) + b.astype(jnp.float32)
    return y.astype(x.dtype)


def in_projection(x, w_in):
    h = jnp.einsum('btd,de->bte', x, w_in)
    cuts = np.cumsum(np.array(IN_SPLITS))[:-1].tolist()
    return jnp.split(h, cuts, axis=-1)


def mlstm_chunk(q, k, v, ig, lf, C0, n0, m0):
    L = q.shape[2]
    b = jnp.cumsum(lf, axis=-1)
    log_w = b[..., :, None] - b[..., None, :] + ig[..., None, :]
    causal = jnp.tril(jnp.ones((L, L), dtype=bool))
    log_w = jnp.where(causal, log_w, -jnp.inf)
    log_w0 = b + m0[..., None]
    m = jnp.maximum(log_w0, jnp.max(log_w, axis=-1))
    w = jnp.exp(log_w - m[..., None])
    w0 = jnp.exp(log_w0 - m)
    s = jnp.einsum('bhtd,bhsd->bhts', q, k) * w
    num = jnp.einsum('bhts,bhse->bhte', s, v) + w0[..., None] * jnp.einsum('bhtd,bhde->bhte', q, C0)
    den = jnp.sum(s, axis=-1) + w0 * jnp.einsum('bhtd,bhd->bht', q, n0)
    h = num / jnp.maximum(jnp.abs(den), jnp.exp(-m))[..., None]
    m_end = m[..., -1]
    g = jnp.exp(b[..., -1:] - b + ig - m_end[..., None])
    g0 = jnp.exp(b[..., -1] + m0 - m_end)
    C = g0[..., None, None] * C0 + jnp.einsum('bhs,bhsd,bhse->bhde', g, k, v)
    n = g0[..., None] * n0 + jnp.einsum('bhs,bhsd->bhd', g, k)
    return h, C, n, m_end


def mlstm_prompt(q, k, v, ig, lf):
    B, T, H, D = q.shape
    nc = T // M_CHUNK

    def to_chunks(a):
        a = a.reshape((B, nc, M_CHUNK) + a.shape[2:])
        return jnp.moveaxis(jnp.moveaxis(a, 1, 0), 2, 3)

    def step(carry, xs):
        h, C, n, m = mlstm_chunk(*xs, *carry)
        return (C, n, m), h

    init = (jnp.zeros((B, H, D, D), jnp.float32), jnp.zeros((B, H, D), jnp.float32),
            jnp.zeros((B, H), jnp.float32))
    (C, n, m), h = lax.scan(step, init, tuple(to_chunks(a) for a in (q, k, v, ig, lf)))
    h = jnp.swapaxes(jnp.moveaxis(h, 0, 1), 2, 3).reshape(B, T, H, D)
    return h, C, n, m


def mlstm_step(q, k, v, ig, lf, C0, n0, m0):
    f32 = jnp.float32
    h, C, n, m = mlstm_chunk(jnp.swapaxes(q, 1, 2), jnp.swapaxes(k, 1, 2), jnp.swapaxes(v, 1, 2),
                             jnp.swapaxes(ig, 1, 2), jnp.swapaxes(lf, 1, 2),
                             C0.astype(f32), n0.astype(f32), m0.astype(f32))
    return jnp.swapaxes(h, 1, 2), C, n, m


def conformer_conv(u, g, buf, conv_w, conv_b, ln_g, ln_b):
    a = u * jax.nn.sigmoid(g)
    a_full = jnp.concatenate([buf.astype(a.dtype), a], axis=1)
    y = lax.conv_general_dilated(a_full, conv_w.astype(a.dtype)[:, None, :], window_strides=(1,),
                                 padding='VALID', dimension_numbers=('NWC', 'WIO', 'NWC'),
                                 feature_group_count=C_WIDTH)
    y = layer_norm(y + conv_b, ln_g, ln_b)
    return jax.nn.silu(y), a_full[:, a_full.shape[1] - CONV_BUF:]


def sink_softmax(scores, mask, sinks):
    scores = jnp.where(mask, scores, -jnp.inf)
    sink = jnp.broadcast_to(sinks.astype(jnp.float32).reshape(KV_HEADS, GROUP, 1, 1),
                            scores.shape[:-1] + (1,))
    probs = jax.nn.softmax(jnp.concatenate([scores, sink], axis=-1), axis=-1)
    return probs[..., :-1]


def swa_prompt(q, k, v, sinks):
    B, T = q.shape[:2]
    nb = T // A_BLOCK
    qb = q.reshape(B, nb, A_BLOCK, KV_HEADS, GROUP, HEAD_DIM)

    def with_prev(a):
        a = a.reshape(B, nb, A_BLOCK, KV_HEADS, HEAD_DIM)
        prev = jnp.concatenate([jnp.zeros_like(a[:, :1]), a[:, :-1]], axis=1)
        return jnp.concatenate([prev, a], axis=2)

    kk, vv = with_prev(k), with_prev(v)
    s = jnp.einsum('bnikgd,bnjkd->bnkgij', qb, kk, preferred_element_type=jnp.float32) * (HEAD_DIM ** -0.5)
    i = jnp.arange(A_BLOCK)[:, None]
    j = jnp.arange(2 * A_BLOCK)[None, :]
    rel = A_BLOCK + i - j
    band = (rel >= 0) & (rel <= WINDOW)
    has_prev = jnp.arange(nb)[:, None, None] > 0
    mask = band[None] & (has_prev | (j >= A_BLOCK)[None])
    p = sink_softmax(s, mask[None, :, None, None], sinks)
    o = jnp.einsum('bnkgij,bnjkd->bnikgd', p.astype(v.dtype), vv).reshape(B, T, A_WIDTH)
    return o, k[:, T - WINDOW:], v[:, T - WINDOW:]


def swa_sample(q, k, v, ck, cv, sinks):
    B, S = q.shape[:2]
    n_buf = ck.shape[1]
    kk = jnp.concatenate([ck.astype(k.dtype), k], axis=1)
    vv = jnp.concatenate([cv.astype(v.dtype), v], axis=1)
    qg = q.reshape(B, S, KV_HEADS, GROUP, HEAD_DIM)
    s = jnp.einsum('bikgd,bjkd->bkgij', qg, kk, preferred_element_type=jnp.float32) * (HEAD_DIM ** -0.5)
    i = jnp.arange(S)[:, None]
    j = jnp.arange(n_buf + S)[None, :]
    rel = n_buf + i - j
    mask = (rel >= 0) & (rel <= WINDOW)
    p = sink_softmax(s, mask, sinks)
    o = jnp.einsum('bkgij,bjkd->bikgd', p.astype(v.dtype), vv).reshape(B, S, A_WIDTH)
    return o, kk[:, S:], vv[:, S:]


def mixer_layer(x, w_in, w_out, b_igate, b_fgate, m_norm_g, conv_w, conv_b, conv_ln_g, conv_ln_b,
                sinks, past):
    B, T, _ = x.shape
    f32 = jnp.float32
    mq, mk, mv, mo, mi, mf, mz, cu, cg, cz, aq, ak, av, az = in_projection(x, w_in)
    q = mq.reshape(B, T, M_HEADS, M_DK).astype(f32)
    k = mk.reshape(B, T, M_HEADS, M_DK).astype(f32) * (M_DK ** -0.5)
    v = mv.reshape(B, T, M_HEADS, M_DK).astype(f32)
    ig = mi.astype(f32) + b_igate.astype(f32)
    lf = jax.nn.log_sigmoid(mf.astype(f32) + b_fgate.astype(f32))
    qa = aq.reshape(B, T, A_HEADS, HEAD_DIM)
    ka = ak.reshape(B, T, KV_HEADS, HEAD_DIM)
    va = av.reshape(B, T, KV_HEADS, HEAD_DIM)
    if past is None:
        h, C, n, m = mlstm_prompt(q, k, v, ig, lf)
        conv_buf = jnp.zeros((B, CONV_BUF, C_WIDTH), x.dtype)
        a_out, new_k, new_v = swa_prompt(qa, ka, va, sinks)
    else:
        C0, n0, m0, conv_buf, ck, cv = past
        h, C, n, m = mlstm_step(q, k, v, ig, lf, C0, n0, m0)
        a_out, new_k, new_v = swa_sample(qa, ka, va, ck, cv, sinks)
    mu = jnp.mean(h, axis=-1, keepdims=True)
    var = jnp.mean(jnp.square(h - mu), axis=-1, keepdims=True)
    hn = ((h - mu) * lax.rsqrt(var + LN_EPS)).reshape(B, T, M_WIDTH) * m_norm_g.astype(f32)
    m_out = hn * jax.nn.sigmoid(mo.astype(f32)) * jax.nn.silu(mz.astype(f32))
    c_out, new_buf = conformer_conv(cu, cg, conv_buf, conv_w, conv_b, conv_ln_g, conv_ln_b)
    merged = jnp.concatenate([m_out.astype(x.dtype),
                              (c_out * jax.nn.silu(cz)).astype(x.dtype),
                              (a_out * jax.nn.silu(az)).astype(x.dtype)], axis=-1)
    y = jnp.einsum('bte,ed->btd', merged, w_out)
    return y, (C, n, m, new_buf, new_k, new_v)


def setup_inputs(seed: int = 0) -> dict:
    key = jax.random.key(seed)
    ks = jax.random.split(key, 24)
    nrm = jax.random.normal
    n_buf = min(WINDOW, PAST_LEN)
    return {
        'x_prompt': nrm(ks[0], (BATCH, SEQ, D_MODEL), jnp.float32),
        'x_sample': nrm(ks[1], (DEC_BATCH, DEC_SEQ, D_MODEL), jnp.float32),
        'state_C': 0.5 * nrm(ks[2], (DEPTH, DEC_BATCH, M_HEADS, M_DK, M_DK), jnp.float32),
        'state_n': 0.5 * nrm(ks[3], (DEPTH, DEC_BATCH, M_HEADS, M_DK), jnp.float32),
        'state_m': nrm(ks[4], (DEPTH, DEC_BATCH, M_HEADS), jnp.float32),
        'state_conv': 0.5 * nrm(ks[5], (DEPTH, DEC_BATCH, CONV_BUF, C_WIDTH), jnp.float32),
        'cache_k': nrm(ks[6], (DEPTH, DEC_BATCH, n_buf, KV_HEADS, HEAD_DIM), jnp.float32),
        'cache_v': nrm(ks[7], (DEPTH, DEC_BATCH, n_buf, KV_HEADS, HEAD_DIM), jnp.float32),
        'w_in': nrm(ks[8], (DEPTH, D_MODEL, D_IN), jnp.float32) * (D_MODEL ** -0.5),
        'w_out': nrm(ks[9], (DEPTH, D_MIX, D_MODEL), jnp.float32) * (D_MIX ** -0.5) * BETA,
        'b_igate': 0.1 * nrm(ks[10], (DEPTH, M_HEADS), jnp.float32),
        'b_fgate': 3.0 + 3.0 * jax.random.uniform(ks[11], (DEPTH, M_HEADS), jnp.float32),
        'm_norm_g': 1.0 + 0.1 * nrm(ks[12], (DEPTH, M_WIDTH), jnp.float32),
        'conv_w': nrm(ks[13], (DEPTH, CONV_K, C_WIDTH), jnp.float32) * (CONV_K ** -0.5),
        'conv_b': 0.02 * nrm(ks[14], (DEPTH, C_WIDTH), jnp.float32),
        'conv_ln_g': 1.0 + 0.1 * nrm(ks[15], (DEPTH, C_WIDTH), jnp.float32),
        'conv_ln_b': 0.02 * nrm(ks[16], (DEPTH, C_WIDTH), jnp.float32),
        'sinks': 0.5 * nrm(ks[17], (DEPTH, A_HEADS), jnp.float32),
        'ln_g': 1.0 + 0.1 * nrm(ks[18], (DEPTH, D_MODEL), jnp.float32),
        'ln_b': 0.02 * nrm(ks[19], (DEPTH, D_MODEL), jnp.float32),
    }


def reference(x_prompt, x_sample, state_C, state_n, state_m, state_conv, cache_k, cache_v,
              w_in, w_out, b_igate, b_fgate, m_norm_g, conv_w, conv_b, conv_ln_g, conv_ln_b,
              sinks, ln_g, ln_b):
    xp, xs = x_prompt, x_sample
    new_p = [[] for _ in range(6)]
    new_s = [[] for _ in range(6)]
    for l in range(DEPTH):
        params = (w_in[l], w_out[l], b_igate[l], b_fgate[l], m_norm_g[l], conv_w[l], conv_b[l],
                  conv_ln_g[l], conv_ln_b[l], sinks[l])
        yp, sp = mixer_layer(xp, *params, None)
        ys, ss = mixer_layer(xs, *params, (state_C[l], state_n[l], state_m[l], state_conv[l],
                                           cache_k[l], cache_v[l]))
        xp = layer_norm(ALPHA * xp + yp, ln_g[l], ln_b[l])
        xs = layer_norm(ALPHA * xs + ys, ln_g[l], ln_b[l])
        for lst, a in zip(new_p, sp):
            lst.append(a)
        for lst, a in zip(new_s, ss):
            lst.append(a)
    new_C_p, new_n_p, new_m_p, new_conv_p, new_k_p, new_v_p = [jnp.stack(a, axis=0) for a in new_p]
    new_C_s, new_n_s, new_m_s, new_conv_s, new_k_s, new_v_s = [jnp.stack(a, axis=0) for a in new_s]
    return (xp, xs, new_C_p, new_n_p, new_m_p, new_conv_p, new_k_p, new_v_p,
            new_C_s, new_n_s, new_m_s, new_conv_s, new_k_s, new_v_s)
```

```python
import functools

import jax
import jax.numpy as jnp
from jax import lax
from jax.experimental import pallas as pl
from jax.experimental.pallas import tpu as pltpu

F32 = jnp.float32
BF16 = jnp.bfloat16

D_MODEL = 2048
DEPTH = 2
M_WIDTH = 1024
M_HEADS = 4
M_DK = 256
C_WIDTH = 512
CONV_K = 31
CONV_BUF = CONV_K - 1
A_WIDTH = 512
HEAD_DIM = 64
A_HEADS = 8
KV_HEADS = 2
GROUP = A_HEADS // KV_HEADS
KV_WIDTH = KV_HEADS * HEAD_DIM
WINDOW = 128
ALPHA = (2 * DEPTH) ** 0.25
LN_EPS = 1e-5
K_SCALE = M_DK ** -0.5
A_SCALE = HEAD_DIM ** -0.5

LANES = 128
SUBLANES = 8

HM_W = 5 * M_WIDTH
HC_W = 3 * C_WIDTH
HA_W = 2 * A_WIDTH + 2 * KV_WIDTH
MAIN_W = HM_W + HC_W + HA_W
PROJ_W = MAIN_W + LANES
GATE_OFF = 4 * M_WIDTH
HA_K = A_WIDTH
HA_V = A_WIDTH + KV_WIDTH
HA_Z = A_WIDTH + 2 * KV_WIDTH

TM_PROJ = 256
TB = 256
A_BLK = 128
CONV_HALO = 32
SB = 8
VMEM_LIMIT = 56 * 1024 * 1024


def _sigmoid(x):
    return 1.0 / (1.0 + jnp.exp(-x))


def _log_sigmoid(x):
    return jnp.minimum(x, 0.0) - jnp.log1p(jnp.exp(-jnp.abs(x)))


def _layer_norm(x, g, b):
    mu = jnp.mean(x, axis=-1, keepdims=True)
    xc = x - mu
    var = jnp.mean(xc * xc, axis=-1, keepdims=True)
    return xc * lax.rsqrt(var + LN_EPS) * g + b


def _dot(a, b):
    return jnp.dot(a, b, preferred_element_type=F32)


def _dot_nt(a, b):
    return lax.dot_general(a, b, (((1,), (1,)), ((), ())), preferred_element_type=F32)


def _dot_tn(a, b):
    return lax.dot_general(a, b, (((0,), (0,)), ((), ())), preferred_element_type=F32)


def _inproj_kernel(x_ref, w_ref, hm_ref, hc_ref, ha_ref, g_ref):
    xb = x_ref[...].astype(BF16)

    def proj(c0, c1):
        return _dot(xb, w_ref[:, c0:c1])

    step = 1024
    for c in range(0, HM_W, step):
        hm_ref[:, c:c + step] = proj(c, c + step).astype(BF16)
    for c in range(0, HC_W, C_WIDTH):
        hc_ref[:, c:c + C_WIDTH] = proj(HM_W + c, HM_W + c + C_WIDTH).astype(BF16)
    base = HM_W + HC_W
    for c0, c1 in ((0, 640), (640, HA_W)):
        ha_ref[:, c0:c1] = proj(base + c0, base + c1).astype(BF16)
    g_ref[...] = proj(MAIN_W, PROJ_W)


def _inproj(x2d, w):
    m = x2d.shape[0]
    return pl.pallas_call(
        _inproj_kernel,
        grid=(m // TM_PROJ,),
        in_specs=[
            pl.BlockSpec((TM_PROJ, D_MODEL), lambda i: (i, 0)),
            pl.BlockSpec((D_MODEL, PROJ_W), lambda i: (0, 0), pipeline_mode=pl.Buffered(1)),
        ],
        out_specs=[
            pl.BlockSpec((TM_PROJ, HM_W), lambda i: (i, 0)),
            pl.BlockSpec((TM_PROJ, HC_W), lambda i: (i, 0)),
            pl.BlockSpec((TM_PROJ, HA_W), lambda i: (i, 0)),
            pl.BlockSpec((TM_PROJ, LANES), lambda i: (i, 0)),
        ],
        out_shape=[
            jax.ShapeDtypeStruct((m, HM_W), BF16),
            jax.ShapeDtypeStruct((m, HC_W), BF16),
            jax.ShapeDtypeStruct((m, HA_W), BF16),
            jax.ShapeDtypeStruct((m, LANES), F32),
        ],
        compiler_params=pltpu.CompilerParams(
            dimension_semantics=("arbitrary",), vmem_limit_bytes=VMEM_LIMIT),
        name="inproj_prompt",
    )(x2d, w)


def _cumsum_rows(x):
    n = x.shape[0]
    row = lax.broadcasted_iota(jnp.int32, x.shape, 0)
    s = 1
    while s < n:
        x = x + jnp.where(row >= s, pltpu.roll(x, s, axis=0), 0.0)
        s *= 2
    return x


def _mixer_kernel(sinks_ref, hm_ref, hc_ref, ha_ref, g_ref, x_ref, wout_ref, gbias_ref, mng_ref,
                  convw_ref, convb_ref, clng_ref, clnb_ref, lng_ref, lnb_ref,
                  y_ref, c_ref, n_ref, m_ref, conv_ref, k_ref, v_ref,
                  abuf, ybuf, kprev, vprev, merged):
    j = pl.program_id(1)
    last = pl.num_programs(1) - 1

    @pl.when(j == 0)
    def _():
        c_ref[...] = jnp.zeros_like(c_ref)
        n_ref[...] = jnp.zeros_like(n_ref)
        m_ref[...] = jnp.zeros_like(m_ref)
        abuf[0:CONV_HALO, :] = jnp.zeros((CONV_HALO, C_WIDTH), F32)
        kprev[...] = jnp.zeros_like(kprev)
        vprev[...] = jnp.zeros_like(vprev)

    gb = g_ref[...] + gbias_ref[...]
    bsum = _cumsum_rows(_log_sigmoid(gb))
    bsum = pltpu.roll(bsum, LANES - M_HEADS, axis=1)
    a_all = gb - bsum
    a_t = a_all.T
    m_prev = m_ref[0]
    row_i = lax.broadcasted_iota(jnp.int32, (TB, TB), 0)
    col_i = lax.broadcasted_iota(jnp.int32, (TB, TB), 1)
    causal = col_i <= row_i
    lane_i = lax.broadcasted_iota(jnp.int32, (1, LANES), 1)
    m_new = m_prev

    for h in range(M_HEADS):
        cs = slice(h * M_DK, (h + 1) * M_DK)
        q = hm_ref[:, cs]
        k = hm_ref[:, M_WIDTH + h * M_DK:M_WIDTH + (h + 1) * M_DK]
        v = hm_ref[:, 2 * M_WIDTH + h * M_DK:2 * M_WIDTH + (h + 1) * M_DK]
        logit = jnp.where(causal, a_t[h:h + 1, :], -jnp.inf)
        m0 = m_prev[:, h:h + 1]
        mx = jnp.maximum(jnp.max(logit, axis=1, keepdims=True), m0)
        w = jnp.exp(logit - mx)
        w0 = jnp.exp(m0 - mx)
        s = _dot_nt(q, k) * (w * K_SCALE)
        c0 = c_ref[0, h]
        n0 = n_ref[0, h:h + 1, :]
        num = _dot(s.astype(BF16), v) + w0 * _dot(q, c0.astype(BF16))
        den = (jnp.sum(s, axis=1, keepdims=True)
               + w0 * jnp.sum(q.astype(F32) * n0, axis=1, keepdims=True))
        m_t = bsum[:, h:h + 1] + mx
        hh = num / jnp.maximum(jnp.abs(den), jnp.exp(-m_t))
        mu = jnp.mean(hh, axis=1, keepdims=True)
        hc = hh - mu
        var = jnp.mean(hc * hc, axis=1, keepdims=True)
        hn = hc * lax.rsqrt(var + LN_EPS) * mng_ref[:, cs]
        o = hm_ref[:, 3 * M_WIDTH + h * M_DK:3 * M_WIDTH + (h + 1) * M_DK].astype(F32)
        z = hm_ref[:, 4 * M_WIDTH + h * M_DK:4 * M_WIDTH + (h + 1) * M_DK].astype(F32)
        merged[:, cs] = (hn * _sigmoid(o) * (z * _sigmoid(z))).astype(BF16)
        mx_end = mx[TB - 1:TB, :]
        g_col = jnp.exp(a_all[:, h:h + 1] - mx_end) * K_SCALE
        g0 = jnp.exp(m0 - mx_end)
        kg = k.astype(F32) * g_col
        c_ref[0, h] = g0 * c0 + _dot_tn(kg.astype(BF16), v)
        n_ref[0, h:h + 1, :] = g0 * n0 + jnp.sum(kg, axis=0, keepdims=True)
        m_end = bsum[TB - 1:TB, h:h + 1] + mx_end
        m_new = jnp.where(lane_i == h, m_end, m_new)
    m_ref[0] = m_new

    cu = hc_ref[:, 0:C_WIDTH].astype(F32)
    cg = hc_ref[:, C_WIDTH:2 * C_WIDTH].astype(F32)
    abuf[CONV_HALO:CONV_HALO + TB, :] = cu * _sigmoid(cg)
    tap0 = CONV_HALO - CONV_BUF
    for r in range(0, TB, 128):
        for c in range(0, C_WIDTH, LANES):
            acc = jnp.zeros((128, LANES), F32)
            for t in range(CONV_K):
                acc = acc + (abuf[r + tap0 + t:r + tap0 + t + 128, c:c + LANES]
                             * convw_ref[t:t + 1, c:c + LANES])
            ybuf[r:r + 128, c:c + LANES] = acc + convb_ref[:, c:c + LANES]

    @pl.when(j == last)
    def _():
        conv_ref[0] = abuf[CONV_HALO + TB - CONV_BUF:CONV_HALO + TB, :]

    abuf[0:CONV_HALO, :] = abuf[TB:TB + CONV_HALO, :]
    yc = _layer_norm(ybuf[...], clng_ref[...], clnb_ref[...])
    cz = hc_ref[:, 2 * C_WIDTH:3 * C_WIDTH].astype(F32)
    merged[:, M_WIDTH:M_WIDTH + C_WIDTH] = (
        (yc * _sigmoid(yc)) * (cz * _sigmoid(cz))).astype(BF16)

    arow = lax.broadcasted_iota(jnp.int32, (A_BLK, 2 * A_BLK), 0)
    acol = lax.broadcasted_iota(jnp.int32, (A_BLK, 2 * A_BLK), 1)
    rel = A_BLK + arow - acol
    band = (rel >= 0) & (rel <= WINDOW)
    for sub in range(TB // A_BLK):
        r0 = sub * A_BLK
        if sub == 0:
            kcat = jnp.concatenate([kprev[...], ha_ref[0:A_BLK, HA_K:HA_K + KV_WIDTH]], axis=0)
            vcat = jnp.concatenate([vprev[...], ha_ref[0:A_BLK, HA_V:HA_V + KV_WIDTH]], axis=0)
            mask = band & (acol >= jnp.where(j > 0, 0, A_BLK))
        else:
            kcat = ha_ref[r0 - A_BLK:r0 + A_BLK, HA_K:HA_K + KV_WIDTH]
            vcat = ha_ref[r0 - A_BLK:r0 + A_BLK, HA_V:HA_V + KV_WIDTH]
            mask = band
        for hp in range(A_HEADS // 2):
            outs = []
            for hq in (2 * hp, 2 * hp + 1):
                kvh = hq // GROUP
                qh = ha_ref[r0:r0 + A_BLK, hq * HEAD_DIM:(hq + 1) * HEAD_DIM]
                kh = kcat[:, kvh * HEAD_DIM:(kvh + 1) * HEAD_DIM]
                vh = vcat[:, kvh * HEAD_DIM:(kvh + 1) * HEAD_DIM]
                sc = jnp.where(mask, _dot_nt(qh, kh) * A_SCALE, -jnp.inf)
                sink = sinks_ref[hq]
                smax = jnp.maximum(jnp.max(sc, axis=1, keepdims=True), sink)
                e = jnp.exp(sc - smax)
                denom = jnp.sum(e, axis=1, keepdims=True) + jnp.exp(sink - smax)
                outs.append(_dot((e / denom).astype(BF16), vh))
            ao = jnp.concatenate(outs, axis=1)
            az = ha_ref[r0:r0 + A_BLK, HA_Z + hp * LANES:HA_Z + (hp + 1) * LANES].astype(F32)
            col = M_WIDTH + C_WIDTH + hp * LANES
            merged[r0:r0 + A_BLK, col:col + LANES] = (ao * (az * _sigmoid(az))).astype(BF16)
    kprev[...] = ha_ref[TB - A_BLK:TB, HA_K:HA_K + KV_WIDTH]
    vprev[...] = ha_ref[TB - A_BLK:TB, HA_V:HA_V + KV_WIDTH]

    @pl.when(j == last)
    def _():
        k_ref[0] = ha_ref[TB - WINDOW:TB, HA_K:HA_K + KV_WIDTH].astype(F32)
        v_ref[0] = ha_ref[TB - WINDOW:TB, HA_V:HA_V + KV_WIDTH].astype(F32)

    y = _dot(merged[...], wout_ref[...])
    y_ref[...] = _layer_norm(ALPHA * x_ref[...] + y, lng_ref[...], lnb_ref[...])


def _mixer(sinks, hm, hc, ha, g, x2d, wout, gbias, mng, convw, convb, clng, clnb, lng, lnb, batch, seq):
    nb = seq // TB
    row = lambda b, j: (b * nb + j, 0)
    const = lambda b, j: (0, 0)
    per_b3 = lambda b, j: (b, 0, 0)
    return pl.pallas_call(
        _mixer_kernel,
        grid=(batch, nb),
        in_specs=[
            pl.BlockSpec(memory_space=pltpu.SMEM),
            pl.BlockSpec((TB, HM_W), row),
            pl.BlockSpec((TB, HC_W), row),
            pl.BlockSpec((TB, HA_W), row),
            pl.BlockSpec((TB, LANES), row),
            pl.BlockSpec((TB, D_MODEL), row),
            pl.BlockSpec((D_MODEL, D_MODEL), const),
            pl.BlockSpec((1, LANES), const),
            pl.BlockSpec((1, M_WIDTH), const),
            pl.BlockSpec((CONV_K + 1, C_WIDTH), const),
            pl.BlockSpec((1, C_WIDTH), const),
            pl.BlockSpec((1, C_WIDTH), const),
            pl.BlockSpec((1, C_WIDTH), const),
            pl.BlockSpec((1, D_MODEL), const),
            pl.BlockSpec((1, D_MODEL), const),
        ],
        out_specs=[
            pl.BlockSpec((TB, D_MODEL), row),
            pl.BlockSpec((1, M_HEADS, M_DK, M_DK), lambda b, j: (b, 0, 0, 0)),
            pl.BlockSpec((1, M_HEADS, M_DK), per_b3),
            pl.BlockSpec((1, 1, LANES), per_b3),
            pl.BlockSpec((1, CONV_BUF, C_WIDTH), per_b3),
            pl.BlockSpec((1, WINDOW, KV_WIDTH), per_b3),
            pl.BlockSpec((1, WINDOW, KV_WIDTH), per_b3),
        ],
        out_shape=[
            jax.ShapeDtypeStruct((batch * seq, D_MODEL), F32),
            jax.ShapeDtypeStruct((batch, M_HEADS, M_DK, M_DK), F32),
            jax.ShapeDtypeStruct((batch, M_HEADS, M_DK), F32),
            jax.ShapeDtypeStruct((batch, 1, LANES), F32),
            jax.ShapeDtypeStruct((batch, CONV_BUF, C_WIDTH), F32),
            jax.ShapeDtypeStruct((batch, WINDOW, KV_WIDTH), F32),
            jax.ShapeDtypeStruct((batch, WINDOW, KV_WIDTH), F32),
        ],
        scratch_shapes=[
            pltpu.VMEM((CONV_HALO + TB, C_WIDTH), F32),
            pltpu.VMEM((TB, C_WIDTH), F32),
            pltpu.VMEM((A_BLK, KV_WIDTH), BF16),
            pltpu.VMEM((A_BLK, KV_WIDTH), BF16),
            pltpu.VMEM((TB, D_MODEL), BF16),
        ],
        compiler_params=pltpu.CompilerParams(
            dimension_semantics=("arbitrary", "arbitrary"), vmem_limit_bytes=VMEM_LIMIT),
        name="mixer_prompt",
    )(sinks, hm, hc, ha, g, x2d, wout, gbias, mng, convw, convb, clng, clnb, lng, lnb)


SAMPLE_TN = 1152


def _inproj_sample_kernel(x_ref, w_ref, h_ref):
    h_ref[...] = _dot(x_ref[...].astype(BF16), w_ref[...])


def _inproj_sample(xs, w):
    nb = xs.shape[0]
    return pl.pallas_call(
        _inproj_sample_kernel,
        grid=(PROJ_W // SAMPLE_TN,),
        in_specs=[pl.BlockSpec((nb, D_MODEL), lambda i: (0, 0)),
                  pl.BlockSpec((D_MODEL, SAMPLE_TN), lambda i: (0, i))],
        out_specs=pl.BlockSpec((nb, SAMPLE_TN), lambda i: (0, i)),
        out_shape=jax.ShapeDtypeStruct((nb, PROJ_W), F32),
        compiler_params=pltpu.CompilerParams(dimension_semantics=("arbitrary",)),
        name="inproj_sample",
    )(xs, w)


def _sample_mixer_kernel(hs_ref, c_ref, n_ref, m_ref, conv_ref, ck_ref, cv_ref,
                         gbias_ref, mng_ref, convw_ref, convb_ref, clng_ref, clnb_ref, sink_ref,
                         merged_ref, cout_ref, nout_ref, mout_ref, convout_ref, kout_ref, vout_ref):
    gb = hs_ref[:, MAIN_W:PROJ_W] + gbias_ref[...]
    lf = pltpu.roll(_log_sigmoid(gb), LANES - M_HEADS, axis=1)
    m0 = m_ref[...]
    m_new = jnp.maximum(lf + m0, gb)
    wgt = jnp.exp(gb - m_new)
    w0 = jnp.exp(lf + m0 - m_new)
    einv = jnp.exp(-m_new)
    mout_ref[...] = m_new
    for h in range(M_HEADS):
        cs = slice(h * M_DK, (h + 1) * M_DK)
        q = hs_ref[:, cs]
        k = hs_ref[:, M_WIDTH + h * M_DK:M_WIDTH + (h + 1) * M_DK] * K_SCALE
        v = hs_ref[:, 2 * M_WIDTH + h * M_DK:2 * M_WIDTH + (h + 1) * M_DK]
        q_t = q.T
        k_t = k.T
        wh = wgt[:, h:h + 1]
        w0h = w0[:, h:h + 1]
        n0 = n_ref[:, h, :]
        s = jnp.sum(q * k, axis=1, keepdims=True) * wh
        qc_rows = []
        for b in range(SB):
            c0 = c_ref[b, h]
            qc_rows.append(jnp.sum(q_t[:, b:b + 1] * c0, axis=0, keepdims=True))
            cout_ref[b, h] = w0h[b:b + 1, :] * c0 + k_t[:, b:b + 1] * (wh[b:b + 1, :] * v[b:b + 1, :])
        qc = jnp.concatenate(qc_rows, axis=0)
        num = s * v + w0h * qc
        den = s + w0h * jnp.sum(q * n0, axis=1, keepdims=True)
        hh = num / jnp.maximum(jnp.abs(den), einv[:, h:h + 1])
        nout_ref[:, h, :] = w0h * n0 + wh * k
        mu = jnp.mean(hh, axis=1, keepdims=True)
        hc = hh - mu
        var = jnp.mean(hc * hc, axis=1, keepdims=True)
        hn = hc * lax.rsqrt(var + LN_EPS) * mng_ref[:, cs]
        o = hs_ref[:, 3 * M_WIDTH + h * M_DK:3 * M_WIDTH + (h + 1) * M_DK]
        z = hs_ref[:, 4 * M_WIDTH + h * M_DK:4 * M_WIDTH + (h + 1) * M_DK]
        merged_ref[:, cs] = hn * _sigmoid(o) * (z * _sigmoid(z))

    cu = hs_ref[:, HM_W:HM_W + C_WIDTH]
    cg = hs_ref[:, HM_W + C_WIDTH:HM_W + 2 * C_WIDTH]
    cz = hs_ref[:, HM_W + 2 * C_WIDTH:HM_W + 3 * C_WIDTH]
    a = cu * _sigmoid(cg)
    w_hist = convw_ref[0:CONV_BUF, :]
    rows = []
    for b in range(SB):
        rows.append(jnp.sum(conv_ref[b] * w_hist, axis=0, keepdims=True))
        convout_ref[b, 0:CONV_BUF - 1, :] = conv_ref[b, 1:CONV_BUF, :]
        convout_ref[b, CONV_BUF - 1:CONV_BUF, :] = a[b:b + 1, :]
    yc = jnp.concatenate(rows, axis=0) + a * convw_ref[CONV_BUF:CONV_K, :] + convb_ref[...]
    yc = _layer_norm(yc, clng_ref[...], clnb_ref[...])
    merged_ref[:, M_WIDTH:M_WIDTH + C_WIDTH] = (yc * _sigmoid(yc)) * (cz * _sigmoid(cz))

    ab = HM_W + HC_W
    lane = lax.broadcasted_iota(jnp.int32, (1, LANES), 1)
    krow = lax.broadcasted_iota(jnp.int32, (2 * WINDOW, LANES), 0)
    wrow = lax.broadcasted_iota(jnp.int32, (WINDOW, LANES), 0)
    sink_row = sink_ref[...]
    zpad = jnp.zeros((WINDOW - 1, LANES), F32)
    for b in range(SB):
        knew = hs_ref[b:b + 1, ab + HA_K:ab + HA_K + KV_WIDTH]
        vnew = hs_ref[b:b + 1, ab + HA_V:ab + HA_V + KV_WIDTH]
        ck = ck_ref[b]
        cv = cv_ref[b]
        qrows = []
        for hq in range(A_HEADS):
            kvh = hq // GROUP
            t, u = hq // 2, hq % 2
            tile = hs_ref[b:b + 1, ab + t * LANES:ab + (t + 1) * LANES]
            if u != kvh:
                tile = pltpu.roll(tile, HEAD_DIM, axis=1)
            qrows.append(jnp.where((lane // HEAD_DIM) == kvh, tile, 0.0))
        qsel = jnp.concatenate(qrows + [jnp.zeros((LANES - A_HEADS, LANES), F32)], axis=0)
        kext = jnp.concatenate([ck, knew, zpad], axis=0)
        vext = jnp.concatenate([cv, vnew, zpad], axis=0)
        sc = _dot_nt(kext.astype(BF16), qsel.astype(BF16)) * A_SCALE
        sc = jnp.where(krow <= WINDOW, sc, -jnp.inf)
        smax = jnp.maximum(jnp.max(sc, axis=0, keepdims=True), sink_row)
        e = jnp.exp(sc - smax)
        denom = jnp.sum(e, axis=0, keepdims=True) + jnp.exp(sink_row - smax)
        p = e / denom
        ob = _dot_tn(p.astype(BF16), vext.astype(BF16))
        tiles = []
        for t in range(A_HEADS // 2):
            halves = []
            for u in range(2):
                hq = 2 * t + u
                kvh = hq // GROUP
                x = ob[hq:hq + 1, :]
                if u != kvh:
                    x = pltpu.roll(x, HEAD_DIM, axis=1)
                halves.append(x)
            tiles.append(jnp.where(lane < HEAD_DIM, halves[0], halves[1]))
        ao = jnp.concatenate(tiles, axis=1)
        az = hs_ref[b:b + 1, ab + HA_Z:ab + HA_Z + A_WIDTH]
        merged_ref[b:b + 1, M_WIDTH + C_WIDTH:D_MODEL] = ao * (az * _sigmoid(az))
        kout_ref[b] = jnp.where(wrow == WINDOW - 1, knew, pltpu.roll(ck, WINDOW - 1, axis=0))
        vout_ref[b] = jnp.where(wrow == WINDOW - 1, vnew, pltpu.roll(cv, WINDOW - 1, axis=0))


def _sample_mixer(hs, c, n, m, conv, ck, cv, gbias, mng, convw, convb, clng, clnb, sink_row):
    nb = hs.shape[0]
    row2 = lambda i: (i, 0)
    row3 = lambda i: (i, 0, 0)
    row4 = lambda i: (i, 0, 0, 0)
    const = lambda i: (0, 0)
    return pl.pallas_call(
        _sample_mixer_kernel,
        grid=(nb // SB,),
        in_specs=[
            pl.BlockSpec((SB, PROJ_W), row2),
            pl.BlockSpec((SB, M_HEADS, M_DK, M_DK), row4),
            pl.BlockSpec((SB, M_HEADS, M_DK), row3),
            pl.BlockSpec((SB, LANES), row2),
            pl.BlockSpec((SB, CONV_BUF, C_WIDTH), row3),
            pl.BlockSpec((SB, WINDOW, KV_WIDTH), row3),
            pl.BlockSpec((SB, WINDOW, KV_WIDTH), row3),
            pl.BlockSpec((1, LANES), const),
            pl.BlockSpec((1, M_WIDTH), const),
            pl.BlockSpec((CONV_K + 1, C_WIDTH), const),
            pl.BlockSpec((1, C_WIDTH), const),
            pl.BlockSpec((1, C_WIDTH), const),
            pl.BlockSpec((1, C_WIDTH), const),
            pl.BlockSpec((1, LANES), const),
        ],
        out_specs=[
            pl.BlockSpec((SB, D_MODEL), row2),
            pl.BlockSpec((SB, M_HEADS, M_DK, M_DK), row4),
            pl.BlockSpec((SB, M_HEADS, M_DK), row3),
            pl.BlockSpec((SB, LANES), row2),
            pl.BlockSpec((SB, CONV_BUF, C_WIDTH), row3),
            pl.BlockSpec((SB, WINDOW, KV_WIDTH), row3),
            pl.BlockSpec((SB, WINDOW, KV_WIDTH), row3),
        ],
        out_shape=[
            jax.ShapeDtypeStruct((nb, D_MODEL), F32),
            jax.ShapeDtypeStruct((nb, M_HEADS, M_DK, M_DK), F32),
            jax.ShapeDtypeStruct((nb, M_HEADS, M_DK), F32),
            jax.ShapeDtypeStruct((nb, LANES), F32),
            jax.ShapeDtypeStruct((nb, CONV_BUF, C_WIDTH), F32),
            jax.ShapeDtypeStruct((nb, WINDOW, KV_WIDTH), F32),
            jax.ShapeDtypeStruct((nb, WINDOW, KV_WIDTH), F32),
        ],
        compiler_params=pltpu.CompilerParams(
            dimension_semantics=("arbitrary",), vmem_limit_bytes=VMEM_LIMIT),
        name="mixer_sample",
    )(hs, c, n, m, conv, ck, cv, gbias, mng, convw, convb, clng, clnb, sink_row)


def _outproj_sample_kernel(mg_ref, x_ref, wout_ref, lng_ref, lnb_ref, y_ref):
    y = _dot(mg_ref[...].astype(BF16), wout_ref[...])
    y_ref[...] = _layer_norm(ALPHA * x_ref[...] + y, lng_ref[...], lnb_ref[...])


def _outproj_sample(mg, xs, wout, lng, lnb):
    nb = xs.shape[0]
    full = lambda shape: pl.BlockSpec(shape, lambda i: (0, 0))
    return pl.pallas_call(
        _outproj_sample_kernel,
        grid=(1,),
        in_specs=[full((nb, D_MODEL)), full((nb, D_MODEL)), full((D_MODEL, D_MODEL)),
                  full((1, D_MODEL)), full((1, D_MODEL))],
        out_specs=full((nb, D_MODEL)),
        out_shape=jax.ShapeDtypeStruct((nb, D_MODEL), F32),
        compiler_params=pltpu.CompilerParams(
            dimension_semantics=("arbitrary",), vmem_limit_bytes=VMEM_LIMIT),
        name="outproj_sample",
    )(mg, xs, wout, lng, lnb)


def _regroup_w_in(w):
    pad = jnp.zeros((w.shape[0], LANES - 2 * M_HEADS), w.dtype)
    return jnp.concatenate(
        [w[:, :GATE_OFF], w[:, GATE_OFF + 2 * M_HEADS:], w[:, GATE_OFF:GATE_OFF + 2 * M_HEADS], pad],
        axis=1).astype(BF16)


def _lane_row(*vecs):
    v = jnp.concatenate([a.astype(F32) for a in vecs])
    return jnp.pad(v, (0, LANES - v.shape[0])).reshape(1, LANES)


def kernel(x_prompt, x_sample, state_C, state_n, state_m, state_conv, cache_k, cache_v, w_in, w_out,
           b_igate, b_fgate, m_norm_g, conv_w, conv_b, conv_ln_g, conv_ln_b, sinks, ln_g, ln_b):
    batch, seq, _ = x_prompt.shape
    nb = x_sample.shape[0]
    xp = x_prompt.reshape(batch * seq, D_MODEL)
    xs = x_sample.reshape(nb, D_MODEL)
    new_p = [[] for _ in range(6)]
    new_s = [[] for _ in range(6)]
    for l in range(DEPTH):
        w = _regroup_w_in(w_in[l])
        wout = w_out[l].astype(BF16)
        gbias = _lane_row(b_igate[l], b_fgate[l])
        mng = m_norm_g[l].reshape(1, M_WIDTH)
        convw = jnp.pad(conv_w[l], ((0, 1), (0, 0)))
        convb = conv_b[l].reshape(1, C_WIDTH)
        clng = conv_ln_g[l].reshape(1, C_WIDTH)
        clnb = conv_ln_b[l].reshape(1, C_WIDTH)
        lng = ln_g[l].reshape(1, D_MODEL)
        lnb = ln_b[l].reshape(1, D_MODEL)

        hm, hc, ha, g = _inproj(xp, w)
        xp, c_p, n_p, m_p, conv_p, k_p, v_p = _mixer(
            sinks[l], hm, hc, ha, g, xp, wout, gbias, mng, convw, convb, clng, clnb, lng, lnb,
            batch, seq)
        for lst, a in zip(new_p, (c_p, n_p, m_p[:, 0, :M_HEADS], conv_p,
                                  k_p.reshape(batch, WINDOW, KV_HEADS, HEAD_DIM),
                                  v_p.reshape(batch, WINDOW, KV_HEADS, HEAD_DIM))):
            lst.append(a)

        hs = _inproj_sample(xs, w)
        m_in = jnp.pad(state_m[l], ((0, 0), (0, LANES - M_HEADS)))
        mg, c_s, n_s, m_s, conv_s, k_s, v_s = _sample_mixer(
            hs, state_C[l], state_n[l], m_in, state_conv[l],
            cache_k[l].reshape(nb, WINDOW, KV_WIDTH), cache_v[l].reshape(nb, WINDOW, KV_WIDTH),
            gbias, mng, convw, convb, clng, clnb, _lane_row(sinks[l]))
        xs = _outproj_sample(mg, xs, wout, lng, lnb)
        for lst, a in zip(new_s, (c_s, n_s, m_s[:, :M_HEADS], conv_s,
                                  k_s.reshape(nb, WINDOW, KV_HEADS, HEAD_DIM),
                                  v_s.reshape(nb, WINDOW, KV_HEADS, HEAD_DIM))):
            lst.append(a)

    outs_p = [jnp.stack(a, axis=0) for a in new_p]
    outs_s = [jnp.stack(a, axis=0) for a in new_s]
    return (xp.reshape(batch, seq, D_MODEL), xs.reshape(nb, 1, D_MODEL), *outs_p, *outs_s)
```

```python
import functools

import jax
import jax.numpy as jnp
from jax import lax
from jax.experimental import pallas as pl
from jax.experimental.pallas import tpu as pltpu

F32 = jnp.float32
BF16 = jnp.bfloat16

D_MODEL = 2048
DEPTH = 2
M_WIDTH = 1024
M_HEADS = 4
M_DK = 256
C_WIDTH = 512
CONV_K = 31
CONV_BUF = CONV_K - 1
A_WIDTH = 512
HEAD_DIM = 64
A_HEADS = 8
KV_HEADS = 2
GROUP = A_HEADS // KV_HEADS
KV_WIDTH = KV_HEADS * HEAD_DIM
WINDOW = 128
ALPHA = (2 * DEPTH) ** 0.25
LN_EPS = 1e-5
K_SCALE = M_DK ** -0.5
A_SCALE = HEAD_DIM ** -0.5

LANES = 128
SUBLANES = 8

HM_W = 5 * M_WIDTH
HC_W = 3 * C_WIDTH
HA_W = 2 * A_WIDTH + 2 * KV_WIDTH
MAIN_W = HM_W + HC_W + HA_W
GATE_END = MAIN_W + LANES
PREP_COLS = 256
PROJ_W = 8192
GATE_OFF = 4 * M_WIDTH
HA_K = A_WIDTH
HA_V = A_WIDTH + KV_WIDTH
HA_Z = A_WIDTH + 2 * KV_WIDTH

TM_PROJ = 256
TB = 256
A_BLK = 128
CONV_HALO = 32
SB = 8
VMEM_LIMIT = 56 * 1024 * 1024


def _sigmoid(x):
    return 1.0 / (1.0 + jnp.exp(-x))


def _log_sigmoid(x):
    return jnp.minimum(x, 0.0) - jnp.log1p(jnp.exp(-jnp.abs(x)))


def _layer_norm(x, g, b):
    mu = jnp.mean(x, axis=-1, keepdims=True)
    xc = x - mu
    var = jnp.mean(xc * xc, axis=-1, keepdims=True)
    return xc * lax.rsqrt(var + LN_EPS) * g + b


def _dot(a, b):
    return jnp.dot(a, b, preferred_element_type=F32)


def _dot_nt(a, b):
    return lax.dot_general(a, b, (((1,), (1,)), ((), ())), preferred_element_type=F32)


def _dot_tn(a, b):
    return lax.dot_general(a, b, (((0,), (0,)), ((), ())), preferred_element_type=F32)


def _inproj_kernel(x_ref, w_ref, hm_ref, hc_ref, ha_ref, g_ref):
    xb = x_ref[...].astype(BF16)

    def proj(c0, c1):
        return _dot(xb, w_ref[:, c0:c1])

    step = 1024
    for c in range(0, HM_W, step):
        hm_ref[:, c:c + step] = proj(c, c + step).astype(BF16)
    for c in range(0, HC_W, C_WIDTH):
        hc_ref[:, c:c + C_WIDTH] = proj(HM_W + c, HM_W + c + C_WIDTH).astype(BF16)
    base = HM_W + HC_W
    for c0, c1 in ((0, 640), (640, HA_W)):
        ha_ref[:, c0:c1] = proj(base + c0, base + c1).astype(BF16)
    g_ref[...] = proj(MAIN_W, GATE_END)


def _inproj(x2d, w, layer):
    m = x2d.shape[0]
    return pl.pallas_call(
        _inproj_kernel,
        grid=(m // TM_PROJ,),
        in_specs=[
            pl.BlockSpec((TM_PROJ, D_MODEL), lambda i: (i, 0)),
            pl.BlockSpec((None, D_MODEL, PROJ_W), lambda i: (layer, 0, 0),
                         pipeline_mode=pl.Buffered(1)),
        ],
        out_specs=[
            pl.BlockSpec((TM_PROJ, HM_W), lambda i: (i, 0)),
            pl.BlockSpec((TM_PROJ, HC_W), lambda i: (i, 0)),
            pl.BlockSpec((TM_PROJ, HA_W), lambda i: (i, 0)),
            pl.BlockSpec((TM_PROJ, LANES), lambda i: (i, 0)),
        ],
        out_shape=[
            jax.ShapeDtypeStruct((m, HM_W), BF16),
            jax.ShapeDtypeStruct((m, HC_W), BF16),
            jax.ShapeDtypeStruct((m, HA_W), BF16),
            jax.ShapeDtypeStruct((m, LANES), F32),
        ],
        compiler_params=pltpu.CompilerParams(
            dimension_semantics=("arbitrary",), vmem_limit_bytes=VMEM_LIMIT),
        name="inproj_prompt",
    )(x2d, w)


def _cumsum_rows(x):
    n = x.shape[0]
    row = lax.broadcasted_iota(jnp.int32, x.shape, 0)
    s = 1
    while s < n:
        x = x + jnp.where(row >= s, pltpu.roll(x, s, axis=0), 0.0)
        s *= 2
    return x


def _mixer_kernel(layer, sinks_ref, hm_ref, hc_ref, ha_ref, g_ref, x_ref, wout_ref,
                  gbias_ref, mng_ref, convw_ref, convb_ref, clng_ref, clnb_ref, lng_ref, lnb_ref,
                  y_ref, c_ref, n_ref, m_ref, conv_ref, k_ref, v_ref,
                  abuf, ybuf, kprev, vprev, merged):
    j = pl.program_id(1)
    last = pl.num_programs(1) - 1

    @pl.when(j == 0)
    def _():
        c_ref[...] = jnp.zeros_like(c_ref)
        n_ref[...] = jnp.zeros_like(n_ref)
        m_ref[...] = jnp.zeros_like(m_ref)
        abuf[0:CONV_HALO, :] = jnp.zeros((CONV_HALO, C_WIDTH), F32)
        kprev[...] = jnp.zeros_like(kprev)
        vprev[...] = jnp.zeros_like(vprev)

    gb = g_ref[...] + gbias_ref[...]
    bsum = _cumsum_rows(_log_sigmoid(gb))
    bsum = pltpu.roll(bsum, LANES - M_HEADS, axis=1)
    a_all = gb - bsum
    a_t = a_all.T
    m_prev = m_ref[0]
    row_i = lax.broadcasted_iota(jnp.int32, (TB, TB), 0)
    col_i = lax.broadcasted_iota(jnp.int32, (TB, TB), 1)
    causal = col_i <= row_i
    lane_i = lax.broadcasted_iota(jnp.int32, (1, LANES), 1)
    m_new = m_prev

    for h in range(M_HEADS):
        cs = slice(h * M_DK, (h + 1) * M_DK)
        q = hm_ref[:, cs]
        k = hm_ref[:, M_WIDTH + h * M_DK:M_WIDTH + (h + 1) * M_DK]
        v = hm_ref[:, 2 * M_WIDTH + h * M_DK:2 * M_WIDTH + (h + 1) * M_DK]
        logit = jnp.where(causal, a_t[h:h + 1, :], -jnp.inf)
        m0 = m_prev[:, h:h + 1]
        mx = jnp.maximum(jnp.max(logit, axis=1, keepdims=True), m0)
        w = jnp.exp(logit - mx)
        w0 = jnp.exp(m0 - mx)
        s = _dot_nt(q, k) * (w * K_SCALE)
        c0 = c_ref[0, h]
        n0 = n_ref[0, h:h + 1, :]
        num = _dot(s.astype(BF16), v) + w0 * _dot(q, c0.astype(BF16))
        den = (jnp.sum(s, axis=1, keepdims=True)
               + w0 * jnp.sum(q.astype(F32) * n0, axis=1, keepdims=True))
        m_t = bsum[:, h:h + 1] + mx
        hh = num / jnp.maximum(jnp.abs(den), jnp.exp(-m_t))
        mu = jnp.mean(hh, axis=1, keepdims=True)
        hc = hh - mu
        var = jnp.mean(hc * hc, axis=1, keepdims=True)
        hn = hc * lax.rsqrt(var + LN_EPS) * mng_ref[:, cs]
        o = hm_ref[:, 3 * M_WIDTH + h * M_DK:3 * M_WIDTH + (h + 1) * M_DK].astype(F32)
        z = hm_ref[:, 4 * M_WIDTH + h * M_DK:4 * M_WIDTH + (h + 1) * M_DK].astype(F32)
        merged[:, cs] = (hn * _sigmoid(o) * (z * _sigmoid(z))).astype(BF16)
        mx_end = mx[TB - 1:TB, :]
        g_col = jnp.exp(a_all[:, h:h + 1] - mx_end) * K_SCALE
        g0 = jnp.exp(m0 - mx_end)
        kg = k.astype(F32) * g_col
        c_ref[0, h] = g0 * c0 + _dot_tn(kg.astype(BF16), v)
        n_ref[0, h:h + 1, :] = g0 * n0 + jnp.sum(kg, axis=0, keepdims=True)
        m_end = bsum[TB - 1:TB, h:h + 1] + mx_end
        m_new = jnp.where(lane_i == h, m_end, m_new)
    m_ref[0] = m_new

    cu = hc_ref[:, 0:C_WIDTH].astype(F32)
    cg = hc_ref[:, C_WIDTH:2 * C_WIDTH].astype(F32)
    abuf[CONV_HALO:CONV_HALO + TB, :] = cu * _sigmoid(cg)
    tap0 = CONV_HALO - CONV_BUF
    for r in range(0, TB, 128):
        for c in range(0, C_WIDTH, LANES):
            acc = jnp.zeros((128, LANES), F32)
            for t in range(CONV_K):
                acc = acc + (abuf[r + tap0 + t:r + tap0 + t + 128, c:c + LANES]
                             * convw_ref[t:t + 1, c:c + LANES])
            ybuf[r:r + 128, c:c + LANES] = acc + convb_ref[:, c:c + LANES]

    @pl.when(j == last)
    def _():
        conv_ref[0] = abuf[CONV_HALO + TB - CONV_BUF:CONV_HALO + TB, :]

    abuf[0:CONV_HALO, :] = abuf[TB:TB + CONV_HALO, :]
    yc = _layer_norm(ybuf[...], clng_ref[...], clnb_ref[...])
    cz = hc_ref[:, 2 * C_WIDTH:3 * C_WIDTH].astype(F32)
    merged[:, M_WIDTH:M_WIDTH + C_WIDTH] = (
        (yc * _sigmoid(yc)) * (cz * _sigmoid(cz))).astype(BF16)

    arow = lax.broadcasted_iota(jnp.int32, (A_BLK, 2 * A_BLK), 0)
    acol = lax.broadcasted_iota(jnp.int32, (A_BLK, 2 * A_BLK), 1)
    rel = A_BLK + arow - acol
    band = (rel >= 0) & (rel <= WINDOW)
    for sub in range(TB // A_BLK):
        r0 = sub * A_BLK
        if sub == 0:
            kcat = jnp.concatenate([kprev[...], ha_ref[0:A_BLK, HA_K:HA_K + KV_WIDTH]], axis=0)
            vcat = jnp.concatenate([vprev[...], ha_ref[0:A_BLK, HA_V:HA_V + KV_WIDTH]], axis=0)
            mask = band & (acol >= jnp.where(j > 0, 0, A_BLK))
        else:
            kcat = ha_ref[r0 - A_BLK:r0 + A_BLK, HA_K:HA_K + KV_WIDTH]
            vcat = ha_ref[r0 - A_BLK:r0 + A_BLK, HA_V:HA_V + KV_WIDTH]
            mask = band
        for hp in range(A_HEADS // 2):
            outs = []
            for hq in (2 * hp, 2 * hp + 1):
                kvh = hq // GROUP
                qh = ha_ref[r0:r0 + A_BLK, hq * HEAD_DIM:(hq + 1) * HEAD_DIM]
                kh = kcat[:, kvh * HEAD_DIM:(kvh + 1) * HEAD_DIM]
                vh = vcat[:, kvh * HEAD_DIM:(kvh + 1) * HEAD_DIM]
                sc = jnp.where(mask, _dot_nt(qh, kh) * A_SCALE, -jnp.inf)
                sink = sinks_ref[layer, hq]
                smax = jnp.maximum(jnp.max(sc, axis=1, keepdims=True), sink)
                e = jnp.exp(sc - smax)
                denom = jnp.sum(e, axis=1, keepdims=True) + jnp.exp(sink - smax)
                outs.append(_dot((e / denom).astype(BF16), vh))
            ao = jnp.concatenate(outs, axis=1)
            az = ha_ref[r0:r0 + A_BLK, HA_Z + hp * LANES:HA_Z + (hp + 1) * LANES].astype(F32)
            col = M_WIDTH + C_WIDTH + hp * LANES
            merged[r0:r0 + A_BLK, col:col + LANES] = (ao * (az * _sigmoid(az))).astype(BF16)
    kprev[...] = ha_ref[TB - A_BLK:TB, HA_K:HA_K + KV_WIDTH]
    vprev[...] = ha_ref[TB - A_BLK:TB, HA_V:HA_V + KV_WIDTH]

    @pl.when(j == last)
    def _():
        k_ref[0] = ha_ref[TB - WINDOW:TB, HA_K:HA_K + KV_WIDTH].astype(F32)
        v_ref[0] = ha_ref[TB - WINDOW:TB, HA_V:HA_V + KV_WIDTH].astype(F32)

    y = _dot(merged[...], wout_ref[...])
    y_ref[...] = _layer_norm(ALPHA * x_ref[...] + y, lng_ref[...], lnb_ref[...])


def _param_spec(width, layer, ngrid):
    if ngrid == 1:
        return lambda rows=1: pl.BlockSpec((None, rows, width), lambda i: (layer, 0, 0))
    return lambda rows=1: pl.BlockSpec((None, rows, width), lambda b, j: (layer, 0, 0))


def _mixer(layer, sinks, hm, hc, ha, g, x2d, wout, prm, batch, seq):
    nb = seq // TB
    row = lambda b, j: (b * nb + j, 0)
    per_b3 = lambda b, j: (b, 0, 0)
    state_shapes = [
        (batch, M_HEADS, M_DK, M_DK), (batch, M_HEADS, M_DK), (batch, 1, LANES),
        (batch, CONV_BUF, C_WIDTH), (batch, WINDOW, KV_WIDTH), (batch, WINDOW, KV_WIDTH)]
    return pl.pallas_call(
        functools.partial(_mixer_kernel, layer),
        grid=(batch, nb),
        in_specs=[
            pl.BlockSpec(memory_space=pltpu.SMEM),
            pl.BlockSpec((TB, HM_W), row),
            pl.BlockSpec((TB, HC_W), row),
            pl.BlockSpec((TB, HA_W), row),
            pl.BlockSpec((TB, LANES), row),
            pl.BlockSpec((TB, D_MODEL), row),
            _param_spec(D_MODEL, layer, 2)(D_MODEL),
            _param_spec(LANES, layer, 2)(),
            _param_spec(M_WIDTH, layer, 2)(),
            _param_spec(C_WIDTH, layer, 2)(CONV_K + 1),
            _param_spec(C_WIDTH, layer, 2)(),
            _param_spec(C_WIDTH, layer, 2)(),
            _param_spec(C_WIDTH, layer, 2)(),
            _param_spec(D_MODEL, layer, 2)(),
            _param_spec(D_MODEL, layer, 2)(),
        ],
        out_specs=[
            pl.BlockSpec((TB, D_MODEL), row),
            pl.BlockSpec((1, M_HEADS, M_DK, M_DK), lambda b, j: (b, 0, 0, 0)),
            pl.BlockSpec((1, M_HEADS, M_DK), per_b3),
            pl.BlockSpec((1, 1, LANES), per_b3),
            pl.BlockSpec((1, CONV_BUF, C_WIDTH), per_b3),
            pl.BlockSpec((1, WINDOW, KV_WIDTH), per_b3),
            pl.BlockSpec((1, WINDOW, KV_WIDTH), per_b3),
        ],
        out_shape=[jax.ShapeDtypeStruct((batch * seq, D_MODEL), F32)]
        + [jax.ShapeDtypeStruct(s, F32) for s in state_shapes],
        scratch_shapes=[
            pltpu.VMEM((CONV_HALO + TB, C_WIDTH), F32),
            pltpu.VMEM((TB, C_WIDTH), F32),
            pltpu.VMEM((A_BLK, KV_WIDTH), BF16),
            pltpu.VMEM((A_BLK, KV_WIDTH), BF16),
            pltpu.VMEM((TB, D_MODEL), BF16),
        ],
        compiler_params=pltpu.CompilerParams(
            dimension_semantics=("arbitrary", "arbitrary"), vmem_limit_bytes=VMEM_LIMIT),
        name="mixer_prompt",
    )(sinks, hm, hc, ha, g, x2d, wout, prm["gbias"], prm["mng"], prm["convw"], prm["convb"],
      prm["clng"], prm["clnb"], prm["lng"], prm["lnb"])


SAMPLE_TN = 1024


def _inproj_sample_kernel(x_ref, w_ref, h_ref):
    h_ref[...] = _dot(x_ref[...].astype(BF16), w_ref[...])


def _inproj_sample(xs, w, layer):
    nb = xs.shape[0]
    return pl.pallas_call(
        _inproj_sample_kernel,
        grid=(PROJ_W // SAMPLE_TN,),
        in_specs=[pl.BlockSpec((nb, D_MODEL), lambda i: (0, 0)),
                  pl.BlockSpec((None, D_MODEL, SAMPLE_TN), lambda i: (layer, 0, i))],
        out_specs=pl.BlockSpec((nb, SAMPLE_TN), lambda i: (0, i)),
        out_shape=jax.ShapeDtypeStruct((nb, PROJ_W), F32),
        compiler_params=pltpu.CompilerParams(dimension_semantics=("arbitrary",)),
        name="inproj_sample",
    )(xs, w)


def _sample_mixer_kernel(hs_ref, c_ref, n_ref, m_ref, conv_ref, ck_ref, cv_ref,
                         gbias_ref, mng_ref, convw_ref, convb_ref, clng_ref, clnb_ref, sink_ref,
                         merged_ref, cout_ref, nout_ref, mout_ref, convout_ref, kout_ref, vout_ref):
    gb = hs_ref[:, MAIN_W:GATE_END] + gbias_ref[...]
    lf = pltpu.roll(_log_sigmoid(gb), LANES - M_HEADS, axis=1)
    m0 = m_ref[...]
    m_new = jnp.maximum(lf + m0, gb)
    wgt = jnp.exp(gb - m_new)
    w0 = jnp.exp(lf + m0 - m_new)
    einv = jnp.exp(-m_new)
    mout_ref[...] = m_new
    for h in range(M_HEADS):
        cs = slice(h * M_DK, (h + 1) * M_DK)
        q = hs_ref[:, cs]
        k = hs_ref[:, M_WIDTH + h * M_DK:M_WIDTH + (h + 1) * M_DK] * K_SCALE
        v = hs_ref[:, 2 * M_WIDTH + h * M_DK:2 * M_WIDTH + (h + 1) * M_DK]
        q_t = q.T
        k_t = k.T
        wh = wgt[:, h:h + 1]
        w0h = w0[:, h:h + 1]
        n0 = n_ref[:, h, :]
        s = jnp.sum(q * k, axis=1, keepdims=True) * wh
        qc_rows = []
        for b in range(SB):
            c0 = c_ref[b, h]
            qc_rows.append(jnp.sum(q_t[:, b:b + 1] * c0, axis=0, keepdims=True))
            cout_ref[b, h] = w0h[b:b + 1, :] * c0 + k_t[:, b:b + 1] * (wh[b:b + 1, :] * v[b:b + 1, :])
        qc = jnp.concatenate(qc_rows, axis=0)
        num = s * v + w0h * qc
        den = s + w0h * jnp.sum(q * n0, axis=1, keepdims=True)
        hh = num / jnp.maximum(jnp.abs(den), einv[:, h:h + 1])
        nout_ref[:, h, :] = w0h * n0 + wh * k
        mu = jnp.mean(hh, axis=1, keepdims=True)
        hc = hh - mu
        var = jnp.mean(hc * hc, axis=1, keepdims=True)
        hn = hc * lax.rsqrt(var + LN_EPS) * mng_ref[:, cs]
        o = hs_ref[:, 3 * M_WIDTH + h * M_DK:3 * M_WIDTH + (h + 1) * M_DK]
        z = hs_ref[:, 4 * M_WIDTH + h * M_DK:4 * M_WIDTH + (h + 1) * M_DK]
        merged_ref[:, cs] = hn * _sigmoid(o) * (z * _sigmoid(z))

    cu = hs_ref[:, HM_W:HM_W + C_WIDTH]
    cg = hs_ref[:, HM_W + C_WIDTH:HM_W + 2 * C_WIDTH]
    cz = hs_ref[:, HM_W + 2 * C_WIDTH:HM_W + 3 * C_WIDTH]
    a = cu * _sigmoid(cg)
    yc = a * convw_ref[CONV_BUF:CONV_K, :] + convb_ref[...]
    for t in range(CONV_BUF):
        yc = yc + conv_ref[t] * convw_ref[t:t + 1, :]
    for t in range(CONV_BUF - 1):
        convout_ref[t] = conv_ref[t + 1]
    convout_ref[CONV_BUF - 1] = a
    yc = _layer_norm(yc, clng_ref[...], clnb_ref[...])
    merged_ref[:, M_WIDTH:M_WIDTH + C_WIDTH] = (yc * _sigmoid(yc)) * (cz * _sigmoid(cz))

    ab = HM_W + HC_W
    lane = lax.broadcasted_iota(jnp.int32, (1, LANES), 1)
    krow = lax.broadcasted_iota(jnp.int32, (2 * WINDOW, LANES), 0)
    wrow = lax.broadcasted_iota(jnp.int32, (WINDOW, LANES), 0)
    sink_row = sink_ref[...]
    zpad = jnp.zeros((WINDOW - 1, LANES), F32)
    for b in range(SB):
        knew = hs_ref[b:b + 1, ab + HA_K:ab + HA_K + KV_WIDTH]
        vnew = hs_ref[b:b + 1, ab + HA_V:ab + HA_V + KV_WIDTH]
        ck = ck_ref[b]
        cv = cv_ref[b]
        qrows = []
        for hq in range(A_HEADS):
            kvh = hq // GROUP
            t, u = hq // 2, hq % 2
            tile = hs_ref[b:b + 1, ab + t * LANES:ab + (t + 1) * LANES]
            if u != kvh:
                tile = pltpu.roll(tile, HEAD_DIM, axis=1)
            qrows.append(jnp.where((lane // HEAD_DIM) == kvh, tile, 0.0))
        qsel = jnp.concatenate(qrows + [jnp.zeros((LANES - A_HEADS, LANES), F32)], axis=0)
        kext = jnp.concatenate([ck, knew, zpad], axis=0)
        vext = jnp.concatenate([cv, vnew, zpad], axis=0)
        sc = _dot_nt(kext.astype(BF16), qsel.astype(BF16)) * A_SCALE
        sc = jnp.where(krow <= WINDOW, sc, -jnp.inf)
        smax = jnp.maximum(jnp.max(sc, axis=0, keepdims=True), sink_row)
        e = jnp.exp(sc - smax)
        denom = jnp.sum(e, axis=0, keepdims=True) + jnp.exp(sink_row - smax)
        p = e / denom
        ob = _dot_tn(p.astype(BF16), vext.astype(BF16))
        tiles = []
        for t in range(A_HEADS // 2):
            halves = []
            for u in range(2):
                hq = 2 * t + u
                kvh = hq // GROUP
                x = ob[hq:hq + 1, :]
                if u != kvh:
                    x = pltpu.roll(x, HEAD_DIM, axis=1)
                halves.append(x)
            tiles.append(jnp.where(lane < HEAD_DIM, halves[0], halves[1]))
        ao = jnp.concatenate(tiles, axis=1)
        az = hs_ref[b:b + 1, ab + HA_Z:ab + HA_Z + A_WIDTH]
        merged_ref[b:b + 1, M_WIDTH + C_WIDTH:D_MODEL] = ao * (az * _sigmoid(az))
        kout_ref[b] = jnp.where(wrow == WINDOW - 1, knew, pltpu.roll(ck, WINDOW - 1, axis=0))
        vout_ref[b] = jnp.where(wrow == WINDOW - 1, vnew, pltpu.roll(cv, WINDOW - 1, axis=0))


def _sample_mixer(layer, hs, c, n, m, conv, ck, cv, prm):
    nb = hs.shape[0]
    row2 = lambda i: (i, 0)

    def state_specs(lead, pre):
        return [
            pl.BlockSpec(lead + (SB, M_HEADS, M_DK, M_DK), lambda i: pre + (i, 0, 0, 0)),
            pl.BlockSpec(lead + (SB, M_HEADS, M_DK), lambda i: pre + (i, 0, 0)),
            pl.BlockSpec(lead + (SB, LANES), lambda i: pre + (i, 0)),
            pl.BlockSpec(lead + (CONV_BUF, SB, C_WIDTH), lambda i: pre + (0, i, 0)),
            pl.BlockSpec(lead + (SB, WINDOW, KV_WIDTH), lambda i: pre + (i, 0, 0)),
            pl.BlockSpec(lead + (SB, WINDOW, KV_WIDTH), lambda i: pre + (i, 0, 0)),
        ]

    state_shapes = [
        (nb, M_HEADS, M_DK, M_DK), (nb, M_HEADS, M_DK), (nb, LANES),
        (CONV_BUF, nb, C_WIDTH), (nb, WINDOW, KV_WIDTH), (nb, WINDOW, KV_WIDTH)]
    return pl.pallas_call(
        _sample_mixer_kernel,
        grid=(nb // SB,),
        in_specs=[pl.BlockSpec((SB, PROJ_W), row2)] + state_specs((None,), (layer,)) + [
            _param_spec(LANES, layer, 1)(),
            _param_spec(M_WIDTH, layer, 1)(),
            _param_spec(C_WIDTH, layer, 1)(CONV_K + 1),
            _param_spec(C_WIDTH, layer, 1)(),
            _param_spec(C_WIDTH, layer, 1)(),
            _param_spec(C_WIDTH, layer, 1)(),
            _param_spec(LANES, layer, 1)(),
        ],
        out_specs=[pl.BlockSpec((SB, D_MODEL), row2)] + state_specs((), ()),
        out_shape=[jax.ShapeDtypeStruct((nb, D_MODEL), F32)]
        + [jax.ShapeDtypeStruct(s, F32) for s in state_shapes],
        compiler_params=pltpu.CompilerParams(
            dimension_semantics=("arbitrary",), vmem_limit_bytes=VMEM_LIMIT),
        name="mixer_sample",
    )(hs, c, n, m, conv, ck, cv, prm["gbias"], prm["mng"], prm["convw"], prm["convb"],
      prm["clng"], prm["clnb"], prm["sink_row"])


def _outproj_sample_kernel(mg_ref, x_ref, wout_ref, lng_ref, lnb_ref, y_ref):
    y = _dot(mg_ref[...].astype(BF16), wout_ref[...])
    y_ref[...] = _layer_norm(ALPHA * x_ref[...] + y, lng_ref[...], lnb_ref[...])


def _outproj_sample(mg, xs, wout, prm, layer):
    nb = xs.shape[0]
    full = lambda shape: pl.BlockSpec(shape, lambda i: (0, 0))
    return pl.pallas_call(
        _outproj_sample_kernel,
        grid=(1,),
        in_specs=[full((nb, D_MODEL)), full((nb, D_MODEL)),
                  _param_spec(D_MODEL, layer, 1)(D_MODEL),
                  _param_spec(D_MODEL, layer, 1)(), _param_spec(D_MODEL, layer, 1)()],
        out_specs=full((nb, D_MODEL)),
        out_shape=jax.ShapeDtypeStruct((nb, D_MODEL), F32),
        compiler_params=pltpu.CompilerParams(
            dimension_semantics=("arbitrary",), vmem_limit_bytes=VMEM_LIMIT),
        name="outproj_sample",
    )(mg, xs, wout, prm["lng"], prm["lnb"])


N_GATES = 2 * M_HEADS
PLAIN_BLOCKS = GATE_OFF // PREP_COLS
GATE_BLOCK = MAIN_W // PREP_COLS


def _regroup_kernel(wt_ref, o_ref):
    j = pl.program_id(1)
    blk = wt_ref[0]
    row = lax.broadcasted_iota(jnp.int32, blk.shape, 0)
    blk = jnp.where(row < jnp.where(j == GATE_BLOCK, N_GATES, PREP_COLS), blk, 0.0)
    o_ref[...] = blk.T.astype(BF16)


def _regroup_w_in(w_in_t):
    depth, _, d = w_in_t.shape

    def src_col(l, j):
        per_blk = PREP_COLS // SUBLANES
        start = jnp.where(j < PLAIN_BLOCKS, j * per_blk,
                          jnp.where(j < GATE_BLOCK, j * per_blk + N_GATES // SUBLANES,
                                    GATE_OFF // SUBLANES))
        return (l, start * SUBLANES, 0)

    return pl.pallas_call(
        _regroup_kernel,
        grid=(depth, PROJ_W // PREP_COLS),
        in_specs=[pl.BlockSpec((pl.Element(1), pl.Element(PREP_COLS), pl.Element(d)), src_col)],
        out_specs=pl.BlockSpec((None, d, PREP_COLS), lambda l, j: (l, 0, j)),
        out_shape=jax.ShapeDtypeStruct((depth, d, PROJ_W), BF16),
        compiler_params=pltpu.CompilerParams(dimension_semantics=("arbitrary", "arbitrary")),
        name="regroup_w_in",
    )(w_in_t)


def _cast_kernel(w_ref, o_ref):
    o_ref[...] = w_ref[...].astype(BF16)


def _cast_w_out(w_out):
    depth, d, n = w_out.shape
    rows = 512
    return pl.pallas_call(
        _cast_kernel,
        grid=(depth, d // rows),
        in_specs=[pl.BlockSpec((None, rows, n), lambda l, i: (l, i, 0))],
        out_specs=pl.BlockSpec((None, rows, n), lambda l, i: (l, i, 0)),
        out_shape=jax.ShapeDtypeStruct((depth, d, n), BF16),
        compiler_params=pltpu.CompilerParams(dimension_semantics=("arbitrary", "arbitrary")),
        name="cast_w_out",
    )(w_out)


def _lane_rows(*mats):
    v = jnp.concatenate([a.astype(F32) for a in mats], axis=1)
    return jnp.pad(v, ((0, 0), (0, LANES - v.shape[1])))[:, None, :]


def kernel(x_prompt, x_sample, state_C, state_n, state_m, state_conv, cache_k, cache_v, w_in, w_out,
           b_igate, b_fgate, m_norm_g, conv_w, conv_b, conv_ln_g, conv_ln_b, sinks, ln_g, ln_b):
    batch, seq, _ = x_prompt.shape
    nb = x_sample.shape[0]
    xp = x_prompt.reshape(batch * seq, D_MODEL)
    xs = x_sample.reshape(nb, D_MODEL)
    w = _regroup_w_in(jnp.swapaxes(w_in, 1, 2))
    wout = _cast_w_out(w_out)
    prm = {
        "gbias": _lane_rows(b_igate, b_fgate),
        "sink_row": _lane_rows(sinks),
        "mng": m_norm_g[:, None, :],
        "convw": jnp.pad(conv_w, ((0, 0), (0, 1), (0, 0))),
        "convb": conv_b[:, None, :],
        "clng": conv_ln_g[:, None, :],
        "clnb": conv_ln_b[:, None, :],
        "lng": ln_g[:, None, :],
        "lnb": ln_b[:, None, :],
    }
    m_in = jnp.pad(state_m, ((0, 0), (0, 0), (0, LANES - M_HEADS)))
    sconv = jnp.swapaxes(state_conv, 1, 2)
    ck = cache_k.reshape(DEPTH, nb, WINDOW, KV_WIDTH)
    cv = cache_v.reshape(DEPTH, nb, WINDOW, KV_WIDTH)
    st_p = []
    st_s = []
    for l in range(DEPTH):
        hm, hc, ha, g = _inproj(xp, w, l)
        xp, *states = _mixer(l, sinks, hm, hc, ha, g, xp, wout, prm, batch, seq)
        st_p.append(states)
        hs = _inproj_sample(xs, w, l)
        mg, *states = _sample_mixer(l, hs, state_C, state_n, m_in, sconv, ck, cv, prm)
        st_s.append(states)
        xs = _outproj_sample(mg, xs, wout, prm, l)

    def finish(per_layer, n, conv_tap_major):
        c, nn, m, conv, k, v = (jnp.stack(a, axis=0) for a in zip(*per_layer))
        if conv_tap_major:
            conv = jnp.swapaxes(conv, 1, 2)
        return (c, nn, m.reshape(DEPTH, n, LANES)[:, :, :M_HEADS], conv,
                k.reshape(DEPTH, n, WINDOW, KV_HEADS, HEAD_DIM),
                v.reshape(DEPTH, n, WINDOW, KV_HEADS, HEAD_DIM))

    return (xp.reshape(batch, seq, D_MODEL), xs.reshape(nb, 1, D_MODEL),
            *finish(st_p, batch, False), *finish(st_s, nb, True))
```

```python
import functools

import jax
import jax.numpy as jnp
from jax import lax
from jax.experimental import pallas as pl
from jax.experimental.pallas import tpu as pltpu

F32 = jnp.float32
BF16 = jnp.bfloat16

D_MODEL = 2048
DEPTH = 2
M_WIDTH = 1024
M_HEADS = 4
M_DK = 256
C_WIDTH = 512
CONV_K = 31
CONV_BUF = CONV_K - 1
A_WIDTH = 512
HEAD_DIM = 64
A_HEADS = 8
KV_HEADS = 2
GROUP = A_HEADS // KV_HEADS
KV_WIDTH = KV_HEADS * HEAD_DIM
WINDOW = 128
ALPHA = (2 * DEPTH) ** 0.25
LN_EPS = 1e-5
K_SCALE = M_DK ** -0.5
A_SCALE = HEAD_DIM ** -0.5

LANES = 128
SUBLANES = 8

HM_W = 5 * M_WIDTH
HC_W = 3 * C_WIDTH
HA_W = 2 * A_WIDTH + 2 * KV_WIDTH
MAIN_W = HM_W + HC_W + HA_W
GATE_END = MAIN_W + LANES
PREP_COLS = 256
PROJ_W = 8192
GATE_OFF = 4 * M_WIDTH
HA_K = A_WIDTH
HA_V = A_WIDTH + KV_WIDTH
HA_Z = A_WIDTH + 2 * KV_WIDTH

TM_PROJ = 256
TB = 256
A_BLK = 128
CONV_HALO = 32
SB = 8
VMEM_LIMIT = 56 * 1024 * 1024


def _sigmoid(x):
    return 1.0 / (1.0 + jnp.exp(-x))


def _log_sigmoid(x):
    return jnp.minimum(x, 0.0) - jnp.log1p(jnp.exp(-jnp.abs(x)))


def _layer_norm(x, g, b):
    mu = jnp.mean(x, axis=-1, keepdims=True)
    xc = x - mu
    var = jnp.mean(xc * xc, axis=-1, keepdims=True)
    return xc * lax.rsqrt(var + LN_EPS) * g + b


def _dot(a, b):
    return jnp.dot(a, b, preferred_element_type=F32)


def _dot_nt(a, b):
    return lax.dot_general(a, b, (((1,), (1,)), ((), ())), preferred_element_type=F32)


def _dot_tn(a, b):
    return lax.dot_general(a, b, (((0,), (0,)), ((), ())), preferred_element_type=F32)


PM_W = 4 * M_WIDTH
PA_QKV = A_WIDTH + 2 * KV_WIDTH
SHIFTED_ROWS = CONV_HALO + TM_PROJ - SUBLANES


def _silu(x):
    return x * _sigmoid(x)


def _inproj_kernel(blocks_per_seq, x_ref, wt_ref, convw_ref, convb_ref, clng_ref, clnb_ref,
                   pm_ref, pc_ref, pa_ref, g_ref, conv_ref, xbuf, abuf, sbuf, ybuf, zbuf):
    i = pl.program_id(0)

    @pl.when(i % blocks_per_seq == 0)
    def _():
        abuf[0:CONV_HALO, :] = jnp.zeros((CONV_HALO, C_WIDTH), F32)

    xbuf[...] = x_ref[...].astype(BF16)

    def proj(c0, c1):
        return _dot_nt(xbuf[...], wt_ref[c0:c1, :])

    for c in range(0, 3 * M_WIDTH, M_WIDTH):
        pm_ref[:, c:c + M_WIDTH] = proj(c, c + M_WIDTH).astype(BF16)
    for h in range(M_HEADS):
        o = proj(3 * M_WIDTH + h * M_DK, 3 * M_WIDTH + (h + 1) * M_DK)
        z = proj(4 * M_WIDTH + h * M_DK, 4 * M_WIDTH + (h + 1) * M_DK)
        pm_ref[:, 3 * M_WIDTH + h * M_DK:3 * M_WIDTH + (h + 1) * M_DK] = (
            _sigmoid(o) * _silu(z)).astype(BF16)

    cu = proj(HM_W, HM_W + C_WIDTH)
    cg = proj(HM_W + C_WIDTH, HM_W + 2 * C_WIDTH)
    abuf[CONV_HALO:CONV_HALO + TM_PROJ, :] = cu * _sigmoid(cg)
    zbuf[...] = _silu(proj(HM_W + 2 * C_WIDTH, HM_W + 3 * C_WIDTH))
    for p in range(1, SUBLANES):
        sbuf[p - 1] = abuf[p:p + SHIFTED_ROWS, :]
    tap0 = CONV_HALO - CONV_BUF
    for r in range(0, TM_PROJ, 128):
        for c in range(0, C_WIDTH, LANES):
            acc = jnp.zeros((128, LANES), F32)
            for t in range(CONV_K):
                p = (tap0 + t) % SUBLANES
                base = r + tap0 + t - p
                src = abuf if p == 0 else sbuf.at[p - 1]
                acc = acc + src[base:base + 128, c:c + LANES] * convw_ref[t:t + 1, c:c + LANES]
            ybuf[r:r + 128, c:c + LANES] = acc + convb_ref[:, c:c + LANES]
    conv_ref[0] = abuf[CONV_HALO + TM_PROJ - CONV_BUF:CONV_HALO + TM_PROJ, :]
    abuf[0:CONV_HALO, :] = abuf[TM_PROJ:TM_PROJ + CONV_HALO, :]
    yc = _layer_norm(ybuf[...], clng_ref[...], clnb_ref[...])
    pc_ref[...] = (_silu(yc) * zbuf[...]).astype(BF16)

    base = HM_W + HC_W
    pa_ref[:, 0:PA_QKV] = proj(base, base + PA_QKV).astype(BF16)
    pa_ref[:, PA_QKV:HA_W] = _silu(proj(base + PA_QKV, base + HA_W)).astype(BF16)
    g_ref[...] = proj(MAIN_W, GATE_END)


def _inproj(x2d, wt, prm, layer, batch):
    m = x2d.shape[0]
    nblk = m // TM_PROJ
    row = lambda i: (i, 0)
    return pl.pallas_call(
        functools.partial(_inproj_kernel, nblk // batch),
        grid=(nblk,),
        in_specs=[
            pl.BlockSpec((TM_PROJ, D_MODEL), row),
            pl.BlockSpec((None, PROJ_W, D_MODEL), lambda i: (layer, 0, 0),
                         pipeline_mode=pl.Buffered(1)),
            _param_spec(C_WIDTH, layer, 1)(CONV_K + 1),
            _param_spec(C_WIDTH, layer, 1)(),
            _param_spec(C_WIDTH, layer, 1)(),
            _param_spec(C_WIDTH, layer, 1)(),
        ],
        out_specs=[
            pl.BlockSpec((TM_PROJ, PM_W), row),
            pl.BlockSpec((TM_PROJ, C_WIDTH), row),
            pl.BlockSpec((TM_PROJ, HA_W), row),
            pl.BlockSpec((TM_PROJ, LANES), row),
            pl.BlockSpec((1, CONV_BUF, C_WIDTH), lambda i: (i // (nblk // batch), 0, 0)),
        ],
        out_shape=[
            jax.ShapeDtypeStruct((m, PM_W), BF16),
            jax.ShapeDtypeStruct((m, C_WIDTH), BF16),
            jax.ShapeDtypeStruct((m, HA_W), BF16),
            jax.ShapeDtypeStruct((m, LANES), F32),
            jax.ShapeDtypeStruct((batch, CONV_BUF, C_WIDTH), F32),
        ],
        scratch_shapes=[
            pltpu.VMEM((TM_PROJ, D_MODEL), BF16),
            pltpu.VMEM((CONV_HALO + TM_PROJ, C_WIDTH), F32),
            pltpu.VMEM((SUBLANES - 1, SHIFTED_ROWS, C_WIDTH), F32),
            pltpu.VMEM((TM_PROJ, C_WIDTH), F32),
            pltpu.VMEM((TM_PROJ, C_WIDTH), F32),
        ],
        compiler_params=pltpu.CompilerParams(
            dimension_semantics=("arbitrary",), vmem_limit_bytes=VMEM_LIMIT),
        name="inproj_prompt",
    )(x2d, wt, prm["convw"], prm["convb"], prm["clng"], prm["clnb"])


def _cumsum_rows(x):
    n = x.shape[0]
    row = lax.broadcasted_iota(jnp.int32, x.shape, 0)
    s = 1
    while s < n:
        x = x + jnp.where(row >= s, pltpu.roll(x, s, axis=0), 0.0)
        s *= 2
    return x


def _mixer_kernel(layer, sinks_ref, hm_ref, pc_ref, ha_ref, g_ref, x_ref, wout_ref,
                  gbias_ref, mng_ref, lng_ref, lnb_ref,
                  y_ref, c_ref, n_ref, m_ref, k_ref, v_ref,
                  kprev, vprev, merged):
    j = pl.program_id(1)

    @pl.when(j == 0)
    def _():
        c_ref[...] = jnp.zeros_like(c_ref)
        n_ref[...] = jnp.zeros_like(n_ref)
        m_ref[...] = jnp.zeros_like(m_ref)
        kprev[...] = jnp.zeros_like(kprev)
        vprev[...] = jnp.zeros_like(vprev)

    gb = g_ref[...] + gbias_ref[...]
    bsum = _cumsum_rows(_log_sigmoid(gb))
    bsum = pltpu.roll(bsum, LANES - M_HEADS, axis=1)
    a_all = gb - bsum
    a_t = a_all.T
    m_prev = m_ref[0]
    row_i = lax.broadcasted_iota(jnp.int32, (TB, TB), 0)
    col_i = lax.broadcasted_iota(jnp.int32, (TB, TB), 1)
    causal = col_i <= row_i
    lane_i = lax.broadcasted_iota(jnp.int32, (1, LANES), 1)
    m_new = m_prev

    for h in range(M_HEADS):
        cs = slice(h * M_DK, (h + 1) * M_DK)
        q = hm_ref[:, cs]
        k = hm_ref[:, M_WIDTH + h * M_DK:M_WIDTH + (h + 1) * M_DK]
        v = hm_ref[:, 2 * M_WIDTH + h * M_DK:2 * M_WIDTH + (h + 1) * M_DK]
        logit = jnp.where(causal, a_t[h:h + 1, :], -jnp.inf)
        m0 = m_prev[:, h:h + 1]
        mx = jnp.maximum(jnp.max(logit, axis=1, keepdims=True), m0)
        w = jnp.exp(logit - mx)
        w0 = jnp.exp(m0 - mx)
        s = _dot_nt(q, k) * (w * K_SCALE)
        c0 = c_ref[0, h]
        n0 = n_ref[0, h:h + 1, :]
        num = _dot(s.astype(BF16), v) + w0 * _dot(q, c0.astype(BF16))
        den = (jnp.sum(s, axis=1, keepdims=True)
               + w0 * jnp.sum(q.astype(F32) * n0, axis=1, keepdims=True))
        m_t = bsum[:, h:h + 1] + mx
        hh = num / jnp.maximum(jnp.abs(den), jnp.exp(-m_t))
        mu = jnp.mean(hh, axis=1, keepdims=True)
        hc = hh - mu
        var = jnp.mean(hc * hc, axis=1, keepdims=True)
        hn = hc * lax.rsqrt(var + LN_EPS) * mng_ref[:, cs]
        gate = hm_ref[:, 3 * M_WIDTH + h * M_DK:3 * M_WIDTH + (h + 1) * M_DK].astype(F32)
        merged[:, cs] = (hn * gate).astype(BF16)
        mx_end = mx[TB - 1:TB, :]
        g_col = jnp.exp(a_all[:, h:h + 1] - mx_end) * K_SCALE
        g0 = jnp.exp(m0 - mx_end)
        kg = k.astype(F32) * g_col
        c_ref[0, h] = g0 * c0 + _dot_tn(kg.astype(BF16), v)
        n_ref[0, h:h + 1, :] = g0 * n0 + jnp.sum(kg, axis=0, keepdims=True)
        m_end = bsum[TB - 1:TB, h:h + 1] + mx_end
        m_new = jnp.where(lane_i == h, m_end, m_new)
    m_ref[0] = m_new

    merged[:, M_WIDTH:M_WIDTH + C_WIDTH] = pc_ref[...]

    arow = lax.broadcasted_iota(jnp.int32, (A_BLK, 2 * A_BLK), 0)
    acol = lax.broadcasted_iota(jnp.int32, (A_BLK, 2 * A_BLK), 1)
    rel = A_BLK + arow - acol
    band = (rel >= 0) & (rel <= WINDOW)
    for sub in range(TB // A_BLK):
        r0 = sub * A_BLK
        if sub == 0:
            kcat = jnp.concatenate([kprev[...], ha_ref[0:A_BLK, HA_K:HA_K + KV_WIDTH]], axis=0)
            vcat = jnp.concatenate([vprev[...], ha_ref[0:A_BLK, HA_V:HA_V + KV_WIDTH]], axis=0)
            mask = band & (acol >= jnp.where(j > 0, 0, A_BLK))
        else:
            kcat = ha_ref[r0 - A_BLK:r0 + A_BLK, HA_K:HA_K + KV_WIDTH]
            vcat = ha_ref[r0 - A_BLK:r0 + A_BLK, HA_V:HA_V + KV_WIDTH]
            mask = band
        for hp in range(A_HEADS // 2):
            outs = []
            for hq in (2 * hp, 2 * hp + 1):
                kvh = hq // GROUP
                qh = ha_ref[r0:r0 + A_BLK, hq * HEAD_DIM:(hq + 1) * HEAD_DIM]
                kh = kcat[:, kvh * HEAD_DIM:(kvh + 1) * HEAD_DIM]
                vh = vcat[:, kvh * HEAD_DIM:(kvh + 1) * HEAD_DIM]
                sc = jnp.where(mask, _dot_nt(qh, kh) * A_SCALE, -jnp.inf)
                sink = sinks_ref[layer, hq]
                smax = jnp.maximum(jnp.max(sc, axis=1, keepdims=True), sink)
                e = jnp.exp(sc - smax)
                denom = jnp.sum(e, axis=1, keepdims=True) + jnp.exp(sink - smax)
                outs.append(_dot((e / denom).astype(BF16), vh))
            ao = jnp.concatenate(outs, axis=1)
            gate = ha_ref[r0:r0 + A_BLK, HA_Z + hp * LANES:HA_Z + (hp + 1) * LANES].astype(F32)
            col = M_WIDTH + C_WIDTH + hp * LANES
            merged[r0:r0 + A_BLK, col:col + LANES] = (ao * gate).astype(BF16)
    kprev[...] = ha_ref[TB - A_BLK:TB, HA_K:HA_K + KV_WIDTH]
    vprev[...] = ha_ref[TB - A_BLK:TB, HA_V:HA_V + KV_WIDTH]
    k_ref[0] = ha_ref[TB - WINDOW:TB, HA_K:HA_K + KV_WIDTH].astype(F32)
    v_ref[0] = ha_ref[TB - WINDOW:TB, HA_V:HA_V + KV_WIDTH].astype(F32)

    y = _dot(merged[...], wout_ref[...])
    y_ref[...] = _layer_norm(ALPHA * x_ref[...] + y, lng_ref[...], lnb_ref[...])


def _param_spec(width, layer, ngrid):
    if ngrid == 1:
        return lambda rows=1: pl.BlockSpec((None, rows, width), lambda i: (layer, 0, 0))
    return lambda rows=1: pl.BlockSpec((None, rows, width), lambda b, j: (layer, 0, 0))


def _mixer(layer, sinks, pm, pc, pa, g, x2d, wout, prm, batch, seq):
    nb = seq // TB
    row = lambda b, j: (b * nb + j, 0)
    per_b3 = lambda b, j: (b, 0, 0)
    state_shapes = [
        (batch, M_HEADS, M_DK, M_DK), (batch, M_HEADS, M_DK), (batch, 1, LANES),
        (batch, WINDOW, KV_WIDTH), (batch, WINDOW, KV_WIDTH)]
    return pl.pallas_call(
        functools.partial(_mixer_kernel, layer),
        grid=(batch, nb),
        in_specs=[
            pl.BlockSpec(memory_space=pltpu.SMEM),
            pl.BlockSpec((TB, PM_W), row),
            pl.BlockSpec((TB, C_WIDTH), row),
            pl.BlockSpec((TB, HA_W), row),
            pl.BlockSpec((TB, LANES), row),
            pl.BlockSpec((TB, D_MODEL), row),
            _param_spec(D_MODEL, layer, 2)(D_MODEL),
            _param_spec(LANES, layer, 2)(),
            _param_spec(M_WIDTH, layer, 2)(),
            _param_spec(D_MODEL, layer, 2)(),
            _param_spec(D_MODEL, layer, 2)(),
        ],
        out_specs=[
            pl.BlockSpec((TB, D_MODEL), row),
            pl.BlockSpec((1, M_HEADS, M_DK, M_DK), lambda b, j: (b, 0, 0, 0)),
            pl.BlockSpec((1, M_HEADS, M_DK), per_b3),
            pl.BlockSpec((1, 1, LANES), per_b3),
            pl.BlockSpec((1, WINDOW, KV_WIDTH), per_b3),
            pl.BlockSpec((1, WINDOW, KV_WIDTH), per_b3),
        ],
        out_shape=[jax.ShapeDtypeStruct((batch * seq, D_MODEL), F32)]
        + [jax.ShapeDtypeStruct(s, F32) for s in state_shapes],
        scratch_shapes=[
            pltpu.VMEM((A_BLK, KV_WIDTH), BF16),
            pltpu.VMEM((A_BLK, KV_WIDTH), BF16),
            pltpu.VMEM((TB, D_MODEL), BF16),
        ],
        compiler_params=pltpu.CompilerParams(
            dimension_semantics=("arbitrary", "arbitrary"), vmem_limit_bytes=VMEM_LIMIT),
        name="mixer_prompt",
    )(sinks, pm, pc, pa, g, x2d, wout, prm["gbias"], prm["mng"], prm["lng"], prm["lnb"])


SAMPLE_TN = 1024


def _inproj_sample_kernel(x_ref, wt_ref, h_ref):
    h_ref[...] = _dot_nt(x_ref[...].astype(BF16), wt_ref[...])


def _inproj_sample(xs, wt, layer):
    nb = xs.shape[0]
    return pl.pallas_call(
        _inproj_sample_kernel,
        grid=(PROJ_W // SAMPLE_TN,),
        in_specs=[pl.BlockSpec((nb, D_MODEL), lambda i: (0, 0)),
                  pl.BlockSpec((None, SAMPLE_TN, D_MODEL), lambda i: (layer, i, 0))],
        out_specs=pl.BlockSpec((nb, SAMPLE_TN), lambda i: (0, i)),
        out_shape=jax.ShapeDtypeStruct((nb, PROJ_W), F32),
        compiler_params=pltpu.CompilerParams(dimension_semantics=("arbitrary",)),
        name="inproj_sample",
    )(xs, wt)


def _sample_mixer_kernel(hs_ref, c_ref, n_ref, m_ref, conv_ref, ck_ref, cv_ref,
                         gbias_ref, mng_ref, convw_ref, convb_ref, clng_ref, clnb_ref, sink_ref,
                         merged_ref, cout_ref, nout_ref, mout_ref, convout_ref, kout_ref, vout_ref):
    gb = hs_ref[:, MAIN_W:GATE_END] + gbias_ref[...]
    lf = pltpu.roll(_log_sigmoid(gb), LANES - M_HEADS, axis=1)
    m0 = m_ref[...]
    m_new = jnp.maximum(lf + m0, gb)
    wgt = jnp.exp(gb - m_new)
    w0 = jnp.exp(lf + m0 - m_new)
    einv = jnp.exp(-m_new)
    mout_ref[...] = m_new
    for h in range(M_HEADS):
        cs = slice(h * M_DK, (h + 1) * M_DK)
        q = hs_ref[:, cs]
        k = hs_ref[:, M_WIDTH + h * M_DK:M_WIDTH + (h + 1) * M_DK] * K_SCALE
        v = hs_ref[:, 2 * M_WIDTH + h * M_DK:2 * M_WIDTH + (h + 1) * M_DK]
        q_t = q.T
        k_t = k.T
        wh = wgt[:, h:h + 1]
        w0h = w0[:, h:h + 1]
        n0 = n_ref[:, h, :]
        s = jnp.sum(q * k, axis=1, keepdims=True) * wh
        qc_rows = []
        for b in range(SB):
            c0 = c_ref[b, h]
            qc_rows.append(jnp.sum(q_t[:, b:b + 1] * c0, axis=0, keepdims=True))
            cout_ref[b, h] = w0h[b:b + 1, :] * c0 + k_t[:, b:b + 1] * (wh[b:b + 1, :] * v[b:b + 1, :])
        qc = jnp.concatenate(qc_rows, axis=0)
        num = s * v + w0h * qc
        den = s + w0h * jnp.sum(q * n0, axis=1, keepdims=True)
        hh = num / jnp.maximum(jnp.abs(den), einv[:, h:h + 1])
        nout_ref[:, h, :] = w0h * n0 + wh * k
        mu = jnp.mean(hh, axis=1, keepdims=True)
        hc = hh - mu
        var = jnp.mean(hc * hc, axis=1, keepdims=True)
        hn = hc * lax.rsqrt(var + LN_EPS) * mng_ref[:, cs]
        o = hs_ref[:, 3 * M_WIDTH + h * M_DK:3 * M_WIDTH + (h + 1) * M_DK]
        z = hs_ref[:, 4 * M_WIDTH + h * M_DK:4 * M_WIDTH + (h + 1) * M_DK]
        merged_ref[:, cs] = hn * _sigmoid(o) * (z * _sigmoid(z))

    cu = hs_ref[:, HM_W:HM_W + C_WIDTH]
    cg = hs_ref[:, HM_W + C_WIDTH:HM_W + 2 * C_WIDTH]
    cz = hs_ref[:, HM_W + 2 * C_WIDTH:HM_W + 3 * C_WIDTH]
    a = cu * _sigmoid(cg)
    yc = a * convw_ref[CONV_BUF:CONV_K, :] + convb_ref[...]
    for t in range(CONV_BUF):
        yc = yc + conv_ref[t] * convw_ref[t:t + 1, :]
    for t in range(CONV_BUF - 1):
        convout_ref[t] = conv_ref[t + 1]
    convout_ref[CONV_BUF - 1] = a
    yc = _layer_norm(yc, clng_ref[...], clnb_ref[...])
    merged_ref[:, M_WIDTH:M_WIDTH + C_WIDTH] = (yc * _sigmoid(yc)) * (cz * _sigmoid(cz))

    ab = HM_W + HC_W
    lane = lax.broadcasted_iota(jnp.int32, (1, LANES), 1)
    krow = lax.broadcasted_iota(jnp.int32, (2 * WINDOW, LANES), 0)
    wrow = lax.broadcasted_iota(jnp.int32, (WINDOW, LANES), 0)
    sink_row = sink_ref[...]
    zpad = jnp.zeros((WINDOW - 1, LANES), F32)
    for b in range(SB):
        knew = hs_ref[b:b + 1, ab + HA_K:ab + HA_K + KV_WIDTH]
        vnew = hs_ref[b:b + 1, ab + HA_V:ab + HA_V + KV_WIDTH]
        ck = ck_ref[b]
        cv = cv_ref[b]
        qrows = []
        for hq in range(A_HEADS):
            kvh = hq // GROUP
            t, u = hq // 2, hq % 2
            tile = hs_ref[b:b + 1, ab + t * LANES:ab + (t + 1) * LANES]
            if u != kvh:
                tile = pltpu.roll(tile, HEAD_DIM, axis=1)
            qrows.append(jnp.where((lane // HEAD_DIM) == kvh, tile, 0.0))
        qsel = jnp.concatenate(qrows + [jnp.zeros((LANES - A_HEADS, LANES), F32)], axis=0)
        kext = jnp.concatenate([ck, knew, zpad], axis=0)
        vext = jnp.concatenate([cv, vnew, zpad], axis=0)
        sc = _dot_nt(kext.astype(BF16), qsel.astype(BF16)) * A_SCALE
        sc = jnp.where(krow <= WINDOW, sc, -jnp.inf)
        smax = jnp.maximum(jnp.max(sc, axis=0, keepdims=True), sink_row)
        e = jnp.exp(sc - smax)
        denom = jnp.sum(e, axis=0, keepdims=True) + jnp.exp(sink_row - smax)
        p = e / denom
        ob = _dot_tn(p.astype(BF16), vext.astype(BF16))
        tiles = []
        for t in range(A_HEADS // 2):
            halves = []
            for u in range(2):
                hq = 2 * t + u
                kvh = hq // GROUP
                x = ob[hq:hq + 1, :]
                if u != kvh:
                    x = pltpu.roll(x, HEAD_DIM, axis=1)
                halves.append(x)
            tiles.append(jnp.where(lane < HEAD_DIM, halves[0], halves[1]))
        ao = jnp.concatenate(tiles, axis=1)
        az = hs_ref[b:b + 1, ab + HA_Z:ab + HA_Z + A_WIDTH]
        merged_ref[b:b + 1, M_WIDTH + C_WIDTH:D_MODEL] = ao * (az * _sigmoid(az))
        kout_ref[b] = jnp.where(wrow == WINDOW - 1, knew, pltpu.roll(ck, WINDOW - 1, axis=0))
        vout_ref[b] = jnp.where(wrow == WINDOW - 1, vnew, pltpu.roll(cv, WINDOW - 1, axis=0))


def _sample_mixer(layer, hs, c, n, m, conv, ck, cv, prm):
    nb = hs.shape[0]
    row2 = lambda i: (i, 0)

    def state_specs(lead, pre):
        return [
            pl.BlockSpec(lead + (SB, M_HEADS, M_DK, M_DK), lambda i: pre + (i, 0, 0, 0)),
            pl.BlockSpec(lead + (SB, M_HEADS, M_DK), lambda i: pre + (i, 0, 0)),
            pl.BlockSpec(lead + (SB, LANES), lambda i: pre + (i, 0)),
            pl.BlockSpec(lead + (CONV_BUF, SB, C_WIDTH), lambda i: pre + (0, i, 0)),
            pl.BlockSpec(lead + (SB, WINDOW, KV_WIDTH), lambda i: pre + (i, 0, 0)),
            pl.BlockSpec(lead + (SB, WINDOW, KV_WIDTH), lambda i: pre + (i, 0, 0)),
        ]

    state_shapes = [
        (nb, M_HEADS, M_DK, M_DK), (nb, M_HEADS, M_DK), (nb, LANES),
        (CONV_BUF, nb, C_WIDTH), (nb, WINDOW, KV_WIDTH), (nb, WINDOW, KV_WIDTH)]
    return pl.pallas_call(
        _sample_mixer_kernel,
        grid=(nb // SB,),
        in_specs=[pl.BlockSpec((SB, PROJ_W), row2)] + state_specs((None,), (layer,)) + [
            _param_spec(LANES, layer, 1)(),
            _param_spec(M_WIDTH, layer, 1)(),
            _param_spec(C_WIDTH, layer, 1)(CONV_K + 1),
            _param_spec(C_WIDTH, layer, 1)(),
            _param_spec(C_WIDTH, layer, 1)(),
            _param_spec(C_WIDTH, layer, 1)(),
            _param_spec(LANES, layer, 1)(),
        ],
        out_specs=[pl.BlockSpec((SB, D_MODEL), row2)] + state_specs((), ()),
        out_shape=[jax.ShapeDtypeStruct((nb, D_MODEL), F32)]
        + [jax.ShapeDtypeStruct(s, F32) for s in state_shapes],
        compiler_params=pltpu.CompilerParams(
            dimension_semantics=("arbitrary",), vmem_limit_bytes=VMEM_LIMIT),
        name="mixer_sample",
    )(hs, c, n, m, conv, ck, cv, prm["gbias"], prm["mng"], prm["convw"], prm["convb"],
      prm["clng"], prm["clnb"], prm["sink_row"])


def _outproj_sample_kernel(mg_ref, x_ref, wout_ref, lng_ref, lnb_ref, y_ref):
    y = _dot(mg_ref[...].astype(BF16), wout_ref[...])
    y_ref[...] = _layer_norm(ALPHA * x_ref[...] + y, lng_ref[...], lnb_ref[...])


def _outproj_sample(mg, xs, wout, prm, layer):
    nb = xs.shape[0]
    full = lambda shape: pl.BlockSpec(shape, lambda i: (0, 0))
    return pl.pallas_call(
        _outproj_sample_kernel,
        grid=(1,),
        in_specs=[full((nb, D_MODEL)), full((nb, D_MODEL)),
                  _param_spec(D_MODEL, layer, 1)(D_MODEL),
                  _param_spec(D_MODEL, layer, 1)(), _param_spec(D_MODEL, layer, 1)()],
        out_specs=full((nb, D_MODEL)),
        out_shape=jax.ShapeDtypeStruct((nb, D_MODEL), F32),
        compiler_params=pltpu.CompilerParams(
            dimension_semantics=("arbitrary",), vmem_limit_bytes=VMEM_LIMIT),
        name="outproj_sample",
    )(mg, xs, wout, prm["lng"], prm["lnb"])


N_GATES = 2 * M_HEADS
PLAIN_BLOCKS = GATE_OFF // PREP_COLS
GATE_BLOCK = MAIN_W // PREP_COLS


def _regroup_kernel(wt_ref, o_ref):
    j = pl.program_id(1)
    blk = wt_ref[0]
    row = lax.broadcasted_iota(jnp.int32, blk.shape, 0)
    blk = jnp.where(row < jnp.where(j == GATE_BLOCK, N_GATES, PREP_COLS), blk, 0.0)
    o_ref[...] = blk.astype(BF16)


def _regroup_w_in(w_in_t):
    depth, _, d = w_in_t.shape

    def src_col(l, j):
        per_blk = PREP_COLS // SUBLANES
        start = jnp.where(j < PLAIN_BLOCKS, j * per_blk,
                          jnp.where(j < GATE_BLOCK, j * per_blk + N_GATES // SUBLANES,
                                    GATE_OFF // SUBLANES))
        return (l, start * SUBLANES, 0)

    return pl.pallas_call(
        _regroup_kernel,
        grid=(depth, PROJ_W // PREP_COLS),
        in_specs=[pl.BlockSpec((pl.Element(1), pl.Element(PREP_COLS), pl.Element(d)), src_col)],
        out_specs=pl.BlockSpec((None, PREP_COLS, d), lambda l, j: (l, j, 0)),
        out_shape=jax.ShapeDtypeStruct((depth, PROJ_W, d), BF16),
        compiler_params=pltpu.CompilerParams(dimension_semantics=("arbitrary", "arbitrary")),
        name="regroup_w_in",
    )(w_in_t)


def _cast_kernel(w_ref, o_ref):
    o_ref[...] = w_ref[...].astype(BF16)


def _cast_w_out(w_out):
    depth, d, n = w_out.shape
    rows = 512
    return pl.pallas_call(
        _cast_kernel,
        grid=(depth, d // rows),
        in_specs=[pl.BlockSpec((None, rows, n), lambda l, i: (l, i, 0))],
        out_specs=pl.BlockSpec((None, rows, n), lambda l, i: (l, i, 0)),
        out_shape=jax.ShapeDtypeStruct((depth, d, n), BF16),
        compiler_params=pltpu.CompilerParams(dimension_semantics=("arbitrary", "arbitrary")),
        name="cast_w_out",
    )(w_out)


def _lane_rows(*mats):
    v = jnp.concatenate([a.astype(F32) for a in mats], axis=1)
    return jnp.pad(v, ((0, 0), (0, LANES - v.shape[1])))[:, None, :]


def kernel(x_prompt, x_sample, state_C, state_n, state_m, state_conv, cache_k, cache_v, w_in, w_out,
           b_igate, b_fgate, m_norm_g, conv_w, conv_b, conv_ln_g, conv_ln_b, sinks, ln_g, ln_b):
    batch, seq, _ = x_prompt.shape
    nb = x_sample.shape[0]
    xp = x_prompt.reshape(batch * seq, D_MODEL)
    xs = x_sample.reshape(nb, D_MODEL)
    wt = _regroup_w_in(jnp.swapaxes(w_in, 1, 2))
    wout = _cast_w_out(w_out)
    prm = {
        "gbias": _lane_rows(b_igate, b_fgate),
        "sink_row": _lane_rows(sinks),
        "mng": m_norm_g[:, None, :],
        "convw": jnp.pad(conv_w, ((0, 0), (0, 1), (0, 0))),
        "convb": conv_b[:, None, :],
        "clng": conv_ln_g[:, None, :],
        "clnb": conv_ln_b[:, None, :],
        "lng": ln_g[:, None, :],
        "lnb": ln_b[:, None, :],
    }
    m_in = jnp.pad(state_m, ((0, 0), (0, 0), (0, LANES - M_HEADS)))
    sconv = jnp.swapaxes(state_conv, 1, 2)
    ck = cache_k.reshape(DEPTH, nb, WINDOW, KV_WIDTH)
    cv = cache_v.reshape(DEPTH, nb, WINDOW, KV_WIDTH)
    st_p = []
    st_s = []
    for l in range(DEPTH):
        pm, pc, pa, g, conv_p = _inproj(xp, wt, prm, l, batch)
        xp, c_p, n_p, m_p, k_p, v_p = _mixer(l, sinks, pm, pc, pa, g, xp, wout, prm, batch, seq)
        st_p.append((c_p, n_p, m_p, conv_p, k_p, v_p))
        hs = _inproj_sample(xs, wt, l)
        mg, *states = _sample_mixer(l, hs, state_C, state_n, m_in, sconv, ck, cv, prm)
        st_s.append(states)
        xs = _outproj_sample(mg, xs, wout, prm, l)

    def finish(per_layer, n, conv_tap_major):
        c, nn, m, conv, k, v = (jnp.stack(a, axis=0) for a in zip(*per_layer))
        if conv_tap_major:
            conv = jnp.swapaxes(conv, 1, 2)
        return (c, nn, m.reshape(DEPTH, n, LANES)[:, :, :M_HEADS], conv,
                k.reshape(DEPTH, n, WINDOW, KV_HEADS, HEAD_DIM),
                v.reshape(DEPTH, n, WINDOW, KV_HEADS, HEAD_DIM))

    return (xp.reshape(batch, seq, D_MODEL), xs.reshape(nb, 1, D_MODEL),
            *finish(st_p, batch, False), *finish(st_s, nb, True))
```

```python
import functools

import jax
import jax.numpy as jnp
from jax import lax
from jax.experimental import pallas as pl
from jax.experimental.pallas import tpu as pltpu

F32 = jnp.float32
BF16 = jnp.bfloat16

D_MODEL = 2048
DEPTH = 2
M_WIDTH = 1024
M_HEADS = 4
M_DK = 256
C_WIDTH = 512
CONV_K = 31
CONV_BUF = CONV_K - 1
A_WIDTH = 512
HEAD_DIM = 64
A_HEADS = 8
KV_HEADS = 2
GROUP = A_HEADS // KV_HEADS
KV_WIDTH = KV_HEADS * HEAD_DIM
WINDOW = 128
ALPHA = (2 * DEPTH) ** 0.25
LN_EPS = 1e-5
K_SCALE = M_DK ** -0.5
A_SCALE = HEAD_DIM ** -0.5

LANES = 128
SUBLANES = 8

HM_W = 5 * M_WIDTH
HC_W = 3 * C_WIDTH
HA_W = 2 * A_WIDTH + 2 * KV_WIDTH
MAIN_W = HM_W + HC_W + HA_W
GATE_END = MAIN_W + LANES
PREP_COLS = 256
PROJ_W = 8192
GATE_OFF = 4 * M_WIDTH
HA_K = A_WIDTH
HA_V = A_WIDTH + KV_WIDTH
HA_Z = A_WIDTH + 2 * KV_WIDTH

TM_PROJ = 256
TB = 256
A_BLK = 128
CONV_HALO = 32
SB = 8
VMEM_LIMIT = 56 * 1024 * 1024


def _sigmoid(x):
    return 1.0 / (1.0 + jnp.exp(-x))


def _log_sigmoid(x):
    return jnp.minimum(x, 0.0) - jnp.log1p(jnp.exp(-jnp.abs(x)))


def _layer_norm(x, g, b):
    mu = jnp.mean(x, axis=-1, keepdims=True)
    xc = x - mu
    var = jnp.mean(xc * xc, axis=-1, keepdims=True)
    return xc * lax.rsqrt(var + LN_EPS) * g + b


def _dot(a, b):
    return jnp.dot(a, b, preferred_element_type=F32)


def _dot_nt(a, b):
    return lax.dot_general(a, b, (((1,), (1,)), ((), ())), preferred_element_type=F32)


def _dot_tn(a, b):
    return lax.dot_general(a, b, (((0,), (0,)), ((), ())), preferred_element_type=F32)


PM_W = 4 * M_WIDTH
PA_QKV = A_WIDTH + 2 * KV_WIDTH
SHIFTED_ROWS = CONV_HALO + TM_PROJ - SUBLANES


def _silu(x):
    return x * _sigmoid(x)


def _inproj_kernel(blocks_per_seq, x_ref, xs_ref, wt_ref, convw_ref, convb_ref, clng_ref, clnb_ref,
                   pm_ref, pc_ref, pa_ref, g_ref, conv_ref, hs_ref, xbuf, abuf, sbuf, ybuf, zbuf):
    i = pl.program_id(0)

    @pl.when(i == 0)
    def _():
        hs_ref[...] = _dot_nt(xs_ref[...].astype(BF16), wt_ref[...])

    @pl.when(i % blocks_per_seq == 0)
    def _():
        abuf[0:CONV_HALO, :] = jnp.zeros((CONV_HALO, C_WIDTH), F32)

    xbuf[...] = x_ref[...].astype(BF16)

    def proj(c0, c1):
        return _dot_nt(xbuf[...], wt_ref[c0:c1, :])

    for c in range(0, 3 * M_WIDTH, M_WIDTH):
        pm_ref[:, c:c + M_WIDTH] = proj(c, c + M_WIDTH).astype(BF16)
    for h in range(M_HEADS):
        o = proj(3 * M_WIDTH + h * M_DK, 3 * M_WIDTH + (h + 1) * M_DK)
        z = proj(4 * M_WIDTH + h * M_DK, 4 * M_WIDTH + (h + 1) * M_DK)
        pm_ref[:, 3 * M_WIDTH + h * M_DK:3 * M_WIDTH + (h + 1) * M_DK] = (
            _sigmoid(o) * _silu(z)).astype(BF16)

    cu = proj(HM_W, HM_W + C_WIDTH)
    cg = proj(HM_W + C_WIDTH, HM_W + 2 * C_WIDTH)
    abuf[CONV_HALO:CONV_HALO + TM_PROJ, :] = cu * _sigmoid(cg)
    zbuf[...] = _silu(proj(HM_W + 2 * C_WIDTH, HM_W + 3 * C_WIDTH))
    for p in range(1, SUBLANES):
        sbuf[p - 1] = abuf[p:p + SHIFTED_ROWS, :]
    tap0 = CONV_HALO - CONV_BUF
    for r in range(0, TM_PROJ, 128):
        for c in range(0, C_WIDTH, LANES):
            acc = jnp.zeros((128, LANES), F32)
            for t in range(CONV_K):
                p = (tap0 + t) % SUBLANES
                base = r + tap0 + t - p
                src = abuf if p == 0 else sbuf.at[p - 1]
                acc = acc + src[base:base + 128, c:c + LANES] * convw_ref[t:t + 1, c:c + LANES]
            ybuf[r:r + 128, c:c + LANES] = acc + convb_ref[:, c:c + LANES]
    conv_ref[0] = abuf[CONV_HALO + TM_PROJ - CONV_BUF:CONV_HALO + TM_PROJ, :]
    abuf[0:CONV_HALO, :] = abuf[TM_PROJ:TM_PROJ + CONV_HALO, :]
    yc = _layer_norm(ybuf[...], clng_ref[...], clnb_ref[...])
    pc_ref[...] = (_silu(yc) * zbuf[...]).astype(BF16)

    base = HM_W + HC_W
    pa_ref[:, 0:PA_QKV] = proj(base, base + PA_QKV).astype(BF16)
    pa_ref[:, PA_QKV:HA_W] = _silu(proj(base + PA_QKV, base + HA_W)).astype(BF16)
    g_ref[...] = proj(MAIN_W, GATE_END)


def _inproj(x2d, xs, wt, prm, layer, batch):
    m = x2d.shape[0]
    ns = xs.shape[0]
    nblk = m // TM_PROJ
    row = lambda i: (i, 0)
    once = lambda i: (0, 0)
    return pl.pallas_call(
        functools.partial(_inproj_kernel, nblk // batch),
        grid=(nblk,),
        in_specs=[
            pl.BlockSpec((TM_PROJ, D_MODEL), row),
            pl.BlockSpec((ns, D_MODEL), once),
            pl.BlockSpec((None, PROJ_W, D_MODEL), lambda i: (layer, 0, 0),
                         pipeline_mode=pl.Buffered(1)),
            _param_spec(C_WIDTH, layer, 1)(CONV_K + 1),
            _param_spec(C_WIDTH, layer, 1)(),
            _param_spec(C_WIDTH, layer, 1)(),
            _param_spec(C_WIDTH, layer, 1)(),
        ],
        out_specs=[
            pl.BlockSpec((TM_PROJ, PM_W), row),
            pl.BlockSpec((TM_PROJ, C_WIDTH), row),
            pl.BlockSpec((TM_PROJ, HA_W), row),
            pl.BlockSpec((TM_PROJ, LANES), row),
            pl.BlockSpec((1, CONV_BUF, C_WIDTH), lambda i: (i // (nblk // batch), 0, 0)),
            pl.BlockSpec((ns, PROJ_W), once),
        ],
        out_shape=[
            jax.ShapeDtypeStruct((m, PM_W), BF16),
            jax.ShapeDtypeStruct((m, C_WIDTH), BF16),
            jax.ShapeDtypeStruct((m, HA_W), BF16),
            jax.ShapeDtypeStruct((m, LANES), F32),
            jax.ShapeDtypeStruct((batch, CONV_BUF, C_WIDTH), F32),
            jax.ShapeDtypeStruct((ns, PROJ_W), F32),
        ],
        scratch_shapes=[
            pltpu.VMEM((TM_PROJ, D_MODEL), BF16),
            pltpu.VMEM((CONV_HALO + TM_PROJ, C_WIDTH), F32),
            pltpu.VMEM((SUBLANES - 1, SHIFTED_ROWS, C_WIDTH), F32),
            pltpu.VMEM((TM_PROJ, C_WIDTH), F32),
            pltpu.VMEM((TM_PROJ, C_WIDTH), F32),
        ],
        compiler_params=pltpu.CompilerParams(
            dimension_semantics=("arbitrary",), vmem_limit_bytes=VMEM_LIMIT),
        name="inproj_prompt",
    )(x2d, xs, wt, prm["convw"], prm["convb"], prm["clng"], prm["clnb"])


def _cumsum_rows(x):
    n = x.shape[0]
    row = lax.broadcasted_iota(jnp.int32, x.shape, 0)
    s = 1
    while s < n:
        x = x + jnp.where(row >= s, pltpu.roll(x, s, axis=0), 0.0)
        s *= 2
    return x


def _mixer_kernel(layer, sinks_ref, hm_ref, pc_ref, ha_ref, g_ref, x_ref, wout_ref,
                  gbias_ref, mng_ref, lng_ref, lnb_ref,
                  y_ref, c_ref, n_ref, m_ref, k_ref, v_ref,
                  kprev, vprev, merged):
    j = pl.program_id(1)

    @pl.when(j == 0)
    def _():
        c_ref[...] = jnp.zeros_like(c_ref)
        n_ref[...] = jnp.zeros_like(n_ref)
        m_ref[...] = jnp.zeros_like(m_ref)
        kprev[...] = jnp.zeros_like(kprev)
        vprev[...] = jnp.zeros_like(vprev)

    gb = g_ref[...] + gbias_ref[...]
    bsum = _cumsum_rows(_log_sigmoid(gb))
    bsum = pltpu.roll(bsum, LANES - M_HEADS, axis=1)
    a_all = gb - bsum
    a_t = a_all.T
    m_prev = m_ref[0]
    row_i = lax.broadcasted_iota(jnp.int32, (TB, TB), 0)
    col_i = lax.broadcasted_iota(jnp.int32, (TB, TB), 1)
    causal = col_i <= row_i
    lane_i = lax.broadcasted_iota(jnp.int32, (1, LANES), 1)
    m_new = m_prev

    for h in range(M_HEADS):
        cs = slice(h * M_DK, (h + 1) * M_DK)
        q = hm_ref[:, cs]
        k = hm_ref[:, M_WIDTH + h * M_DK:M_WIDTH + (h + 1) * M_DK]
        v = hm_ref[:, 2 * M_WIDTH + h * M_DK:2 * M_WIDTH + (h + 1) * M_DK]
        logit = jnp.where(causal, a_t[h:h + 1, :], -jnp.inf)
        m0 = m_prev[:, h:h + 1]
        mx = jnp.maximum(jnp.max(logit, axis=1, keepdims=True), m0)
        w = jnp.exp(logit - mx)
        w0 = jnp.exp(m0 - mx)
        s = _dot_nt(q, k) * (w * K_SCALE)
        c0 = c_ref[0, h]
        n0 = n_ref[0, h:h + 1, :]
        num = _dot(s.astype(BF16), v) + w0 * _dot(q, c0.astype(BF16))
        den = (jnp.sum(s, axis=1, keepdims=True)
               + w0 * jnp.sum(q.astype(F32) * n0, axis=1, keepdims=True))
        m_t = bsum[:, h:h + 1] + mx
        hh = num / jnp.maximum(jnp.abs(den), jnp.exp(-m_t))
        mu = jnp.mean(hh, axis=1, keepdims=True)
        hc = hh - mu
        var = jnp.mean(hc * hc, axis=1, keepdims=True)
        hn = hc * lax.rsqrt(var + LN_EPS) * mng_ref[:, cs]
        gate = hm_ref[:, 3 * M_WIDTH + h * M_DK:3 * M_WIDTH + (h + 1) * M_DK].astype(F32)
        merged[:, cs] = (hn * gate).astype(BF16)
        mx_end = mx[TB - 1:TB, :]
        g_col = jnp.exp(a_all[:, h:h + 1] - mx_end) * K_SCALE
        g0 = jnp.exp(m0 - mx_end)
        kg = k.astype(F32) * g_col
        c_ref[0, h] = g0 * c0 + _dot_tn(kg.astype(BF16), v)
        n_ref[0, h:h + 1, :] = g0 * n0 + jnp.sum(kg, axis=0, keepdims=True)
        m_end = bsum[TB - 1:TB, h:h + 1] + mx_end
        m_new = jnp.where(lane_i == h, m_end, m_new)
    m_ref[0] = m_new

    merged[:, M_WIDTH:M_WIDTH + C_WIDTH] = pc_ref[...]

    arow = lax.broadcasted_iota(jnp.int32, (A_BLK, 2 * A_BLK), 0)
    acol = lax.broadcasted_iota(jnp.int32, (A_BLK, 2 * A_BLK), 1)
    rel = A_BLK + arow - acol
    band = (rel >= 0) & (rel <= WINDOW)
    for sub in range(TB // A_BLK):
        r0 = sub * A_BLK
        if sub == 0:
            kcat = jnp.concatenate([kprev[...], ha_ref[0:A_BLK, HA_K:HA_K + KV_WIDTH]], axis=0)
            vcat = jnp.concatenate([vprev[...], ha_ref[0:A_BLK, HA_V:HA_V + KV_WIDTH]], axis=0)
            mask = band & (acol >= jnp.where(j > 0, 0, A_BLK))
        else:
            kcat = ha_ref[r0 - A_BLK:r0 + A_BLK, HA_K:HA_K + KV_WIDTH]
            vcat = ha_ref[r0 - A_BLK:r0 + A_BLK, HA_V:HA_V + KV_WIDTH]
            mask = band
        for hp in range(A_HEADS // 2):
            outs = []
            for hq in (2 * hp, 2 * hp + 1):
                kvh = hq // GROUP
                qh = ha_ref[r0:r0 + A_BLK, hq * HEAD_DIM:(hq + 1) * HEAD_DIM]
                kh = kcat[:, kvh * HEAD_DIM:(kvh + 1) * HEAD_DIM]
                vh = vcat[:, kvh * HEAD_DIM:(kvh + 1) * HEAD_DIM]
                sc = jnp.where(mask, _dot_nt(qh, kh) * A_SCALE, -jnp.inf)
                sink = sinks_ref[layer, hq]
                smax = jnp.maximum(jnp.max(sc, axis=1, keepdims=True), sink)
                e = jnp.exp(sc - smax)
                denom = jnp.sum(e, axis=1, keepdims=True) + jnp.exp(sink - smax)
                outs.append(_dot((e / denom).astype(BF16), vh))
            ao = jnp.concatenate(outs, axis=1)
            gate = ha_ref[r0:r0 + A_BLK, HA_Z + hp * LANES:HA_Z + (hp + 1) * LANES].astype(F32)
            col = M_WIDTH + C_WIDTH + hp * LANES
            merged[r0:r0 + A_BLK, col:col + LANES] = (ao * gate).astype(BF16)
    kprev[...] = ha_ref[TB - A_BLK:TB, HA_K:HA_K + KV_WIDTH]
    vprev[...] = ha_ref[TB - A_BLK:TB, HA_V:HA_V + KV_WIDTH]
    k_ref[0] = ha_ref[TB - WINDOW:TB, HA_K:HA_K + KV_WIDTH].astype(F32)
    v_ref[0] = ha_ref[TB - WINDOW:TB, HA_V:HA_V + KV_WIDTH].astype(F32)

    y = _dot(merged[...], wout_ref[...])
    y_ref[...] = _layer_norm(ALPHA * x_ref[...] + y, lng_ref[...], lnb_ref[...])


def _param_spec(width, layer, ngrid):
    if ngrid == 1:
        return lambda rows=1: pl.BlockSpec((None, rows, width), lambda i: (layer, 0, 0))
    return lambda rows=1: pl.BlockSpec((None, rows, width), lambda b, j: (layer, 0, 0))


def _mixer(layer, sinks, pm, pc, pa, g, x2d, wout, prm, batch, seq):
    nb = seq // TB
    row = lambda b, j: (b * nb + j, 0)
    per_b3 = lambda b, j: (b, 0, 0)
    state_shapes = [
        (batch, M_HEADS, M_DK, M_DK), (batch, M_HEADS, M_DK), (batch, 1, LANES),
        (batch, WINDOW, KV_WIDTH), (batch, WINDOW, KV_WIDTH)]
    return pl.pallas_call(
        functools.partial(_mixer_kernel, layer),
        grid=(batch, nb),
        in_specs=[
            pl.BlockSpec(memory_space=pltpu.SMEM),
            pl.BlockSpec((TB, PM_W), row),
            pl.BlockSpec((TB, C_WIDTH), row),
            pl.BlockSpec((TB, HA_W), row),
            pl.BlockSpec((TB, LANES), row),
            pl.BlockSpec((TB, D_MODEL), row),
            _param_spec(D_MODEL, layer, 2)(D_MODEL),
            _param_spec(LANES, layer, 2)(),
            _param_spec(M_WIDTH, layer, 2)(),
            _param_spec(D_MODEL, layer, 2)(),
            _param_spec(D_MODEL, layer, 2)(),
        ],
        out_specs=[
            pl.BlockSpec((TB, D_MODEL), row),
            pl.BlockSpec((1, M_HEADS, M_DK, M_DK), lambda b, j: (b, 0, 0, 0)),
            pl.BlockSpec((1, M_HEADS, M_DK), per_b3),
            pl.BlockSpec((1, 1, LANES), per_b3),
            pl.BlockSpec((1, WINDOW, KV_WIDTH), per_b3),
            pl.BlockSpec((1, WINDOW, KV_WIDTH), per_b3),
        ],
        out_shape=[jax.ShapeDtypeStruct((batch * seq, D_MODEL), F32)]
        + [jax.ShapeDtypeStruct(s, F32) for s in state_shapes],
        scratch_shapes=[
            pltpu.VMEM((A_BLK, KV_WIDTH), BF16),
            pltpu.VMEM((A_BLK, KV_WIDTH), BF16),
            pltpu.VMEM((TB, D_MODEL), BF16),
        ],
        compiler_params=pltpu.CompilerParams(
            dimension_semantics=("arbitrary", "arbitrary"), vmem_limit_bytes=VMEM_LIMIT),
        name="mixer_prompt",
    )(sinks, pm, pc, pa, g, x2d, wout, prm["gbias"], prm["mng"], prm["lng"], prm["lnb"])


def _sample_gates(hs_ref, gbias_ref, m_ref):
    gb = hs_ref[:, MAIN_W:GATE_END] + gbias_ref[...]
    lf = pltpu.roll(_log_sigmoid(gb), LANES - M_HEADS, axis=1)
    m0 = m_ref[...]
    m_new = jnp.maximum(lf + m0, gb)
    wgt = jnp.exp(gb - m_new)
    w0 = jnp.exp(lf + m0 - m_new)
    return m_new, wgt, w0


def _sample_mixer_kernel(hs_ref, c_ref, n_ref, m_ref, conv_ref, ck_ref, cv_ref,
                         gbias_ref, mng_ref, convw_ref, convb_ref, clng_ref, clnb_ref, sink_ref,
                         merged_ref, nout_ref, mout_ref, convout_ref, kout_ref, vout_ref):
    m_new, wgt, w0 = _sample_gates(hs_ref, gbias_ref, m_ref)
    einv = jnp.exp(-m_new)
    mout_ref[...] = m_new
    for h in range(M_HEADS):
        cs = slice(h * M_DK, (h + 1) * M_DK)
        q = hs_ref[:, cs]
        k = hs_ref[:, M_WIDTH + h * M_DK:M_WIDTH + (h + 1) * M_DK] * K_SCALE
        v = hs_ref[:, 2 * M_WIDTH + h * M_DK:2 * M_WIDTH + (h + 1) * M_DK]
        q_t = q.T
        wh = wgt[:, h:h + 1]
        w0h = w0[:, h:h + 1]
        n0 = n_ref[:, h, :]
        s = jnp.sum(q * k, axis=1, keepdims=True) * wh
        qc = jnp.concatenate(
            [jnp.sum(q_t[:, b:b + 1] * c_ref[b, h], axis=0, keepdims=True) for b in range(SB)],
            axis=0)
        num = s * v + w0h * qc
        den = s + w0h * jnp.sum(q * n0, axis=1, keepdims=True)
        hh = num / jnp.maximum(jnp.abs(den), einv[:, h:h + 1])
        nout_ref[:, h, :] = w0h * n0 + wh * k
        mu = jnp.mean(hh, axis=1, keepdims=True)
        hc = hh - mu
        var = jnp.mean(hc * hc, axis=1, keepdims=True)
        hn = hc * lax.rsqrt(var + LN_EPS) * mng_ref[:, cs]
        o = hs_ref[:, 3 * M_WIDTH + h * M_DK:3 * M_WIDTH + (h + 1) * M_DK]
        z = hs_ref[:, 4 * M_WIDTH + h * M_DK:4 * M_WIDTH + (h + 1) * M_DK]
        merged_ref[:, cs] = hn * _sigmoid(o) * (z * _sigmoid(z))

    cu = hs_ref[:, HM_W:HM_W + C_WIDTH]
    cg = hs_ref[:, HM_W + C_WIDTH:HM_W + 2 * C_WIDTH]
    cz = hs_ref[:, HM_W + 2 * C_WIDTH:HM_W + 3 * C_WIDTH]
    a = cu * _sigmoid(cg)
    yc = a * convw_ref[CONV_BUF:CONV_K, :] + convb_ref[...]
    for t in range(CONV_BUF):
        yc = yc + conv_ref[t] * convw_ref[t:t + 1, :]
    for t in range(CONV_BUF - 1):
        convout_ref[t] = conv_ref[t + 1]
    convout_ref[CONV_BUF - 1] = a
    yc = _layer_norm(yc, clng_ref[...], clnb_ref[...])
    merged_ref[:, M_WIDTH:M_WIDTH + C_WIDTH] = (yc * _sigmoid(yc)) * (cz * _sigmoid(cz))

    ab = HM_W + HC_W
    lane = lax.broadcasted_iota(jnp.int32, (1, LANES), 1)
    krow = lax.broadcasted_iota(jnp.int32, (2 * WINDOW, LANES), 0)
    wrow = lax.broadcasted_iota(jnp.int32, (WINDOW, LANES), 0)
    sink_row = sink_ref[...]
    zpad = jnp.zeros((WINDOW - 1, LANES), F32)
    for b in range(SB):
        knew = hs_ref[b:b + 1, ab + HA_K:ab + HA_K + KV_WIDTH]
        vnew = hs_ref[b:b + 1, ab + HA_V:ab + HA_V + KV_WIDTH]
        ck = ck_ref[b]
        cv = cv_ref[b]
        qrows = []
        for hq in range(A_HEADS):
            kvh = hq // GROUP
            t, u = hq // 2, hq % 2
            tile = hs_ref[b:b + 1, ab + t * LANES:ab + (t + 1) * LANES]
            if u != kvh:
                tile = pltpu.roll(tile, HEAD_DIM, axis=1)
            qrows.append(jnp.where((lane // HEAD_DIM) == kvh, tile, 0.0))
        qsel = jnp.concatenate(qrows + [jnp.zeros((LANES - A_HEADS, LANES), F32)], axis=0)
        kext = jnp.concatenate([ck, knew, zpad], axis=0)
        vext = jnp.concatenate([cv, vnew, zpad], axis=0)
        sc = _dot_nt(kext.astype(BF16), qsel.astype(BF16)) * A_SCALE
        sc = jnp.where(krow <= WINDOW, sc, -jnp.inf)
        smax = jnp.maximum(jnp.max(sc, axis=0, keepdims=True), sink_row)
        e = jnp.exp(sc - smax)
        denom = jnp.sum(e, axis=0, keepdims=True) + jnp.exp(sink_row - smax)
        p = e / denom
        ob = _dot_tn(p.astype(BF16), vext.astype(BF16))
        tiles = []
        for t in range(A_HEADS // 2):
            halves = []
            for u in range(2):
                hq = 2 * t + u
                kvh = hq // GROUP
                x = ob[hq:hq + 1, :]
                if u != kvh:
                    x = pltpu.roll(x, HEAD_DIM, axis=1)
                halves.append(x)
            tiles.append(jnp.where(lane < HEAD_DIM, halves[0], halves[1]))
        ao = jnp.concatenate(tiles, axis=1)
        az = hs_ref[b:b + 1, ab + HA_Z:ab + HA_Z + A_WIDTH]
        merged_ref[b:b + 1, M_WIDTH + C_WIDTH:D_MODEL] = ao * (az * _sigmoid(az))
        kout_ref[b] = jnp.where(wrow == WINDOW - 1, knew, pltpu.roll(ck, WINDOW - 1, axis=0))
        vout_ref[b] = jnp.where(wrow == WINDOW - 1, vnew, pltpu.roll(cv, WINDOW - 1, axis=0))


def _sample_mixer(layer, hs, c, n, m, conv, ck, cv, prm):
    nb = hs.shape[0]
    row2 = lambda i: (i, 0)

    def state_specs(lead, pre):
        return [
            pl.BlockSpec(lead + (SB, M_HEADS, M_DK), lambda i: pre + (i, 0, 0)),
            pl.BlockSpec(lead + (SB, LANES), lambda i: pre + (i, 0)),
            pl.BlockSpec(lead + (CONV_BUF, SB, C_WIDTH), lambda i: pre + (0, i, 0)),
            pl.BlockSpec(lead + (SB, WINDOW, KV_WIDTH), lambda i: pre + (i, 0, 0)),
            pl.BlockSpec(lead + (SB, WINDOW, KV_WIDTH), lambda i: pre + (i, 0, 0)),
        ]

    c_spec = pl.BlockSpec((None, SB, M_HEADS, M_DK, M_DK), lambda i: (layer, i, 0, 0, 0))
    state_shapes = [
        (nb, M_HEADS, M_DK), (nb, LANES),
        (CONV_BUF, nb, C_WIDTH), (nb, WINDOW, KV_WIDTH), (nb, WINDOW, KV_WIDTH)]
    return pl.pallas_call(
        _sample_mixer_kernel,
        grid=(nb // SB,),
        in_specs=[pl.BlockSpec((SB, PROJ_W), row2), c_spec] + state_specs((None,), (layer,)) + [
            _param_spec(LANES, layer, 1)(),
            _param_spec(M_WIDTH, layer, 1)(),
            _param_spec(C_WIDTH, layer, 1)(CONV_K + 1),
            _param_spec(C_WIDTH, layer, 1)(),
            _param_spec(C_WIDTH, layer, 1)(),
            _param_spec(C_WIDTH, layer, 1)(),
            _param_spec(LANES, layer, 1)(),
        ],
        out_specs=[pl.BlockSpec((SB, D_MODEL), row2)] + state_specs((), ()),
        out_shape=[jax.ShapeDtypeStruct((nb, D_MODEL), F32)]
        + [jax.ShapeDtypeStruct(s, F32) for s in state_shapes],
        compiler_params=pltpu.CompilerParams(
            dimension_semantics=("arbitrary",), vmem_limit_bytes=VMEM_LIMIT),
        name="mixer_sample",
    )(hs, c, n, m, conv, ck, cv, prm["gbias"], prm["mng"], prm["convw"], prm["convb"],
      prm["clng"], prm["clnb"], prm["sink_row"])


def _update_c_kernel(hs_ref, c_ref, m_ref, gbias_ref, cout_ref):
    _, wgt, w0 = _sample_gates(hs_ref, gbias_ref, m_ref)
    for h in range(M_HEADS):
        k_t = (hs_ref[:, M_WIDTH + h * M_DK:M_WIDTH + (h + 1) * M_DK] * K_SCALE).T
        v = hs_ref[:, 2 * M_WIDTH + h * M_DK:2 * M_WIDTH + (h + 1) * M_DK]
        gv = wgt[:, h:h + 1] * v
        w0h = w0[:, h:h + 1]
        for b in range(SB):
            cout_ref[b, h] = w0h[b:b + 1, :] * c_ref[b, h] + k_t[:, b:b + 1] * gv[b:b + 1, :]


def _update_c(hs_all, c, m, prm):
    depth, nb = hs_all.shape[:2]
    per = lambda l, i: (l, i, 0)
    c_spec = pl.BlockSpec((None, SB, M_HEADS, M_DK, M_DK), lambda l, i: (l, i, 0, 0, 0))
    return pl.pallas_call(
        _update_c_kernel,
        grid=(depth, nb // SB),
        in_specs=[pl.BlockSpec((None, SB, PROJ_W), per), c_spec,
                  pl.BlockSpec((None, SB, LANES), per),
                  pl.BlockSpec((None, 1, LANES), lambda l, i: (l, 0, 0))],
        out_specs=c_spec,
        out_shape=jax.ShapeDtypeStruct(c.shape, F32),
        compiler_params=pltpu.CompilerParams(
            dimension_semantics=("arbitrary", "arbitrary"), vmem_limit_bytes=VMEM_LIMIT),
        name="update_matrix_memory",
    )(hs_all, c, m, prm["gbias"])


def _outproj_sample_kernel(mg_ref, x_ref, wout_ref, lng_ref, lnb_ref, y_ref):
    y = _dot(mg_ref[...].astype(BF16), wout_ref[...])
    y_ref[...] = _layer_norm(ALPHA * x_ref[...] + y, lng_ref[...], lnb_ref[...])


def _outproj_sample(mg, xs, wout, prm, layer):
    nb = xs.shape[0]
    full = lambda shape: pl.BlockSpec(shape, lambda i: (0, 0))
    return pl.pallas_call(
        _outproj_sample_kernel,
        grid=(1,),
        in_specs=[full((nb, D_MODEL)), full((nb, D_MODEL)),
                  _param_spec(D_MODEL, layer, 1)(D_MODEL),
                  _param_spec(D_MODEL, layer, 1)(), _param_spec(D_MODEL, layer, 1)()],
        out_specs=full((nb, D_MODEL)),
        out_shape=jax.ShapeDtypeStruct((nb, D_MODEL), F32),
        compiler_params=pltpu.CompilerParams(
            dimension_semantics=("arbitrary",), vmem_limit_bytes=VMEM_LIMIT),
        name="outproj_sample",
    )(mg, xs, wout, prm["lng"], prm["lnb"])


N_GATES = 2 * M_HEADS
PLAIN_BLOCKS = GATE_OFF // PREP_COLS
GATE_BLOCK = MAIN_W // PREP_COLS


def _regroup_kernel(wt_ref, o_ref):
    j = pl.program_id(1)
    blk = wt_ref[0]
    row = lax.broadcasted_iota(jnp.int32, blk.shape, 0)
    blk = jnp.where(row < jnp.where(j == GATE_BLOCK, N_GATES, PREP_COLS), blk, 0.0)
    o_ref[...] = blk.astype(BF16)


def _regroup_w_in(w_in_t):
    depth, _, d = w_in_t.shape

    def src_col(l, j):
        per_blk = PREP_COLS // SUBLANES
        start = jnp.where(j < PLAIN_BLOCKS, j * per_blk,
                          jnp.where(j < GATE_BLOCK, j * per_blk + N_GATES // SUBLANES,
                                    GATE_OFF // SUBLANES))
        return (l, start * SUBLANES, 0)

    return pl.pallas_call(
        _regroup_kernel,
        grid=(depth, PROJ_W // PREP_COLS),
        in_specs=[pl.BlockSpec((pl.Element(1), pl.Element(PREP_COLS), pl.Element(d)), src_col)],
        out_specs=pl.BlockSpec((None, PREP_COLS, d), lambda l, j: (l, j, 0)),
        out_shape=jax.ShapeDtypeStruct((depth, PROJ_W, d), BF16),
        compiler_params=pltpu.CompilerParams(dimension_semantics=("arbitrary", "arbitrary")),
        name="regroup_w_in",
    )(w_in_t)


def _cast_kernel(w_ref, o_ref):
    o_ref[...] = w_ref[...].astype(BF16)


def _cast_w_out(w_out):
    depth, d, n = w_out.shape
    rows = 512
    return pl.pallas_call(
        _cast_kernel,
        grid=(depth, d // rows),
        in_specs=[pl.BlockSpec((None, rows, n), lambda l, i: (l, i, 0))],
        out_specs=pl.BlockSpec((None, rows, n), lambda l, i: (l, i, 0)),
        out_shape=jax.ShapeDtypeStruct((depth, d, n), BF16),
        compiler_params=pltpu.CompilerParams(dimension_semantics=("arbitrary", "arbitrary")),
        name="cast_w_out",
    )(w_out)


def _lane_rows(*mats):
    v = jnp.concatenate([a.astype(F32) for a in mats], axis=1)
    return jnp.pad(v, ((0, 0), (0, LANES - v.shape[1])))[:, None, :]


def kernel(x_prompt, x_sample, state_C, state_n, state_m, state_conv, cache_k, cache_v, w_in, w_out,
           b_igate, b_fgate, m_norm_g, conv_w, conv_b, conv_ln_g, conv_ln_b, sinks, ln_g, ln_b):
    batch, seq, _ = x_prompt.shape
    nb = x_sample.shape[0]
    xp = x_prompt.reshape(batch * seq, D_MODEL)
    xs = x_sample.reshape(nb, D_MODEL)
    wt = _regroup_w_in(jnp.swapaxes(w_in, 1, 2))
    wout = _cast_w_out(w_out)
    prm = {
        "gbias": _lane_rows(b_igate, b_fgate),
        "sink_row": _lane_rows(sinks),
        "mng": m_norm_g[:, None, :],
        "convw": jnp.pad(conv_w, ((0, 0), (0, 1), (0, 0))),
        "convb": conv_b[:, None, :],
        "clng": conv_ln_g[:, None, :],
        "clnb": conv_ln_b[:, None, :],
        "lng": ln_g[:, None, :],
        "lnb": ln_b[:, None, :],
    }
    m_in = jnp.pad(state_m, ((0, 0), (0, 0), (0, LANES - M_HEADS)))
    sconv = jnp.swapaxes(state_conv, 1, 2)
    ck = cache_k.reshape(DEPTH, nb, WINDOW, KV_WIDTH)
    cv = cache_v.reshape(DEPTH, nb, WINDOW, KV_WIDTH)
    st_p = []
    st_s = []
    hs_all = []
    for l in range(DEPTH):
        pm, pc, pa, g, conv_p, hs = _inproj(xp, xs, wt, prm, l, batch)
        xp, c_p, n_p, m_p, k_p, v_p = _mixer(l, sinks, pm, pc, pa, g, xp, wout, prm, batch, seq)
        st_p.append((c_p, n_p, m_p, conv_p, k_p, v_p))
        mg, *states = _sample_mixer(l, hs, state_C, state_n, m_in, sconv, ck, cv, prm)
        st_s.append(states)
        hs_all.append(hs)
        xs = _outproj_sample(mg, xs, wout, prm, l)
    c_s = _update_c(jnp.stack(hs_all, axis=0), state_C, m_in, prm)

    def stacked(per_layer):
        return [jnp.stack(a, axis=0) for a in zip(*per_layer)]

    def finish(c, nn, m, conv, k, v, n):
        return (c, nn, m.reshape(DEPTH, n, LANES)[:, :, :M_HEADS], conv,
                k.reshape(DEPTH, n, WINDOW, KV_HEADS, HEAD_DIM),
                v.reshape(DEPTH, n, WINDOW, KV_HEADS, HEAD_DIM))

    n_s, m_s, conv_s, k_s, v_s = stacked(st_s)
    return (xp.reshape(batch, seq, D_MODEL), xs.reshape(nb, 1, D_MODEL),
            *finish(*stacked(st_p), batch),
            *finish(c_s, n_s, m_s, jnp.swapaxes(conv_s, 1, 2), k_s, v_s, nb))
```

```python
import functools

import jax
import jax.numpy as jnp
from jax import lax
from jax.experimental import pallas as pl
from jax.experimental.pallas import tpu as pltpu

F32 = jnp.float32
BF16 = jnp.bfloat16

D_MODEL = 2048
DEPTH = 2
M_WIDTH = 1024
M_HEADS = 4
M_DK = 256
C_WIDTH = 512
CONV_K = 31
CONV_BUF = CONV_K - 1
A_WIDTH = 512
HEAD_DIM = 64
A_HEADS = 8
KV_HEADS = 2
GROUP = A_HEADS // KV_HEADS
KV_WIDTH = KV_HEADS * HEAD_DIM
WINDOW = 128
ALPHA = (2 * DEPTH) ** 0.25
LN_EPS = 1e-5
K_SCALE = M_DK ** -0.5
A_SCALE = HEAD_DIM ** -0.5

LANES = 128
SUBLANES = 8

HM_W = 5 * M_WIDTH
HC_W = 3 * C_WIDTH
HA_W = 2 * A_WIDTH + 2 * KV_WIDTH
MAIN_W = HM_W + HC_W + HA_W
GATE_END = MAIN_W + LANES
PROJ_W = GATE_END
GATE_OFF = 4 * M_WIDTH
N_GATES = 2 * M_HEADS
W_CHUNK = 256
PLAIN_CHUNKS = GATE_OFF // W_CHUNK
MAIN_CHUNKS = MAIN_W // W_CHUNK
HA_K = A_WIDTH
HA_V = A_WIDTH + KV_WIDTH
HA_Z = A_WIDTH + 2 * KV_WIDTH

TM_PROJ = 256
TB = 256
A_BLK = 128
CONV_HALO = 32
SB = 8
VMEM_LIMIT = 56 * 1024 * 1024
INPROJ_VMEM_LIMIT = 60 * 1024 * 1024


def _sigmoid(x):
    return 1.0 / (1.0 + jnp.exp(-x))


def _log_sigmoid(x):
    return jnp.minimum(x, 0.0) - jnp.log1p(jnp.exp(-jnp.abs(x)))


def _layer_norm(x, g, b):
    mu = jnp.mean(x, axis=-1, keepdims=True)
    xc = x - mu
    var = jnp.mean(xc * xc, axis=-1, keepdims=True)
    return xc * lax.rsqrt(var + LN_EPS) * g + b


def _dot(a, b):
    return jnp.dot(a, b, preferred_element_type=F32)


def _dot_nt(a, b):
    return lax.dot_general(a, b, (((1,), (1,)), ((), ())), preferred_element_type=F32)


def _dot_tn(a, b):
    return lax.dot_general(a, b, (((0,), (0,)), ((), ())), preferred_element_type=F32)


PM_W = 4 * M_WIDTH
PA_QKV = A_WIDTH + 2 * KV_WIDTH
QKV_CHUNK = 512
SHIFTED_ROWS = CONV_HALO + TM_PROJ - SUBLANES


def _silu(x):
    return x * _sigmoid(x)


def _stage_weights(layer, w_hbm, wt_ref, stage, sem):
    def chunk_copy(c, slot):
        src = jnp.where(c < PLAIN_CHUNKS, c * W_CHUNK, c * W_CHUNK + N_GATES)
        return pltpu.make_async_copy(
            w_hbm.at[layer, pl.ds(pl.multiple_of(src, SUBLANES), W_CHUNK), :],
            stage.at[slot], sem.at[slot])

    chunk_copy(0, 0).start()

    def body(c, carry):
        slot = c % 2
        chunk_copy(c, slot).wait()

        @pl.when(c + 1 < MAIN_CHUNKS)
        def _():
            chunk_copy(c + 1, 1 - slot).start()

        wt_ref[pl.ds(pl.multiple_of(c * W_CHUNK, W_CHUNK), W_CHUNK), :] = stage[slot].astype(BF16)
        return carry

    lax.fori_loop(0, MAIN_CHUNKS, body, 0)
    stage[0, 0:LANES, :] = jnp.zeros((LANES, D_MODEL), F32)
    gate_copy = pltpu.make_async_copy(
        w_hbm.at[layer, GATE_OFF:GATE_OFF + N_GATES, :], stage.at[0, 0:N_GATES, :], sem.at[0])
    gate_copy.start()
    gate_copy.wait()
    wt_ref[MAIN_W:GATE_END, :] = stage[0, 0:LANES, :].astype(BF16)


def _inproj_kernel(layer, blocks_per_seq, x_ref, xs_ref, w_hbm, convw_ref, convb_ref, clng_ref,
                   clnb_ref, pm_ref, pc_ref, pa_ref, g_ref, conv_ref, hs_ref,
                   wt_ref, stage, sem, xbuf, abuf, sbuf, ybuf, zbuf):
    i = pl.program_id(0)

    @pl.when(i == 0)
    def _():
        _stage_weights(layer, w_hbm, wt_ref, stage, sem)
        hs_ref[...] = _dot_nt(xs_ref[...].astype(BF16), wt_ref[...])

    @pl.when(i % blocks_per_seq == 0)
    def _():
        abuf[0:CONV_HALO, :] = jnp.zeros((CONV_HALO, C_WIDTH), F32)

    xbuf[...] = x_ref[...].astype(BF16)

    def proj(c0, c1):
        return _dot_nt(xbuf[...], wt_ref[c0:c1, :])

    for c in range(0, 3 * M_WIDTH, QKV_CHUNK):
        pm_ref[:, c:c + QKV_CHUNK] = proj(c, c + QKV_CHUNK).astype(BF16)
    for h in range(M_HEADS):
        o = proj(3 * M_WIDTH + h * M_DK, 3 * M_WIDTH + (h + 1) * M_DK)
        z = proj(4 * M_WIDTH + h * M_DK, 4 * M_WIDTH + (h + 1) * M_DK)
        pm_ref[:, 3 * M_WIDTH + h * M_DK:3 * M_WIDTH + (h + 1) * M_DK] = (
            _sigmoid(o) * _silu(z)).astype(BF16)

    cu = proj(HM_W, HM_W + C_WIDTH)
    cg = proj(HM_W + C_WIDTH, HM_W + 2 * C_WIDTH)
    abuf[CONV_HALO:CONV_HALO + TM_PROJ, :] = cu * _sigmoid(cg)
    zbuf[...] = _silu(proj(HM_W + 2 * C_WIDTH, HM_W + 3 * C_WIDTH))
    for p in range(1, SUBLANES):
        sbuf[p - 1] = abuf[p:p + SHIFTED_ROWS, :]
    tap0 = CONV_HALO - CONV_BUF
    for r in range(0, TM_PROJ, 128):
        for c in range(0, C_WIDTH, LANES):
            acc = jnp.zeros((128, LANES), F32)
            for t in range(CONV_K):
                p = (tap0 + t) % SUBLANES
                base = r + tap0 + t - p
                src = abuf if p == 0 else sbuf.at[p - 1]
                acc = acc + src[base:base + 128, c:c + LANES] * convw_ref[t:t + 1, c:c + LANES]
            ybuf[r:r + 128, c:c + LANES] = acc + convb_ref[:, c:c + LANES]
    conv_ref[0] = abuf[CONV_HALO + TM_PROJ - CONV_BUF:CONV_HALO + TM_PROJ, :]
    abuf[0:CONV_HALO, :] = abuf[TM_PROJ:TM_PROJ + CONV_HALO, :]
    yc = _layer_norm(ybuf[...], clng_ref[...], clnb_ref[...])
    pc_ref[...] = (_silu(yc) * zbuf[...]).astype(BF16)

    base = HM_W + HC_W
    pa_ref[:, 0:PA_QKV] = proj(base, base + PA_QKV).astype(BF16)
    pa_ref[:, PA_QKV:HA_W] = _silu(proj(base + PA_QKV, base + HA_W)).astype(BF16)
    g_ref[...] = proj(MAIN_W, GATE_END)


def _inproj(x2d, xs, w_in_t, prm, layer, batch):
    m = x2d.shape[0]
    ns = xs.shape[0]
    nblk = m // TM_PROJ
    row = lambda i: (i, 0)
    once = lambda i: (0, 0)
    return pl.pallas_call(
        functools.partial(_inproj_kernel, layer, nblk // batch),
        grid=(nblk,),
        in_specs=[
            pl.BlockSpec((TM_PROJ, D_MODEL), row),
            pl.BlockSpec((ns, D_MODEL), once),
            pl.BlockSpec(memory_space=pl.ANY),
            _param_spec(C_WIDTH, layer, 1)(CONV_K + 1),
            _param_spec(C_WIDTH, layer, 1)(),
            _param_spec(C_WIDTH, layer, 1)(),
            _param_spec(C_WIDTH, layer, 1)(),
        ],
        out_specs=[
            pl.BlockSpec((TM_PROJ, PM_W), row),
            pl.BlockSpec((TM_PROJ, C_WIDTH), row),
            pl.BlockSpec((TM_PROJ, HA_W), row),
            pl.BlockSpec((TM_PROJ, LANES), row),
            pl.BlockSpec((1, CONV_BUF, C_WIDTH), lambda i: (i // (nblk // batch), 0, 0)),
            pl.BlockSpec((ns, PROJ_W), once),
        ],
        out_shape=[
            jax.ShapeDtypeStruct((m, PM_W), BF16),
            jax.ShapeDtypeStruct((m, C_WIDTH), BF16),
            jax.ShapeDtypeStruct((m, HA_W), BF16),
            jax.ShapeDtypeStruct((m, LANES), F32),
            jax.ShapeDtypeStruct((batch, CONV_BUF, C_WIDTH), F32),
            jax.ShapeDtypeStruct((ns, PROJ_W), F32),
        ],
        scratch_shapes=[
            pltpu.VMEM((PROJ_W, D_MODEL), BF16),
            pltpu.VMEM((2, W_CHUNK, D_MODEL), F32),
            pltpu.SemaphoreType.DMA((2,)),
            pltpu.VMEM((TM_PROJ, D_MODEL), BF16),
            pltpu.VMEM((CONV_HALO + TM_PROJ, C_WIDTH), F32),
            pltpu.VMEM((SUBLANES - 1, SHIFTED_ROWS, C_WIDTH), F32),
            pltpu.VMEM((TM_PROJ, C_WIDTH), F32),
            pltpu.VMEM((TM_PROJ, C_WIDTH), F32),
        ],
        compiler_params=pltpu.CompilerParams(
            dimension_semantics=("arbitrary",), vmem_limit_bytes=INPROJ_VMEM_LIMIT),
        name="inproj_prompt",
    )(x2d, xs, w_in_t, prm["convw"], prm["convb"], prm["clng"], prm["clnb"])


def _cumsum_rows(x):
    n = x.shape[0]
    row = lax.broadcasted_iota(jnp.int32, x.shape, 0)
    s = 1
    while s < n:
        x = x + jnp.where(row >= s, pltpu.roll(x, s, axis=0), 0.0)
        s *= 2
    return x


def _mixer_kernel(layer, sinks_ref, hm_ref, pc_ref, ha_ref, g_ref, x_ref, wout_ref,
                  gbias_ref, mng_ref, lng_ref, lnb_ref,
                  y_ref, c_ref, n_ref, m_ref, k_ref, v_ref,
                  kprev, vprev, merged):
    j = pl.program_id(1)

    @pl.when(j == 0)
    def _():
        c_ref[...] = jnp.zeros_like(c_ref)
        n_ref[...] = jnp.zeros_like(n_ref)
        m_ref[...] = jnp.zeros_like(m_ref)
        kprev[...] = jnp.zeros_like(kprev)
        vprev[...] = jnp.zeros_like(vprev)

    gb = g_ref[...] + gbias_ref[...]
    bsum = _cumsum_rows(_log_sigmoid(gb))
    bsum = pltpu.roll(bsum, LANES - M_HEADS, axis=1)
    a_all = gb - bsum
    a_t = a_all.T
    m_prev = m_ref[0]
    row_i = lax.broadcasted_iota(jnp.int32, (TB, TB), 0)
    col_i = lax.broadcasted_iota(jnp.int32, (TB, TB), 1)
    causal = col_i <= row_i
    lane_i = lax.broadcasted_iota(jnp.int32, (1, LANES), 1)
    m_new = m_prev

    for h in range(M_HEADS):
        cs = slice(h * M_DK, (h + 1) * M_DK)
        q = hm_ref[:, cs]
        k = hm_ref[:, M_WIDTH + h * M_DK:M_WIDTH + (h + 1) * M_DK]
        v = hm_ref[:, 2 * M_WIDTH + h * M_DK:2 * M_WIDTH + (h + 1) * M_DK]
        logit = jnp.where(causal, a_t[h:h + 1, :], -jnp.inf)
        m0 = m_prev[:, h:h + 1]
        mx = jnp.maximum(jnp.max(logit, axis=1, keepdims=True), m0)
        w = jnp.exp(logit - mx)
        w0 = jnp.exp(m0 - mx)
        s = _dot_nt(q, k) * (w * K_SCALE)
        c0 = c_ref[0, h]
        n0 = n_ref[0, h:h + 1, :]
        num = _dot(s.astype(BF16), v) + w0 * _dot(q, c0.astype(BF16))
        den = (jnp.sum(s, axis=1, keepdims=True)
               + w0 * jnp.sum(q.astype(F32) * n0, axis=1, keepdims=True))
        m_t = bsum[:, h:h + 1] + mx
        hh = num / jnp.maximum(jnp.abs(den), jnp.exp(-m_t))
        mu = jnp.mean(hh, axis=1, keepdims=True)
        hc = hh - mu
        var = jnp.mean(hc * hc, axis=1, keepdims=True)
        hn = hc * lax.rsqrt(var + LN_EPS) * mng_ref[:, cs]
        gate = hm_ref[:, 3 * M_WIDTH + h * M_DK:3 * M_WIDTH + (h + 1) * M_DK].astype(F32)
        merged[:, cs] = (hn * gate).astype(BF16)
        mx_end = mx[TB - 1:TB, :]
        g_col = jnp.exp(a_all[:, h:h + 1] - mx_end) * K_SCALE
        g0 = jnp.exp(m0 - mx_end)
        kg = k.astype(F32) * g_col
        c_ref[0, h] = g0 * c0 + _dot_tn(kg.astype(BF16), v)
        n_ref[0, h:h + 1, :] = g0 * n0 + jnp.sum(kg, axis=0, keepdims=True)
        m_end = bsum[TB - 1:TB, h:h + 1] + mx_end
        m_new = jnp.where(lane_i == h, m_end, m_new)
    m_ref[0] = m_new

    merged[:, M_WIDTH:M_WIDTH + C_WIDTH] = pc_ref[...]

    arow = lax.broadcasted_iota(jnp.int32, (A_BLK, 2 * A_BLK), 0)
    acol = lax.broadcasted_iota(jnp.int32, (A_BLK, 2 * A_BLK), 1)
    rel = A_BLK + arow - acol
    band = (rel >= 0) & (rel <= WINDOW)
    for sub in range(TB // A_BLK):
        r0 = sub * A_BLK
        if sub == 0:
            kcat = jnp.concatenate([kprev[...], ha_ref[0:A_BLK, HA_K:HA_K + KV_WIDTH]], axis=0)
            vcat = jnp.concatenate([vprev[...], ha_ref[0:A_BLK, HA_V:HA_V + KV_WIDTH]], axis=0)
            mask = band & (acol >= jnp.where(j > 0, 0, A_BLK))
        else:
            kcat = ha_ref[r0 - A_BLK:r0 + A_BLK, HA_K:HA_K + KV_WIDTH]
            vcat = ha_ref[r0 - A_BLK:r0 + A_BLK, HA_V:HA_V + KV_WIDTH]
            mask = band
        for hp in range(A_HEADS // 2):
            outs = []
            for hq in (2 * hp, 2 * hp + 1):
                kvh = hq // GROUP
                qh = ha_ref[r0:r0 + A_BLK, hq * HEAD_DIM:(hq + 1) * HEAD_DIM]
                kh = kcat[:, kvh * HEAD_DIM:(kvh + 1) * HEAD_DIM]
                vh = vcat[:, kvh * HEAD_DIM:(kvh + 1) * HEAD_DIM]
                sc = jnp.where(mask, _dot_nt(qh, kh) * A_SCALE, -jnp.inf)
                sink = sinks_ref[layer, hq]
                smax = jnp.maximum(jnp.max(sc, axis=1, keepdims=True), sink)
                e = jnp.exp(sc - smax)
                denom = jnp.sum(e, axis=1, keepdims=True) + jnp.exp(sink - smax)
                outs.append(_dot((e / denom).astype(BF16), vh))
            ao = jnp.concatenate(outs, axis=1)
            gate = ha_ref[r0:r0 + A_BLK, HA_Z + hp * LANES:HA_Z + (hp + 1) * LANES].astype(F32)
            col = M_WIDTH + C_WIDTH + hp * LANES
            merged[r0:r0 + A_BLK, col:col + LANES] = (ao * gate).astype(BF16)
    kprev[...] = ha_ref[TB - A_BLK:TB, HA_K:HA_K + KV_WIDTH]
    vprev[...] = ha_ref[TB - A_BLK:TB, HA_V:HA_V + KV_WIDTH]
    k_ref[0] = ha_ref[TB - WINDOW:TB, HA_K:HA_K + KV_WIDTH].astype(F32)
    v_ref[0] = ha_ref[TB - WINDOW:TB, HA_V:HA_V + KV_WIDTH].astype(F32)

    y = _dot(merged[...], wout_ref[...])
    y_ref[...] = _layer_norm(ALPHA * x_ref[...] + y, lng_ref[...], lnb_ref[...])


def _param_spec(width, layer, ngrid):
    if ngrid == 1:
        return lambda rows=1: pl.BlockSpec((None, rows, width), lambda i: (layer, 0, 0))
    return lambda rows=1: pl.BlockSpec((None, rows, width), lambda b, j: (layer, 0, 0))


def _mixer(layer, sinks, pm, pc, pa, g, x2d, wout, prm, batch, seq):
    nb = seq // TB
    row = lambda b, j: (b * nb + j, 0)
    per_b3 = lambda b, j: (b, 0, 0)
    state_shapes = [
        (batch, M_HEADS, M_DK, M_DK), (batch, M_HEADS, M_DK), (batch, 1, LANES),
        (batch, WINDOW, KV_WIDTH), (batch, WINDOW, KV_WIDTH)]
    return pl.pallas_call(
        functools.partial(_mixer_kernel, layer),
        grid=(batch, nb),
        in_specs=[
            pl.BlockSpec(memory_space=pltpu.SMEM),
            pl.BlockSpec((TB, PM_W), row),
            pl.BlockSpec((TB, C_WIDTH), row),
            pl.BlockSpec((TB, HA_W), row),
            pl.BlockSpec((TB, LANES), row),
            pl.BlockSpec((TB, D_MODEL), row),
            _param_spec(D_MODEL, layer, 2)(D_MODEL),
            _param_spec(LANES, layer, 2)(),
            _param_spec(M_WIDTH, layer, 2)(),
            _param_spec(D_MODEL, layer, 2)(),
            _param_spec(D_MODEL, layer, 2)(),
        ],
        out_specs=[
            pl.BlockSpec((TB, D_MODEL), row),
            pl.BlockSpec((1, M_HEADS, M_DK, M_DK), lambda b, j: (b, 0, 0, 0)),
            pl.BlockSpec((1, M_HEADS, M_DK), per_b3),
            pl.BlockSpec((1, 1, LANES), per_b3),
            pl.BlockSpec((1, WINDOW, KV_WIDTH), per_b3),
            pl.BlockSpec((1, WINDOW, KV_WIDTH), per_b3),
        ],
        out_shape=[jax.ShapeDtypeStruct((batch * seq, D_MODEL), F32)]
        + [jax.ShapeDtypeStruct(s, F32) for s in state_shapes],
        scratch_shapes=[
            pltpu.VMEM((A_BLK, KV_WIDTH), BF16),
            pltpu.VMEM((A_BLK, KV_WIDTH), BF16),
            pltpu.VMEM((TB, D_MODEL), BF16),
        ],
        compiler_params=pltpu.CompilerParams(
            dimension_semantics=("arbitrary", "arbitrary"), vmem_limit_bytes=VMEM_LIMIT),
        name="mixer_prompt",
    )(sinks, pm, pc, pa, g, x2d, wout, prm["gbias"], prm["mng"], prm["lng"], prm["lnb"])


def _sample_gates(hs_ref, gbias_ref, m_ref):
    gb = hs_ref[:, MAIN_W:GATE_END] + gbias_ref[...]
    lf = pltpu.roll(_log_sigmoid(gb), LANES - M_HEADS, axis=1)
    m0 = m_ref[...]
    m_new = jnp.maximum(lf + m0, gb)
    wgt = jnp.exp(gb - m_new)
    w0 = jnp.exp(lf + m0 - m_new)
    return m_new, wgt, w0


def _sample_mixer_kernel(hs_ref, c_ref, n_ref, m_ref, conv_ref, ck_ref, cv_ref,
                         gbias_ref, mng_ref, convw_ref, convb_ref, clng_ref, clnb_ref, sink_ref,
                         merged_ref, nout_ref, mout_ref, convout_ref, kout_ref, vout_ref):
    m_new, wgt, w0 = _sample_gates(hs_ref, gbias_ref, m_ref)
    einv = jnp.exp(-m_new)
    mout_ref[...] = m_new
    for h in range(M_HEADS):
        cs = slice(h * M_DK, (h + 1) * M_DK)
        q = hs_ref[:, cs]
        k = hs_ref[:, M_WIDTH + h * M_DK:M_WIDTH + (h + 1) * M_DK] * K_SCALE
        v = hs_ref[:, 2 * M_WIDTH + h * M_DK:2 * M_WIDTH + (h + 1) * M_DK]
        q_t = q.T
        wh = wgt[:, h:h + 1]
        w0h = w0[:, h:h + 1]
        n0 = n_ref[:, h, :]
        s = jnp.sum(q * k, axis=1, keepdims=True) * wh
        qc = jnp.concatenate(
            [jnp.sum(q_t[:, b:b + 1] * c_ref[b, h], axis=0, keepdims=True) for b in range(SB)],
            axis=0)
        num = s * v + w0h * qc
        den = s + w0h * jnp.sum(q * n0, axis=1, keepdims=True)
        hh = num / jnp.maximum(jnp.abs(den), einv[:, h:h + 1])
        nout_ref[:, h, :] = w0h * n0 + wh * k
        mu = jnp.mean(hh, axis=1, keepdims=True)
        hc = hh - mu
        var = jnp.mean(hc * hc, axis=1, keepdims=True)
        hn = hc * lax.rsqrt(var + LN_EPS) * mng_ref[:, cs]
        o = hs_ref[:, 3 * M_WIDTH + h * M_DK:3 * M_WIDTH + (h + 1) * M_DK]
        z = hs_ref[:, 4 * M_WIDTH + h * M_DK:4 * M_WIDTH + (h + 1) * M_DK]
        merged_ref[:, cs] = hn * _sigmoid(o) * (z * _sigmoid(z))

    cu = hs_ref[:, HM_W:HM_W + C_WIDTH]
    cg = hs_ref[:, HM_W + C_WIDTH:HM_W + 2 * C_WIDTH]
    cz = hs_ref[:, HM_W + 2 * C_WIDTH:HM_W + 3 * C_WIDTH]
    a = cu * _sigmoid(cg)
    yc = a * convw_ref[CONV_BUF:CONV_K, :] + convb_ref[...]
    for t in range(CONV_BUF):
        yc = yc + conv_ref[t] * convw_ref[t:t + 1, :]
    for t in range(CONV_BUF - 1):
        convout_ref[t] = conv_ref[t + 1]
    convout_ref[CONV_BUF - 1] = a
    yc = _layer_norm(yc, clng_ref[...], clnb_ref[...])
    merged_ref[:, M_WIDTH:M_WIDTH + C_WIDTH] = (yc * _sigmoid(yc)) * (cz * _sigmoid(cz))

    ab = HM_W + HC_W
    lane = lax.broadcasted_iota(jnp.int32, (1, LANES), 1)
    krow = lax.broadcasted_iota(jnp.int32, (2 * WINDOW, LANES), 0)
    wrow = lax.broadcasted_iota(jnp.int32, (WINDOW, LANES), 0)
    sink_row = sink_ref[...]
    zpad = jnp.zeros((WINDOW - 1, LANES), F32)
    for b in range(SB):
        knew = hs_ref[b:b + 1, ab + HA_K:ab + HA_K + KV_WIDTH]
        vnew = hs_ref[b:b + 1, ab + HA_V:ab + HA_V + KV_WIDTH]
        ck = ck_ref[b]
        cv = cv_ref[b]
        qrows = []
        for hq in range(A_HEADS):
            kvh = hq // GROUP
            t, u = hq // 2, hq % 2
            tile = hs_ref[b:b + 1, ab + t * LANES:ab + (t + 1) * LANES]
            if u != kvh:
                tile = pltpu.roll(tile, HEAD_DIM, axis=1)
            qrows.append(jnp.where((lane // HEAD_DIM) == kvh, tile, 0.0))
        qsel = jnp.concatenate(qrows + [jnp.zeros((LANES - A_HEADS, LANES), F32)], axis=0)
        kext = jnp.concatenate([ck, knew, zpad], axis=0)
        vext = jnp.concatenate([cv, vnew, zpad], axis=0)
        sc = _dot_nt(kext.astype(BF16), qsel.astype(BF16)) * A_SCALE
        sc = jnp.where(krow <= WINDOW, sc, -jnp.inf)
        smax = jnp.maximum(jnp.max(sc, axis=0, keepdims=True), sink_row)
        e = jnp.exp(sc - smax)
        denom = jnp.sum(e, axis=0, keepdims=True) + jnp.exp(sink_row - smax)
        p = e / denom
        ob = _dot_tn(p.astype(BF16), vext.astype(BF16))
        tiles = []
        for t in range(A_HEADS // 2):
            halves = []
            for u in range(2):
                hq = 2 * t + u
                kvh = hq // GROUP
                x = ob[hq:hq + 1, :]
                if u != kvh:
                    x = pltpu.roll(x, HEAD_DIM, axis=1)
                halves.append(x)
            tiles.append(jnp.where(lane < HEAD_DIM, halves[0], halves[1]))
        ao = jnp.concatenate(tiles, axis=1)
        az = hs_ref[b:b + 1, ab + HA_Z:ab + HA_Z + A_WIDTH]
        merged_ref[b:b + 1, M_WIDTH + C_WIDTH:D_MODEL] = ao * (az * _sigmoid(az))
        kout_ref[b] = jnp.where(wrow == WINDOW - 1, knew, pltpu.roll(ck, WINDOW - 1, axis=0))
        vout_ref[b] = jnp.where(wrow == WINDOW - 1, vnew, pltpu.roll(cv, WINDOW - 1, axis=0))


def _sample_mixer(layer, hs, c, n, m, conv, ck, cv, prm):
    nb = hs.shape[0]
    row2 = lambda i: (i, 0)

    def state_specs(lead, pre):
        return [
            pl.BlockSpec(lead + (SB, M_HEADS, M_DK), lambda i: pre + (i, 0, 0)),
            pl.BlockSpec(lead + (SB, LANES), lambda i: pre + (i, 0)),
            pl.BlockSpec(lead + (CONV_BUF, SB, C_WIDTH), lambda i: pre + (0, i, 0)),
            pl.BlockSpec(lead + (SB, WINDOW, KV_WIDTH), lambda i: pre + (i, 0, 0)),
            pl.BlockSpec(lead + (SB, WINDOW, KV_WIDTH), lambda i: pre + (i, 0, 0)),
        ]

    c_spec = pl.BlockSpec((None, SB, M_HEADS, M_DK, M_DK), lambda i: (layer, i, 0, 0, 0))
    state_shapes = [
        (nb, M_HEADS, M_DK), (nb, LANES),
        (CONV_BUF, nb, C_WIDTH), (nb, WINDOW, KV_WIDTH), (nb, WINDOW, KV_WIDTH)]
    return pl.pallas_call(
        _sample_mixer_kernel,
        grid=(nb // SB,),
        in_specs=[pl.BlockSpec((SB, PROJ_W), row2), c_spec] + state_specs((None,), (layer,)) + [
            _param_spec(LANES, layer, 1)(),
            _param_spec(M_WIDTH, layer, 1)(),
            _param_spec(C_WIDTH, layer, 1)(CONV_K + 1),
            _param_spec(C_WIDTH, layer, 1)(),
            _param_spec(C_WIDTH, layer, 1)(),
            _param_spec(C_WIDTH, layer, 1)(),
            _param_spec(LANES, layer, 1)(),
        ],
        out_specs=[pl.BlockSpec((SB, D_MODEL), row2)] + state_specs((), ()),
        out_shape=[jax.ShapeDtypeStruct((nb, D_MODEL), F32)]
        + [jax.ShapeDtypeStruct(s, F32) for s in state_shapes],
        compiler_params=pltpu.CompilerParams(
            dimension_semantics=("arbitrary",), vmem_limit_bytes=VMEM_LIMIT),
        name="mixer_sample",
    )(hs, c, n, m, conv, ck, cv, prm["gbias"], prm["mng"], prm["convw"], prm["convb"],
      prm["clng"], prm["clnb"], prm["sink_row"])


def _update_c_kernel(hs_ref, c_ref, m_ref, gbias_ref, cout_ref):
    _, wgt, w0 = _sample_gates(hs_ref, gbias_ref, m_ref)
    for h in range(M_HEADS):
        k_t = (hs_ref[:, M_WIDTH + h * M_DK:M_WIDTH + (h + 1) * M_DK] * K_SCALE).T
        v = hs_ref[:, 2 * M_WIDTH + h * M_DK:2 * M_WIDTH + (h + 1) * M_DK]
        gv = wgt[:, h:h + 1] * v
        w0h = w0[:, h:h + 1]
        for b in range(SB):
            cout_ref[b, h] = w0h[b:b + 1, :] * c_ref[b, h] + k_t[:, b:b + 1] * gv[b:b + 1, :]


def _update_c(hs_all, c, m, prm):
    depth, nb = hs_all.shape[:2]
    per = lambda l, i: (l, i, 0)
    c_spec = pl.BlockSpec((None, SB, M_HEADS, M_DK, M_DK), lambda l, i: (l, i, 0, 0, 0))
    return pl.pallas_call(
        _update_c_kernel,
        grid=(depth, nb // SB),
        in_specs=[pl.BlockSpec((None, SB, PROJ_W), per), c_spec,
                  pl.BlockSpec((None, SB, LANES), per),
                  pl.BlockSpec((None, 1, LANES), lambda l, i: (l, 0, 0))],
        out_specs=c_spec,
        out_shape=jax.ShapeDtypeStruct(c.shape, F32),
        compiler_params=pltpu.CompilerParams(
            dimension_semantics=("arbitrary", "arbitrary"), vmem_limit_bytes=VMEM_LIMIT),
        name="update_matrix_memory",
    )(hs_all, c, m, prm["gbias"])


def _outproj_sample_kernel(mg_ref, x_ref, wout_ref, lng_ref, lnb_ref, y_ref):
    y = _dot(mg_ref[...].astype(BF16), wout_ref[...])
    y_ref[...] = _layer_norm(ALPHA * x_ref[...] + y, lng_ref[...], lnb_ref[...])


def _outproj_sample(mg, xs, wout, prm, layer):
    nb = xs.shape[0]
    full = lambda shape: pl.BlockSpec(shape, lambda i: (0, 0))
    return pl.pallas_call(
        _outproj_sample_kernel,
        grid=(1,),
        in_specs=[full((nb, D_MODEL)), full((nb, D_MODEL)),
                  _param_spec(D_MODEL, layer, 1)(D_MODEL),
                  _param_spec(D_MODEL, layer, 1)(), _param_spec(D_MODEL, layer, 1)()],
        out_specs=full((nb, D_MODEL)),
        out_shape=jax.ShapeDtypeStruct((nb, D_MODEL), F32),
        compiler_params=pltpu.CompilerParams(
            dimension_semantics=("arbitrary",), vmem_limit_bytes=VMEM_LIMIT),
        name="outproj_sample",
    )(mg, xs, wout, prm["lng"], prm["lnb"])


def _cast_kernel(w_ref, o_ref):
    o_ref[...] = w_ref[...].astype(BF16)


def _cast_w_out(w_out):
    depth, d, n = w_out.shape
    rows = 512
    return pl.pallas_call(
        _cast_kernel,
        grid=(depth, d // rows),
        in_specs=[pl.BlockSpec((None, rows, n), lambda l, i: (l, i, 0))],
        out_specs=pl.BlockSpec((None, rows, n), lambda l, i: (l, i, 0)),
        out_shape=jax.ShapeDtypeStruct((depth, d, n), BF16),
        compiler_params=pltpu.CompilerParams(dimension_semantics=("arbitrary", "arbitrary")),
        name="cast_w_out",
    )(w_out)


def _lane_rows(*mats):
    v = jnp.concatenate([a.astype(F32) for a in mats], axis=1)
    return jnp.pad(v, ((0, 0), (0, LANES - v.shape[1])))[:, None, :]


def kernel(x_prompt, x_sample, state_C, state_n, state_m, state_conv, cache_k, cache_v, w_in, w_out,
           b_igate, b_fgate, m_norm_g, conv_w, conv_b, conv_ln_g, conv_ln_b, sinks, ln_g, ln_b):
    batch, seq, _ = x_prompt.shape
    nb = x_sample.shape[0]
    xp = x_prompt.reshape(batch * seq, D_MODEL)
    xs = x_sample.reshape(nb, D_MODEL)
    wt = jnp.swapaxes(w_in, 1, 2)
    wout = _cast_w_out(w_out)
    prm = {
        "gbias": _lane_rows(b_igate, b_fgate),
        "sink_row": _lane_rows(sinks),
        "mng": m_norm_g[:, None, :],
        "convw": jnp.pad(conv_w, ((0, 0), (0, 1), (0, 0))),
        "convb": conv_b[:, None, :],
        "clng": conv_ln_g[:, None, :],
        "clnb": conv_ln_b[:, None, :],
        "lng": ln_g[:, None, :],
        "lnb": ln_b[:, None, :],
    }
    m_in = jnp.pad(state_m, ((0, 0), (0, 0), (0, LANES - M_HEADS)))
    sconv = jnp.swapaxes(state_conv, 1, 2)
    ck = cache_k.reshape(DEPTH, nb, WINDOW, KV_WIDTH)
    cv = cache_v.reshape(DEPTH, nb, WINDOW, KV_WIDTH)
    st_p = []
    st_s = []
    hs_all = []
    for l in range(DEPTH):
        pm, pc, pa, g, conv_p, hs = _inproj(xp, xs, wt, prm, l, batch)
        xp, c_p, n_p, m_p, k_p, v_p = _mixer(l, sinks, pm, pc, pa, g, xp, wout, prm, batch, seq)
        st_p.append((c_p, n_p, m_p, conv_p, k_p, v_p))
        mg, *states = _sample_mixer(l, hs, state_C, state_n, m_in, sconv, ck, cv, prm)
        st_s.append(states)
        hs_all.append(hs)
        xs = _outproj_sample(mg, xs, wout, prm, l)
    c_s = _update_c(jnp.stack(hs_all, axis=0), state_C, m_in, prm)

    def stacked(per_layer):
        return [jnp.stack(a, axis=0) for a in zip(*per_layer)]

    def finish(c, nn, m, conv, k, v, n):
        return (c, nn, m.reshape(DEPTH, n, LANES)[:, :, :M_HEADS], conv,
                k.reshape(DEPTH, n, WINDOW, KV_HEADS, HEAD_DIM),
                v.reshape(DEPTH, n, WINDOW, KV_HEADS, HEAD_DIM))

    n_s, m_s, conv_s, k_s, v_s = stacked(st_s)
    return (xp.reshape(batch, seq, D_MODEL), xs.reshape(nb, 1, D_MODEL),
            *finish(*stacked(st_p), batch),
            *finish(c_s, n_s, m_s, jnp.swapaxes(conv_s, 1, 2), k_s, v_s, nb))
```

```python
import functools

import jax
import jax.numpy as jnp
from jax import lax
from jax.experimental import pallas as pl
from jax.experimental.pallas import tpu as pltpu

F32 = jnp.float32
BF16 = jnp.bfloat16

D_MODEL = 2048
DEPTH = 2
M_WIDTH = 1024
M_HEADS = 4
M_DK = 256
C_WIDTH = 512
CONV_K = 31
CONV_BUF = CONV_K - 1
A_WIDTH = 512
HEAD_DIM = 64
A_HEADS = 8
KV_HEADS = 2
GROUP = A_HEADS // KV_HEADS
KV_WIDTH = KV_HEADS * HEAD_DIM
WINDOW = 128
ALPHA = (2 * DEPTH) ** 0.25
LN_EPS = 1e-5
K_SCALE = M_DK ** -0.5
A_SCALE = HEAD_DIM ** -0.5

LANES = 128
SUBLANES = 8

HM_W = 5 * M_WIDTH
HC_W = 3 * C_WIDTH
HA_W = 2 * A_WIDTH + 2 * KV_WIDTH
MAIN_W = HM_W + HC_W + HA_W
GATE_END = MAIN_W + LANES
PROJ_W = GATE_END
GATE_OFF = 4 * M_WIDTH
N_GATES = 2 * M_HEADS
W_CHUNK = 256
W_SLOTS = 4
PLAIN_CHUNKS = GATE_OFF // W_CHUNK
MAIN_CHUNKS = MAIN_W // W_CHUNK
HA_K = A_WIDTH
HA_V = A_WIDTH + KV_WIDTH
HA_Z = A_WIDTH + 2 * KV_WIDTH

TM_PROJ = 256
TB = 256
A_BLK = 128
CONV_HALO = 32
SB = 8
VMEM_LIMIT = 56 * 1024 * 1024
INPROJ_VMEM_LIMIT = 60 * 1024 * 1024


def _sigmoid(x):
    return 1.0 / (1.0 + jnp.exp(-x))


def _log_sigmoid(x):
    return jnp.minimum(x, 0.0) - jnp.log1p(jnp.exp(-jnp.abs(x)))


def _layer_norm(x, g, b):
    mu = jnp.mean(x, axis=-1, keepdims=True)
    xc = x - mu
    var = jnp.mean(xc * xc, axis=-1, keepdims=True)
    return xc * lax.rsqrt(var + LN_EPS) * g + b


def _dot(a, b):
    return jnp.dot(a, b, preferred_element_type=F32)


def _dot_nt(a, b):
    return lax.dot_general(a, b, (((1,), (1,)), ((), ())), preferred_element_type=F32)


def _dot_tn(a, b):
    return lax.dot_general(a, b, (((0,), (0,)), ((), ())), preferred_element_type=F32)


PM_W = 4 * M_WIDTH
PA_QKV = A_WIDTH + 2 * KV_WIDTH
QKV_CHUNK = 512
SHIFTED_ROWS = CONV_HALO + TM_PROJ - SUBLANES


def _silu(x):
    return x * _sigmoid(x)


def _stage_rows(src_hbm, dst_ref, stage, sem, n_chunks, src_row):
    def chunk_copy(c, slot):
        return pltpu.make_async_copy(
            src_hbm.at[pl.ds(pl.multiple_of(src_row(c), SUBLANES), W_CHUNK), :],
            stage.at[slot], sem.at[slot])

    for c in range(W_SLOTS - 1):
        chunk_copy(c, c).start()

    def body(c, carry):
        slot = c % W_SLOTS
        chunk_copy(c, slot).wait()
        ahead = c + W_SLOTS - 1

        @pl.when(ahead < n_chunks)
        def _():
            chunk_copy(ahead, ahead % W_SLOTS).start()

        dst_ref[pl.ds(pl.multiple_of(c * W_CHUNK, W_CHUNK), W_CHUNK), :] = stage[slot].astype(BF16)
        return carry

    lax.fori_loop(0, n_chunks, body, 0)


def _stage_w_in(w_hbm, wt_ref, stage, sem):
    _stage_rows(w_hbm, wt_ref, stage, sem, MAIN_CHUNKS,
                lambda c: jnp.where(c < PLAIN_CHUNKS, c * W_CHUNK, c * W_CHUNK + N_GATES))
    stage[0, 0:LANES, :] = jnp.zeros((LANES, D_MODEL), F32)
    gate_copy = pltpu.make_async_copy(
        w_hbm.at[GATE_OFF:GATE_OFF + N_GATES, :], stage.at[0, 0:N_GATES, :], sem.at[0])
    gate_copy.start()
    gate_copy.wait()
    wt_ref[MAIN_W:GATE_END, :] = stage[0, 0:LANES, :].astype(BF16)


def _inproj_kernel(layer, blocks_per_seq, x_ref, xs_ref, w_hbm, convw_ref, convb_ref, clng_ref,
                   clnb_ref, pm_ref, pc_ref, pa_ref, g_ref, conv_ref, hs_ref,
                   wt_ref, stage, sem, xbuf, abuf, sbuf, ybuf, zbuf):
    i = pl.program_id(0)

    @pl.when(i == 0)
    def _():
        _stage_w_in(w_hbm.at[layer], wt_ref, stage, sem)
        hs_ref[...] = _dot_nt(xs_ref[...].astype(BF16), wt_ref[...])

    @pl.when(i % blocks_per_seq == 0)
    def _():
        abuf[0:CONV_HALO, :] = jnp.zeros((CONV_HALO, C_WIDTH), F32)

    xbuf[...] = x_ref[...].astype(BF16)

    def proj(c0, c1):
        return _dot_nt(xbuf[...], wt_ref[c0:c1, :])

    for c in range(0, 3 * M_WIDTH, QKV_CHUNK):
        pm_ref[:, c:c + QKV_CHUNK] = proj(c, c + QKV_CHUNK).astype(BF16)
    for h in range(M_HEADS):
        o = proj(3 * M_WIDTH + h * M_DK, 3 * M_WIDTH + (h + 1) * M_DK)
        z = proj(4 * M_WIDTH + h * M_DK, 4 * M_WIDTH + (h + 1) * M_DK)
        pm_ref[:, 3 * M_WIDTH + h * M_DK:3 * M_WIDTH + (h + 1) * M_DK] = (
            _sigmoid(o) * _silu(z)).astype(BF16)

    cu = proj(HM_W, HM_W + C_WIDTH)
    cg = proj(HM_W + C_WIDTH, HM_W + 2 * C_WIDTH)
    abuf[CONV_HALO:CONV_HALO + TM_PROJ, :] = cu * _sigmoid(cg)
    zbuf[...] = _silu(proj(HM_W + 2 * C_WIDTH, HM_W + 3 * C_WIDTH))
    for p in range(1, SUBLANES):
        sbuf[p - 1] = abuf[p:p + SHIFTED_ROWS, :]
    tap0 = CONV_HALO - CONV_BUF
    for r in range(0, TM_PROJ, 128):
        for c in range(0, C_WIDTH, LANES):
            acc = jnp.zeros((128, LANES), F32)
            for t in range(CONV_K):
                p = (tap0 + t) % SUBLANES
                base = r + tap0 + t - p
                src = abuf if p == 0 else sbuf.at[p - 1]
                acc = acc + src[base:base + 128, c:c + LANES] * convw_ref[t:t + 1, c:c + LANES]
            ybuf[r:r + 128, c:c + LANES] = acc + convb_ref[:, c:c + LANES]
    conv_ref[0] = abuf[CONV_HALO + TM_PROJ - CONV_BUF:CONV_HALO + TM_PROJ, :]
    abuf[0:CONV_HALO, :] = abuf[TM_PROJ:TM_PROJ + CONV_HALO, :]
    yc = _layer_norm(ybuf[...], clng_ref[...], clnb_ref[...])
    pc_ref[...] = (_silu(yc) * zbuf[...]).astype(BF16)

    base = HM_W + HC_W
    pa_ref[:, 0:PA_QKV] = proj(base, base + PA_QKV).astype(BF16)
    pa_ref[:, PA_QKV:HA_W] = _silu(proj(base + PA_QKV, base + HA_W)).astype(BF16)
    g_ref[...] = proj(MAIN_W, GATE_END)


def _inproj(x2d, xs, w_in_t, prm, layer, batch):
    m = x2d.shape[0]
    ns = xs.shape[0]
    nblk = m // TM_PROJ
    row = lambda i: (i, 0)
    once = lambda i: (0, 0)
    return pl.pallas_call(
        functools.partial(_inproj_kernel, layer, nblk // batch),
        grid=(nblk,),
        in_specs=[
            pl.BlockSpec((TM_PROJ, D_MODEL), row),
            pl.BlockSpec((ns, D_MODEL), once),
            pl.BlockSpec(memory_space=pl.ANY),
            _param_spec(C_WIDTH, layer, 1)(CONV_K + 1),
            _param_spec(C_WIDTH, layer, 1)(),
            _param_spec(C_WIDTH, layer, 1)(),
            _param_spec(C_WIDTH, layer, 1)(),
        ],
        out_specs=[
            pl.BlockSpec((TM_PROJ, PM_W), row),
            pl.BlockSpec((TM_PROJ, C_WIDTH), row),
            pl.BlockSpec((TM_PROJ, HA_W), row),
            pl.BlockSpec((TM_PROJ, LANES), row),
            pl.BlockSpec((1, CONV_BUF, C_WIDTH), lambda i: (i // (nblk // batch), 0, 0)),
            pl.BlockSpec((ns, PROJ_W), once),
        ],
        out_shape=[
            jax.ShapeDtypeStruct((m, PM_W), BF16),
            jax.ShapeDtypeStruct((m, C_WIDTH), BF16),
            jax.ShapeDtypeStruct((m, HA_W), BF16),
            jax.ShapeDtypeStruct((m, LANES), F32),
            jax.ShapeDtypeStruct((batch, CONV_BUF, C_WIDTH), F32),
            jax.ShapeDtypeStruct((ns, PROJ_W), F32),
        ],
        scratch_shapes=[
            pltpu.VMEM((PROJ_W, D_MODEL), BF16),
            pltpu.VMEM((W_SLOTS, W_CHUNK, D_MODEL), F32),
            pltpu.SemaphoreType.DMA((W_SLOTS,)),
            pltpu.VMEM((TM_PROJ, D_MODEL), BF16),
            pltpu.VMEM((CONV_HALO + TM_PROJ, C_WIDTH), F32),
            pltpu.VMEM((SUBLANES - 1, SHIFTED_ROWS, C_WIDTH), F32),
            pltpu.VMEM((TM_PROJ, C_WIDTH), F32),
            pltpu.VMEM((TM_PROJ, C_WIDTH), F32),
        ],
        compiler_params=pltpu.CompilerParams(
            dimension_semantics=("arbitrary",), vmem_limit_bytes=INPROJ_VMEM_LIMIT),
        name="inproj_prompt",
    )(x2d, xs, w_in_t, prm["convw"], prm["convb"], prm["clng"], prm["clnb"])


def _cumsum_rows(x):
    n = x.shape[0]
    row = lax.broadcasted_iota(jnp.int32, x.shape, 0)
    s = 1
    while s < n:
        x = x + jnp.where(row >= s, pltpu.roll(x, s, axis=0), 0.0)
        s *= 2
    return x


def _mixer_kernel(layer, sinks_ref, hm_ref, pc_ref, ha_ref, g_ref, x_ref, wout_ref,
                  gbias_ref, mng_ref, lng_ref, lnb_ref,
                  y_ref, c_ref, n_ref, m_ref, k_ref, v_ref,
                  kprev, vprev, merged):
    j = pl.program_id(1)

    @pl.when(j == 0)
    def _():
        c_ref[...] = jnp.zeros_like(c_ref)
        n_ref[...] = jnp.zeros_like(n_ref)
        m_ref[...] = jnp.zeros_like(m_ref)
        kprev[...] = jnp.zeros_like(kprev)
        vprev[...] = jnp.zeros_like(vprev)

    gb = g_ref[...] + gbias_ref[...]
    bsum = _cumsum_rows(_log_sigmoid(gb))
    bsum = pltpu.roll(bsum, LANES - M_HEADS, axis=1)
    a_all = gb - bsum
    a_t = a_all.T
    m_prev = m_ref[0]
    row_i = lax.broadcasted_iota(jnp.int32, (TB, TB), 0)
    col_i = lax.broadcasted_iota(jnp.int32, (TB, TB), 1)
    causal = col_i <= row_i
    lane_i = lax.broadcasted_iota(jnp.int32, (1, LANES), 1)
    m_new = m_prev

    for h in range(M_HEADS):
        cs = slice(h * M_DK, (h + 1) * M_DK)
        q = hm_ref[:, cs]
        k = hm_ref[:, M_WIDTH + h * M_DK:M_WIDTH + (h + 1) * M_DK]
        v = hm_ref[:, 2 * M_WIDTH + h * M_DK:2 * M_WIDTH + (h + 1) * M_DK]
        logit = jnp.where(causal, a_t[h:h + 1, :], -jnp.inf)
        m0 = m_prev[:, h:h + 1]
        mx = jnp.maximum(jnp.max(logit, axis=1, keepdims=True), m0)
        w = jnp.exp(logit - mx)
        w0 = jnp.exp(m0 - mx)
        s = _dot_nt(q, k) * (w * K_SCALE)
        c0 = c_ref[0, h]
        n0 = n_ref[0, h:h + 1, :]
        num = _dot(s.astype(BF16), v) + w0 * _dot(q, c0.astype(BF16))
        den = (jnp.sum(s, axis=1, keepdims=True)
               + w0 * jnp.sum(q.astype(F32) * n0, axis=1, keepdims=True))
        m_t = bsum[:, h:h + 1] + mx
        hh = num / jnp.maximum(jnp.abs(den), jnp.exp(-m_t))
        mu = jnp.mean(hh, axis=1, keepdims=True)
        hc = hh - mu
        var = jnp.mean(hc * hc, axis=1, keepdims=True)
        hn = hc * lax.rsqrt(var + LN_EPS) * mng_ref[:, cs]
        gate = hm_ref[:, 3 * M_WIDTH + h * M_DK:3 * M_WIDTH + (h + 1) * M_DK].astype(F32)
        merged[:, cs] = (hn * gate).astype(BF16)
        mx_end = mx[TB - 1:TB, :]
        g_col = jnp.exp(a_all[:, h:h + 1] - mx_end) * K_SCALE
        g0 = jnp.exp(m0 - mx_end)
        kg = k.astype(F32) * g_col
        c_ref[0, h] = g0 * c0 + _dot_tn(kg.astype(BF16), v)
        n_ref[0, h:h + 1, :] = g0 * n0 + jnp.sum(kg, axis=0, keepdims=True)
        m_end = bsum[TB - 1:TB, h:h + 1] + mx_end
        m_new = jnp.where(lane_i == h, m_end, m_new)
    m_ref[0] = m_new

    merged[:, M_WIDTH:M_WIDTH + C_WIDTH] = pc_ref[...]

    arow = lax.broadcasted_iota(jnp.int32, (A_BLK, 2 * A_BLK), 0)
    acol = lax.broadcasted_iota(jnp.int32, (A_BLK, 2 * A_BLK), 1)
    rel = A_BLK + arow - acol
    band = (rel >= 0) & (rel <= WINDOW)
    for sub in range(TB // A_BLK):
        r0 = sub * A_BLK
        if sub == 0:
            kcat = jnp.concatenate([kprev[...], ha_ref[0:A_BLK, HA_K:HA_K + KV_WIDTH]], axis=0)
            vcat = jnp.concatenate([vprev[...], ha_ref[0:A_BLK, HA_V:HA_V + KV_WIDTH]], axis=0)
            mask = band & (acol >= jnp.where(j > 0, 0, A_BLK))
        else:
            kcat = ha_ref[r0 - A_BLK:r0 + A_BLK, HA_K:HA_K + KV_WIDTH]
            vcat = ha_ref[r0 - A_BLK:r0 + A_BLK, HA_V:HA_V + KV_WIDTH]
            mask = band
        for hp in range(A_HEADS // 2):
            outs = []
            for hq in (2 * hp, 2 * hp + 1):
                kvh = hq // GROUP
                qh = ha_ref[r0:r0 + A_BLK, hq * HEAD_DIM:(hq + 1) * HEAD_DIM]
                kh = kcat[:, kvh * HEAD_DIM:(kvh + 1) * HEAD_DIM]
                vh = vcat[:, kvh * HEAD_DIM:(kvh + 1) * HEAD_DIM]
                sc = jnp.where(mask, _dot_nt(qh, kh) * A_SCALE, -jnp.inf)
                sink = sinks_ref[layer, hq]
                smax = jnp.maximum(jnp.max(sc, axis=1, keepdims=True), sink)
                e = jnp.exp(sc - smax)
                denom = jnp.sum(e, axis=1, keepdims=True) + jnp.exp(sink - smax)
                outs.append(_dot((e / denom).astype(BF16), vh))
            ao = jnp.concatenate(outs, axis=1)
            gate = ha_ref[r0:r0 + A_BLK, HA_Z + hp * LANES:HA_Z + (hp + 1) * LANES].astype(F32)
            col = M_WIDTH + C_WIDTH + hp * LANES
            merged[r0:r0 + A_BLK, col:col + LANES] = (ao * gate).astype(BF16)
    kprev[...] = ha_ref[TB - A_BLK:TB, HA_K:HA_K + KV_WIDTH]
    vprev[...] = ha_ref[TB - A_BLK:TB, HA_V:HA_V + KV_WIDTH]
    k_ref[0] = ha_ref[TB - WINDOW:TB, HA_K:HA_K + KV_WIDTH].astype(F32)
    v_ref[0] = ha_ref[TB - WINDOW:TB, HA_V:HA_V + KV_WIDTH].astype(F32)

    y = _dot(merged[...], wout_ref[...])
    y_ref[...] = _layer_norm(ALPHA * x_ref[...] + y, lng_ref[...], lnb_ref[...])


def _param_spec(width, layer, ngrid):
    if ngrid == 1:
        return lambda rows=1: pl.BlockSpec((None, rows, width), lambda i: (layer, 0, 0))
    return lambda rows=1: pl.BlockSpec((None, rows, width), lambda b, j: (layer, 0, 0))


def _mixer(layer, sinks, pm, pc, pa, g, x2d, wout, prm, batch, seq):
    nb = seq // TB
    row = lambda b, j: (b * nb + j, 0)
    per_b3 = lambda b, j: (b, 0, 0)
    state_shapes = [
        (batch, M_HEADS, M_DK, M_DK), (batch, M_HEADS, M_DK), (batch, 1, LANES),
        (batch, WINDOW, KV_WIDTH), (batch, WINDOW, KV_WIDTH)]
    return pl.pallas_call(
        functools.partial(_mixer_kernel, layer),
        grid=(batch, nb),
        in_specs=[
            pl.BlockSpec(memory_space=pltpu.SMEM),
            pl.BlockSpec((TB, PM_W), row),
            pl.BlockSpec((TB, C_WIDTH), row),
            pl.BlockSpec((TB, HA_W), row),
            pl.BlockSpec((TB, LANES), row),
            pl.BlockSpec((TB, D_MODEL), row),
            _param_spec(D_MODEL, layer, 2)(D_MODEL),
            _param_spec(LANES, layer, 2)(),
            _param_spec(M_WIDTH, layer, 2)(),
            _param_spec(D_MODEL, layer, 2)(),
            _param_spec(D_MODEL, layer, 2)(),
        ],
        out_specs=[
            pl.BlockSpec((TB, D_MODEL), row),
            pl.BlockSpec((1, M_HEADS, M_DK, M_DK), lambda b, j: (b, 0, 0, 0)),
            pl.BlockSpec((1, M_HEADS, M_DK), per_b3),
            pl.BlockSpec((1, 1, LANES), per_b3),
            pl.BlockSpec((1, WINDOW, KV_WIDTH), per_b3),
            pl.BlockSpec((1, WINDOW, KV_WIDTH), per_b3),
        ],
        out_shape=[jax.ShapeDtypeStruct((batch * seq, D_MODEL), F32)]
        + [jax.ShapeDtypeStruct(s, F32) for s in state_shapes],
        scratch_shapes=[
            pltpu.VMEM((A_BLK, KV_WIDTH), BF16),
            pltpu.VMEM((A_BLK, KV_WIDTH), BF16),
            pltpu.VMEM((TB, D_MODEL), BF16),
        ],
        compiler_params=pltpu.CompilerParams(
            dimension_semantics=("arbitrary", "arbitrary"), vmem_limit_bytes=VMEM_LIMIT),
        name="mixer_prompt",
    )(sinks, pm, pc, pa, g, x2d, wout, prm["gbias"], prm["mng"], prm["lng"], prm["lnb"])


def _sample_gates(hs_ref, gbias_ref, m_ref):
    gb = hs_ref[:, MAIN_W:GATE_END] + gbias_ref[...]
    lf = pltpu.roll(_log_sigmoid(gb), LANES - M_HEADS, axis=1)
    m0 = m_ref[...]
    m_new = jnp.maximum(lf + m0, gb)
    wgt = jnp.exp(gb - m_new)
    w0 = jnp.exp(lf + m0 - m_new)
    return m_new, wgt, w0


def _sample_mixer_kernel(hs_ref, c_ref, n_ref, m_ref, conv_ref, ck_ref, cv_ref,
                         gbias_ref, mng_ref, convw_ref, convb_ref, clng_ref, clnb_ref, sink_ref,
                         merged_ref, nout_ref, mout_ref, convout_ref, kout_ref, vout_ref):
    m_new, wgt, w0 = _sample_gates(hs_ref, gbias_ref, m_ref)
    einv = jnp.exp(-m_new)
    mout_ref[...] = m_new
    for h in range(M_HEADS):
        cs = slice(h * M_DK, (h + 1) * M_DK)
        q = hs_ref[:, cs]
        k = hs_ref[:, M_WIDTH + h * M_DK:M_WIDTH + (h + 1) * M_DK] * K_SCALE
        v = hs_ref[:, 2 * M_WIDTH + h * M_DK:2 * M_WIDTH + (h + 1) * M_DK]
        q_t = q.T
        wh = wgt[:, h:h + 1]
        w0h = w0[:, h:h + 1]
        n0 = n_ref[:, h, :]
        s = jnp.sum(q * k, axis=1, keepdims=True) * wh
        qc = jnp.concatenate(
            [jnp.sum(q_t[:, b:b + 1] * c_ref[b, h], axis=0, keepdims=True) for b in range(SB)],
            axis=0)
        num = s * v + w0h * qc
        den = s + w0h * jnp.sum(q * n0, axis=1, keepdims=True)
        hh = num / jnp.maximum(jnp.abs(den), einv[:, h:h + 1])
        nout_ref[:, h, :] = w0h * n0 + wh * k
        mu = jnp.mean(hh, axis=1, keepdims=True)
        hc = hh - mu
        var = jnp.mean(hc * hc, axis=1, keepdims=True)
        hn = hc * lax.rsqrt(var + LN_EPS) * mng_ref[:, cs]
        o = hs_ref[:, 3 * M_WIDTH + h * M_DK:3 * M_WIDTH + (h + 1) * M_DK]
        z = hs_ref[:, 4 * M_WIDTH + h * M_DK:4 * M_WIDTH + (h + 1) * M_DK]
        merged_ref[:, cs] = hn * _sigmoid(o) * (z * _sigmoid(z))

    cu = hs_ref[:, HM_W:HM_W + C_WIDTH]
    cg = hs_ref[:, HM_W + C_WIDTH:HM_W + 2 * C_WIDTH]
    cz = hs_ref[:, HM_W + 2 * C_WIDTH:HM_W + 3 * C_WIDTH]
    a = cu * _sigmoid(cg)
    yc = a * convw_ref[CONV_BUF:CONV_K, :] + convb_ref[...]
    for t in range(CONV_BUF):
        yc = yc + conv_ref[t] * convw_ref[t:t + 1, :]
    for t in range(CONV_BUF - 1):
        convout_ref[t] = conv_ref[t + 1]
    convout_ref[CONV_BUF - 1] = a
    yc = _layer_norm(yc, clng_ref[...], clnb_ref[...])
    merged_ref[:, M_WIDTH:M_WIDTH + C_WIDTH] = (yc * _sigmoid(yc)) * (cz * _sigmoid(cz))

    ab = HM_W + HC_W
    lane = lax.broadcasted_iota(jnp.int32, (1, LANES), 1)
    krow = lax.broadcasted_iota(jnp.int32, (2 * WINDOW, LANES), 0)
    wrow = lax.broadcasted_iota(jnp.int32, (WINDOW, LANES), 0)
    sink_row = sink_ref[...]
    zpad = jnp.zeros((WINDOW - 1, LANES), F32)
    for b in range(SB):
        knew = hs_ref[b:b + 1, ab + HA_K:ab + HA_K + KV_WIDTH]
        vnew = hs_ref[b:b + 1, ab + HA_V:ab + HA_V + KV_WIDTH]
        ck = ck_ref[b]
        cv = cv_ref[b]
        qrows = []
        for hq in range(A_HEADS):
            kvh = hq // GROUP
            t, u = hq // 2, hq % 2
            tile = hs_ref[b:b + 1, ab + t * LANES:ab + (t + 1) * LANES]
            if u != kvh:
                tile = pltpu.roll(tile, HEAD_DIM, axis=1)
            qrows.append(jnp.where((lane // HEAD_DIM) == kvh, tile, 0.0))
        qsel = jnp.concatenate(qrows + [jnp.zeros((LANES - A_HEADS, LANES), F32)], axis=0)
        kext = jnp.concatenate([ck, knew, zpad], axis=0)
        vext = jnp.concatenate([cv, vnew, zpad], axis=0)
        sc = _dot_nt(kext.astype(BF16), qsel.astype(BF16)) * A_SCALE
        sc = jnp.where(krow <= WINDOW, sc, -jnp.inf)
        smax = jnp.maximum(jnp.max(sc, axis=0, keepdims=True), sink_row)
        e = jnp.exp(sc - smax)
        denom = jnp.sum(e, axis=0, keepdims=True) + jnp.exp(sink_row - smax)
        p = e / denom
        ob = _dot_tn(p.astype(BF16), vext.astype(BF16))
        tiles = []
        for t in range(A_HEADS // 2):
            halves = []
            for u in range(2):
                hq = 2 * t + u
                kvh = hq // GROUP
                x = ob[hq:hq + 1, :]
                if u != kvh:
                    x = pltpu.roll(x, HEAD_DIM, axis=1)
                halves.append(x)
            tiles.append(jnp.where(lane < HEAD_DIM, halves[0], halves[1]))
        ao = jnp.concatenate(tiles, axis=1)
        az = hs_ref[b:b + 1, ab + HA_Z:ab + HA_Z + A_WIDTH]
        merged_ref[b:b + 1, M_WIDTH + C_WIDTH:D_MODEL] = ao * (az * _sigmoid(az))
        kout_ref[b] = jnp.where(wrow == WINDOW - 1, knew, pltpu.roll(ck, WINDOW - 1, axis=0))
        vout_ref[b] = jnp.where(wrow == WINDOW - 1, vnew, pltpu.roll(cv, WINDOW - 1, axis=0))


def _sample_mixer(layer, hs, c, n, m, conv, ck, cv, prm):
    nb = hs.shape[0]
    row2 = lambda i: (i, 0)

    def state_specs(lead, pre):
        return [
            pl.BlockSpec(lead + (SB, M_HEADS, M_DK), lambda i: pre + (i, 0, 0)),
            pl.BlockSpec(lead + (SB, LANES), lambda i: pre + (i, 0)),
            pl.BlockSpec(lead + (CONV_BUF, SB, C_WIDTH), lambda i: pre + (0, i, 0)),
            pl.BlockSpec(lead + (SB, WINDOW, KV_WIDTH), lambda i: pre + (i, 0, 0)),
            pl.BlockSpec(lead + (SB, WINDOW, KV_WIDTH), lambda i: pre + (i, 0, 0)),
        ]

    c_spec = pl.BlockSpec((None, SB, M_HEADS, M_DK, M_DK), lambda i: (layer, i, 0, 0, 0))
    state_shapes = [
        (nb, M_HEADS, M_DK), (nb, LANES),
        (CONV_BUF, nb, C_WIDTH), (nb, WINDOW, KV_WIDTH), (nb, WINDOW, KV_WIDTH)]
    return pl.pallas_call(
        _sample_mixer_kernel,
        grid=(nb // SB,),
        in_specs=[pl.BlockSpec((SB, PROJ_W), row2), c_spec] + state_specs((None,), (layer,)) + [
            _param_spec(LANES, layer, 1)(),
            _param_spec(M_WIDTH, layer, 1)(),
            _param_spec(C_WIDTH, layer, 1)(CONV_K + 1),
            _param_spec(C_WIDTH, layer, 1)(),
            _param_spec(C_WIDTH, layer, 1)(),
            _param_spec(C_WIDTH, layer, 1)(),
            _param_spec(LANES, layer, 1)(),
        ],
        out_specs=[pl.BlockSpec((SB, D_MODEL), row2)] + state_specs((), ()),
        out_shape=[jax.ShapeDtypeStruct((nb, D_MODEL), F32)]
        + [jax.ShapeDtypeStruct(s, F32) for s in state_shapes],
        compiler_params=pltpu.CompilerParams(
            dimension_semantics=("arbitrary",), vmem_limit_bytes=VMEM_LIMIT),
        name="mixer_sample",
    )(hs, c, n, m, conv, ck, cv, prm["gbias"], prm["mng"], prm["convw"], prm["convb"],
      prm["clng"], prm["clnb"], prm["sink_row"])


def _update_c_kernel(hs_ref, c_ref, m_ref, gbias_ref, cout_ref):
    _, wgt, w0 = _sample_gates(hs_ref, gbias_ref, m_ref)
    for h in range(M_HEADS):
        k_t = (hs_ref[:, M_WIDTH + h * M_DK:M_WIDTH + (h + 1) * M_DK] * K_SCALE).T
        v = hs_ref[:, 2 * M_WIDTH + h * M_DK:2 * M_WIDTH + (h + 1) * M_DK]
        gv = wgt[:, h:h + 1] * v
        w0h = w0[:, h:h + 1]
        for b in range(SB):
            cout_ref[b, h] = w0h[b:b + 1, :] * c_ref[b, h] + k_t[:, b:b + 1] * gv[b:b + 1, :]


def _update_c(hs_all, c, m, prm):
    depth, nb = hs_all.shape[:2]
    per = lambda l, i: (l, i, 0)
    c_spec = pl.BlockSpec((None, SB, M_HEADS, M_DK, M_DK), lambda l, i: (l, i, 0, 0, 0))
    return pl.pallas_call(
        _update_c_kernel,
        grid=(depth, nb // SB),
        in_specs=[pl.BlockSpec((None, SB, PROJ_W), per), c_spec,
                  pl.BlockSpec((None, SB, LANES), per),
                  pl.BlockSpec((None, 1, LANES), lambda l, i: (l, 0, 0))],
        out_specs=c_spec,
        out_shape=jax.ShapeDtypeStruct(c.shape, F32),
        compiler_params=pltpu.CompilerParams(
            dimension_semantics=("arbitrary", "arbitrary"), vmem_limit_bytes=VMEM_LIMIT),
        name="update_matrix_memory",
    )(hs_all, c, m, prm["gbias"])


def _outproj_sample_kernel(mg_ref, x_ref, wout_ref, lng_ref, lnb_ref, y_ref):
    y = _dot(mg_ref[...].astype(BF16), wout_ref[...])
    y_ref[...] = _layer_norm(ALPHA * x_ref[...] + y, lng_ref[...], lnb_ref[...])


def _outproj_sample(mg, xs, wout, prm, layer):
    nb = xs.shape[0]
    full = lambda shape: pl.BlockSpec(shape, lambda i: (0, 0))
    return pl.pallas_call(
        _outproj_sample_kernel,
        grid=(1,),
        in_specs=[full((nb, D_MODEL)), full((nb, D_MODEL)),
                  _param_spec(D_MODEL, layer, 1)(D_MODEL),
                  _param_spec(D_MODEL, layer, 1)(), _param_spec(D_MODEL, layer, 1)()],
        out_specs=full((nb, D_MODEL)),
        out_shape=jax.ShapeDtypeStruct((nb, D_MODEL), F32),
        compiler_params=pltpu.CompilerParams(
            dimension_semantics=("arbitrary",), vmem_limit_bytes=VMEM_LIMIT),
        name="outproj_sample",
    )(mg, xs, wout, prm["lng"], prm["lnb"])


def _cast_kernel(w_ref, o_ref):
    o_ref[...] = w_ref[...].astype(BF16)


def _cast_w_out(w_out):
    depth, d, n = w_out.shape
    rows = 512
    return pl.pallas_call(
        _cast_kernel,
        grid=(depth, d // rows),
        in_specs=[pl.BlockSpec((None, rows, n), lambda l, i: (l, i, 0))],
        out_specs=pl.BlockSpec((None, rows, n), lambda l, i: (l, i, 0)),
        out_shape=jax.ShapeDtypeStruct((depth, d, n), BF16),
        compiler_params=pltpu.CompilerParams(dimension_semantics=("arbitrary", "arbitrary")),
        name="cast_w_out",
    )(w_out)


def _lane_rows(*mats):
    v = jnp.concatenate([a.astype(F32) for a in mats], axis=1)
    return jnp.pad(v, ((0, 0), (0, LANES - v.shape[1])))[:, None, :]


def kernel(x_prompt, x_sample, state_C, state_n, state_m, state_conv, cache_k, cache_v, w_in, w_out,
           b_igate, b_fgate, m_norm_g, conv_w, conv_b, conv_ln_g, conv_ln_b, sinks, ln_g, ln_b):
    batch, seq, _ = x_prompt.shape
    nb = x_sample.shape[0]
    xp = x_prompt.reshape(batch * seq, D_MODEL)
    xs = x_sample.reshape(nb, D_MODEL)
    wt = jnp.swapaxes(w_in, 1, 2)
    wout = _cast_w_out(w_out)
    prm = {
        "gbias": _lane_rows(b_igate, b_fgate),
        "sink_row": _lane_rows(sinks),
        "mng": m_norm_g[:, None, :],
        "convw": jnp.pad(conv_w, ((0, 0), (0, 1), (0, 0))),
        "convb": conv_b[:, None, :],
        "clng": conv_ln_g[:, None, :],
        "clnb": conv_ln_b[:, None, :],
        "lng": ln_g[:, None, :],
        "lnb": ln_b[:, None, :],
    }
    m_in = jnp.pad(state_m, ((0, 0), (0, 0), (0, LANES - M_HEADS)))
    sconv = jnp.swapaxes(state_conv, 1, 2)
    ck = cache_k.reshape(DEPTH, nb, WINDOW, KV_WIDTH)
    cv = cache_v.reshape(DEPTH, nb, WINDOW, KV_WIDTH)
    st_p = []
    st_s = []
    hs_all = []
    for l in range(DEPTH):
        pm, pc, pa, g, conv_p, hs = _inproj(xp, xs, wt, prm, l, batch)
        xp, c_p, n_p, m_p, k_p, v_p = _mixer(l, sinks, pm, pc, pa, g, xp, wout, prm, batch, seq)
        st_p.append((c_p, n_p, m_p, conv_p, k_p, v_p))
        mg, *states = _sample_mixer(l, hs, state_C, state_n, m_in, sconv, ck, cv, prm)
        st_s.append(states)
        hs_all.append(hs)
        xs = _outproj_sample(mg, xs, wout, prm, l)
    c_s = _update_c(jnp.stack(hs_all, axis=0), state_C, m_in, prm)

    def stacked(per_layer):
        return [jnp.stack(a, axis=0) for a in zip(*per_layer)]

    def finish(c, nn, m, conv, k, v, n):
        return (c, nn, m.reshape(DEPTH, n, LANES)[:, :, :M_HEADS], conv,
                k.reshape(DEPTH, n, WINDOW, KV_HEADS, HEAD_DIM),
                v.reshape(DEPTH, n, WINDOW, KV_HEADS, HEAD_DIM))

    n_s, m_s, conv_s, k_s, v_s = stacked(st_s)
    return (xp.reshape(batch, seq, D_MODEL), xs.reshape(nb, 1, D_MODEL),
            *finish(*stacked(st_p), batch),
            *finish(c_s, n_s, m_s, jnp.swapaxes(conv_s, 1, 2), k_s, v_s, nb))
```

```python
import functools

import jax
import jax.numpy as jnp
from jax import lax
from jax.experimental import pallas as pl
from jax.experimental.pallas import tpu as pltpu

F32 = jnp.float32
BF16 = jnp.bfloat16

D_MODEL = 2048
DEPTH = 2
M_WIDTH = 1024
M_HEADS = 4
M_DK = 256
C_WIDTH = 512
CONV_K = 31
CONV_BUF = CONV_K - 1
A_WIDTH = 512
HEAD_DIM = 64
A_HEADS = 8
KV_HEADS = 2
GROUP = A_HEADS // KV_HEADS
KV_WIDTH = KV_HEADS * HEAD_DIM
WINDOW = 128
ALPHA = (2 * DEPTH) ** 0.25
LN_EPS = 1e-5
K_SCALE = M_DK ** -0.5
A_SCALE = HEAD_DIM ** -0.5

LANES = 128
SUBLANES = 8

HM_W = 5 * M_WIDTH
HC_W = 3 * C_WIDTH
HA_W = 2 * A_WIDTH + 2 * KV_WIDTH
MAIN_W = HM_W + HC_W + HA_W
GATE_END = MAIN_W + LANES
PROJ_W = GATE_END
GATE_OFF = 4 * M_WIDTH
N_GATES = 2 * M_HEADS
W_CHUNK = 256
W_SLOTS = 4
PLAIN_CHUNKS = GATE_OFF // W_CHUNK
MAIN_CHUNKS = MAIN_W // W_CHUNK
HA_K = A_WIDTH
HA_V = A_WIDTH + KV_WIDTH
HA_Z = A_WIDTH + 2 * KV_WIDTH

TM_PROJ = 256
TB = 256
A_BLK = 128
CONV_HALO = 32
SB = 8
VMEM_LIMIT = 56 * 1024 * 1024
INPROJ_VMEM_LIMIT = 60 * 1024 * 1024


def _sigmoid(x):
    return 1.0 / (1.0 + jnp.exp(-x))


def _silu(x):
    return x * _sigmoid(x)


def _log_sigmoid(x):
    return jnp.minimum(x, 0.0) - jnp.log1p(jnp.exp(-jnp.abs(x)))


def _layer_norm(x, g, b):
    mu = jnp.mean(x, axis=-1, keepdims=True)
    xc = x - mu
    var = jnp.mean(xc * xc, axis=-1, keepdims=True)
    return xc * lax.rsqrt(var + LN_EPS) * g + b


def _dot(a, b):
    return jnp.dot(a, b, preferred_element_type=F32)


def _dot_nt(a, b):
    return lax.dot_general(a, b, (((1,), (1,)), ((), ())), preferred_element_type=F32)


def _dot_tn(a, b):
    return lax.dot_general(a, b, (((0,), (0,)), ((), ())), preferred_element_type=F32)


def _whole(arr):
    zeros = (0,) * arr.ndim
    return pl.BlockSpec(arr.shape, lambda *_: zeros)


PM_W = 4 * M_WIDTH
PA_QKV = A_WIDTH + 2 * KV_WIDTH
QKV_CHUNK = 512
SHIFTED_ROWS = CONV_HALO + TM_PROJ - SUBLANES


def _stage_rows(src_hbm, dst_ref, stage, sem, n_chunks, src_row):
    def chunk_copy(c, slot):
        return pltpu.make_async_copy(
            src_hbm.at[pl.ds(pl.multiple_of(src_row(c), SUBLANES), W_CHUNK), :],
            stage.at[slot], sem.at[slot])

    for c in range(W_SLOTS - 1):
        chunk_copy(c, c).start()

    def body(c, carry):
        slot = c % W_SLOTS
        chunk_copy(c, slot).wait()
        ahead = c + W_SLOTS - 1

        @pl.when(ahead < n_chunks)
        def _():
            chunk_copy(ahead, ahead % W_SLOTS).start()

        dst_ref[pl.ds(pl.multiple_of(c * W_CHUNK, W_CHUNK), W_CHUNK), :] = stage[slot].astype(BF16)
        return carry

    lax.fori_loop(0, n_chunks, body, 0)


def _stage_w_in(w_hbm, wt_ref, stage, sem):
    _stage_rows(w_hbm, wt_ref, stage, sem, MAIN_CHUNKS,
                lambda c: jnp.where(c < PLAIN_CHUNKS, c * W_CHUNK, c * W_CHUNK + N_GATES))
    stage[0, 0:LANES, :] = jnp.zeros((LANES, D_MODEL), F32)
    gate_copy = pltpu.make_async_copy(
        w_hbm.at[GATE_OFF:GATE_OFF + N_GATES, :], stage.at[0, 0:N_GATES, :], sem.at[0])
    gate_copy.start()
    gate_copy.wait()
    wt_ref[MAIN_W:GATE_END, :] = stage[0, 0:LANES, :].astype(BF16)


def _inproj_kernel(layer, blocks_per_seq, x_ref, xs_ref, w_hbm, convw_ref, convb_ref, clng_ref,
                   clnb_ref, pm_ref, pc_ref, pa_ref, g_ref, conv_ref, hs_ref,
                   wt_ref, stage, sem, xbuf, abuf, sbuf, ybuf, zbuf):
    i = pl.program_id(0)
    lrow = slice(layer, layer + 1)

    @pl.when(i == 0)
    def _():
        _stage_w_in(w_hbm.at[layer], wt_ref, stage, sem)
        hs_ref[...] = _dot_nt(xs_ref[...].astype(BF16), wt_ref[...])

    @pl.when(i % blocks_per_seq == 0)
    def _():
        abuf[0:CONV_HALO, :] = jnp.zeros((CONV_HALO, C_WIDTH), F32)

    xbuf[...] = x_ref[...].astype(BF16)

    def proj(c0, c1):
        return _dot_nt(xbuf[...], wt_ref[c0:c1, :])

    for c in range(0, 3 * M_WIDTH, QKV_CHUNK):
        pm_ref[:, c:c + QKV_CHUNK] = proj(c, c + QKV_CHUNK).astype(BF16)
    for h in range(M_HEADS):
        o = proj(3 * M_WIDTH + h * M_DK, 3 * M_WIDTH + (h + 1) * M_DK)
        z = proj(4 * M_WIDTH + h * M_DK, 4 * M_WIDTH + (h + 1) * M_DK)
        pm_ref[:, 3 * M_WIDTH + h * M_DK:3 * M_WIDTH + (h + 1) * M_DK] = (
            _sigmoid(o) * _silu(z)).astype(BF16)

    cu = proj(HM_W, HM_W + C_WIDTH)
    cg = proj(HM_W + C_WIDTH, HM_W + 2 * C_WIDTH)
    abuf[CONV_HALO:CONV_HALO + TM_PROJ, :] = cu * _sigmoid(cg)
    zbuf[...] = _silu(proj(HM_W + 2 * C_WIDTH, HM_W + 3 * C_WIDTH))
    for p in range(1, SUBLANES):
        sbuf[p - 1] = abuf[p:p + SHIFTED_ROWS, :]
    tap0 = CONV_HALO - CONV_BUF
    for r in range(0, TM_PROJ, 128):
        for c in range(0, C_WIDTH, LANES):
            acc = jnp.zeros((128, LANES), F32)
            for t in range(CONV_K):
                p = (tap0 + t) % SUBLANES
                base = r + tap0 + t - p
                src = abuf if p == 0 else sbuf.at[p - 1]
                acc = acc + src[base:base + 128, c:c + LANES] * convw_ref[layer, t:t + 1, c:c + LANES]
            ybuf[r:r + 128, c:c + LANES] = acc + convb_ref[lrow, c:c + LANES]
    conv_ref[0] = abuf[CONV_HALO + TM_PROJ - CONV_BUF:CONV_HALO + TM_PROJ, :]
    abuf[0:CONV_HALO, :] = abuf[TM_PROJ:TM_PROJ + CONV_HALO, :]
    yc = _layer_norm(ybuf[...], clng_ref[lrow, :], clnb_ref[lrow, :])
    pc_ref[...] = (_silu(yc) * zbuf[...]).astype(BF16)

    base = HM_W + HC_W
    pa_ref[:, 0:PA_QKV] = proj(base, base + PA_QKV).astype(BF16)
    pa_ref[:, PA_QKV:HA_W] = _silu(proj(base + PA_QKV, base + HA_W)).astype(BF16)
    g_ref[...] = proj(MAIN_W, GATE_END)


def _inproj(x2d, xs, w_in_t, prm, layer, batch):
    m = x2d.shape[0]
    ns = xs.shape[0]
    nblk = m // TM_PROJ
    row = lambda i: (i, 0)
    once = lambda i: (0, 0)
    params = [prm["convw"], prm["convb"], prm["clng"], prm["clnb"]]
    return pl.pallas_call(
        functools.partial(_inproj_kernel, layer, nblk // batch),
        grid=(nblk,),
        in_specs=[
            pl.BlockSpec((TM_PROJ, D_MODEL), row),
            pl.BlockSpec((ns, D_MODEL), once),
            pl.BlockSpec(memory_space=pl.ANY),
        ] + [_whole(a) for a in params],
        out_specs=[
            pl.BlockSpec((TM_PROJ, PM_W), row),
            pl.BlockSpec((TM_PROJ, C_WIDTH), row),
            pl.BlockSpec((TM_PROJ, HA_W), row),
            pl.BlockSpec((TM_PROJ, LANES), row),
            pl.BlockSpec((1, CONV_BUF, C_WIDTH), lambda i: (i // (nblk // batch), 0, 0)),
            pl.BlockSpec((ns, PROJ_W), once),
        ],
        out_shape=[
            jax.ShapeDtypeStruct((m, PM_W), BF16),
            jax.ShapeDtypeStruct((m, C_WIDTH), BF16),
            jax.ShapeDtypeStruct((m, HA_W), BF16),
            jax.ShapeDtypeStruct((m, LANES), F32),
            jax.ShapeDtypeStruct((batch, CONV_BUF, C_WIDTH), F32),
            jax.ShapeDtypeStruct((ns, PROJ_W), F32),
        ],
        scratch_shapes=[
            pltpu.VMEM((PROJ_W, D_MODEL), BF16),
            pltpu.VMEM((W_SLOTS, W_CHUNK, D_MODEL), F32),
            pltpu.SemaphoreType.DMA((W_SLOTS,)),
            pltpu.VMEM((TM_PROJ, D_MODEL), BF16),
            pltpu.VMEM((CONV_HALO + TM_PROJ, C_WIDTH), F32),
            pltpu.VMEM((SUBLANES - 1, SHIFTED_ROWS, C_WIDTH), F32),
            pltpu.VMEM((TM_PROJ, C_WIDTH), F32),
            pltpu.VMEM((TM_PROJ, C_WIDTH), F32),
        ],
        compiler_params=pltpu.CompilerParams(
            dimension_semantics=("arbitrary",), vmem_limit_bytes=INPROJ_VMEM_LIMIT),
        name="inproj_prompt",
    )(x2d, xs, w_in_t, *params)


def _cumsum_rows(x):
    n = x.shape[0]
    row = lax.broadcasted_iota(jnp.int32, x.shape, 0)
    s = 1
    while s < n:
        x = x + jnp.where(row >= s, pltpu.roll(x, s, axis=0), 0.0)
        s *= 2
    return x


def _mixer_kernel(layer, sinks_ref, hm_ref, pc_ref, ha_ref, g_ref, x_ref, wout_ref,
                  gbias_ref, mng_ref, lng_ref, lnb_ref,
                  y_ref, c_ref, n_ref, m_ref, k_ref, v_ref,
                  kprev, vprev, merged):
    j = pl.program_id(1)
    lrow = slice(layer, layer + 1)

    @pl.when(j == 0)
    def _():
        c_ref[...] = jnp.zeros_like(c_ref)
        n_ref[...] = jnp.zeros_like(n_ref)
        m_ref[...] = jnp.zeros_like(m_ref)
        kprev[...] = jnp.zeros_like(kprev)
        vprev[...] = jnp.zeros_like(vprev)

    gb = g_ref[...] + gbias_ref[lrow, :]
    bsum = _cumsum_rows(_log_sigmoid(gb))
    bsum = pltpu.roll(bsum, LANES - M_HEADS, axis=1)
    a_all = gb - bsum
    a_t = a_all.T
    m_prev = m_ref[0]
    row_i = lax.broadcasted_iota(jnp.int32, (TB, TB), 0)
    col_i = lax.broadcasted_iota(jnp.int32, (TB, TB), 1)
    causal = col_i <= row_i
    lane_i = lax.broadcasted_iota(jnp.int32, (1, LANES), 1)
    m_new = m_prev

    for h in range(M_HEADS):
        cs = slice(h * M_DK, (h + 1) * M_DK)
        q = hm_ref[:, cs]
        k = hm_ref[:, M_WIDTH + h * M_DK:M_WIDTH + (h + 1) * M_DK]
        v = hm_ref[:, 2 * M_WIDTH + h * M_DK:2 * M_WIDTH + (h + 1) * M_DK]
        logit = jnp.where(causal, a_t[h:h + 1, :], -jnp.inf)
        m0 = m_prev[:, h:h + 1]
        mx = jnp.maximum(jnp.max(logit, axis=1, keepdims=True), m0)
        w = jnp.exp(logit - mx)
        w0 = jnp.exp(m0 - mx)
        s = _dot_nt(q, k) * (w * K_SCALE)
        c0 = c_ref[0, h]
        n0 = n_ref[0, h:h + 1, :]
        num = _dot(s.astype(BF16), v) + w0 * _dot(q, c0.astype(BF16))
        den = (jnp.sum(s, axis=1, keepdims=True)
               + w0 * jnp.sum(q.astype(F32) * n0, axis=1, keepdims=True))
        m_t = bsum[:, h:h + 1] + mx
        hh = num / jnp.maximum(jnp.abs(den), jnp.exp(-m_t))
        mu = jnp.mean(hh, axis=1, keepdims=True)
        hc = hh - mu
        var = jnp.mean(hc * hc, axis=1, keepdims=True)
        hn = hc * lax.rsqrt(var + LN_EPS) * mng_ref[lrow, cs]
        gate = hm_ref[:, 3 * M_WIDTH + h * M_DK:3 * M_WIDTH + (h + 1) * M_DK].astype(F32)
        merged[:, cs] = (hn * gate).astype(BF16)
        mx_end = mx[TB - 1:TB, :]
        g_col = jnp.exp(a_all[:, h:h + 1] - mx_end) * K_SCALE
        g0 = jnp.exp(m0 - mx_end)
        kg = k.astype(F32) * g_col
        c_ref[0, h] = g0 * c0 + _dot_tn(kg.astype(BF16), v)
        n_ref[0, h:h + 1, :] = g0 * n0 + jnp.sum(kg, axis=0, keepdims=True)
        m_end = bsum[TB - 1:TB, h:h + 1] + mx_end
        m_new = jnp.where(lane_i == h, m_end, m_new)
    m_ref[0] = m_new

    merged[:, M_WIDTH:M_WIDTH + C_WIDTH] = pc_ref[...]

    arow = lax.broadcasted_iota(jnp.int32, (A_BLK, 2 * A_BLK), 0)
    acol = lax.broadcasted_iota(jnp.int32, (A_BLK, 2 * A_BLK), 1)
    rel = A_BLK + arow - acol
    band = (rel >= 0) & (rel <= WINDOW)
    for sub in range(TB // A_BLK):
        r0 = sub * A_BLK
        if sub == 0:
            kcat = jnp.concatenate([kprev[...], ha_ref[0:A_BLK, HA_K:HA_K + KV_WIDTH]], axis=0)
            vcat = jnp.concatenate([vprev[...], ha_ref[0:A_BLK, HA_V:HA_V + KV_WIDTH]], axis=0)
            mask = band & (acol >= jnp.where(j > 0, 0, A_BLK))
        else:
            kcat = ha_ref[r0 - A_BLK:r0 + A_BLK, HA_K:HA_K + KV_WIDTH]
            vcat = ha_ref[r0 - A_BLK:r0 + A_BLK, HA_V:HA_V + KV_WIDTH]
            mask = band
        for hp in range(A_HEADS // 2):
            outs = []
            for hq in (2 * hp, 2 * hp + 1):
                kvh = hq // GROUP
                qh = ha_ref[r0:r0 + A_BLK, hq * HEAD_DIM:(hq + 1) * HEAD_DIM]
                kh = kcat[:, kvh * HEAD_DIM:(kvh + 1) * HEAD_DIM]
                vh = vcat[:, kvh * HEAD_DIM:(kvh + 1) * HEAD_DIM]
                sc = jnp.where(mask, _dot_nt(qh, kh) * A_SCALE, -jnp.inf)
                sink = sinks_ref[layer, hq]
                smax = jnp.maximum(jnp.max(sc, axis=1, keepdims=True), sink)
                e = jnp.exp(sc - smax)
                denom = jnp.sum(e, axis=1, keepdims=True) + jnp.exp(sink - smax)
                outs.append(_dot((e / denom).astype(BF16), vh))
            ao = jnp.concatenate(outs, axis=1)
            gate = ha_ref[r0:r0 + A_BLK, HA_Z + hp * LANES:HA_Z + (hp + 1) * LANES].astype(F32)
            col = M_WIDTH + C_WIDTH + hp * LANES
            merged[r0:r0 + A_BLK, col:col + LANES] = (ao * gate).astype(BF16)
    kprev[...] = ha_ref[TB - A_BLK:TB, HA_K:HA_K + KV_WIDTH]
    vprev[...] = ha_ref[TB - A_BLK:TB, HA_V:HA_V + KV_WIDTH]
    k_ref[0] = ha_ref[TB - WINDOW:TB, HA_K:HA_K + KV_WIDTH].astype(F32)
    v_ref[0] = ha_ref[TB - WINDOW:TB, HA_V:HA_V + KV_WIDTH].astype(F32)

    y = _dot(merged[...], wout_ref[...])
    y_ref[...] = _layer_norm(ALPHA * x_ref[...] + y, lng_ref[lrow, :], lnb_ref[lrow, :])


def _mixer(layer, sinks, pm, pc, pa, g, x2d, wout, prm, batch, seq):
    nb = seq // TB
    row = lambda b, j: (b * nb + j, 0)
    per_b3 = lambda b, j: (b, 0, 0)
    state_shapes = [
        (batch, M_HEADS, M_DK, M_DK), (batch, M_HEADS, M_DK), (batch, 1, LANES),
        (batch, WINDOW, KV_WIDTH), (batch, WINDOW, KV_WIDTH)]
    params = [prm["gbias"], prm["mng"], prm["lng"], prm["lnb"]]
    return pl.pallas_call(
        functools.partial(_mixer_kernel, layer),
        grid=(batch, nb),
        in_specs=[
            pl.BlockSpec(memory_space=pltpu.SMEM),
            pl.BlockSpec((TB, PM_W), row),
            pl.BlockSpec((TB, C_WIDTH), row),
            pl.BlockSpec((TB, HA_W), row),
            pl.BlockSpec((TB, LANES), row),
            pl.BlockSpec((TB, D_MODEL), row),
            pl.BlockSpec((None, D_MODEL, D_MODEL), lambda b, j: (layer, 0, 0)),
        ] + [_whole(a) for a in params],
        out_specs=[
            pl.BlockSpec((TB, D_MODEL), row),
            pl.BlockSpec((1, M_HEADS, M_DK, M_DK), lambda b, j: (b, 0, 0, 0)),
            pl.BlockSpec((1, M_HEADS, M_DK), per_b3),
            pl.BlockSpec((1, 1, LANES), per_b3),
            pl.BlockSpec((1, WINDOW, KV_WIDTH), per_b3),
            pl.BlockSpec((1, WINDOW, KV_WIDTH), per_b3),
        ],
        out_shape=[jax.ShapeDtypeStruct((batch * seq, D_MODEL), F32)]
        + [jax.ShapeDtypeStruct(s, F32) for s in state_shapes],
        scratch_shapes=[
            pltpu.VMEM((A_BLK, KV_WIDTH), BF16),
            pltpu.VMEM((A_BLK, KV_WIDTH), BF16),
            pltpu.VMEM((TB, D_MODEL), BF16),
        ],
        compiler_params=pltpu.CompilerParams(
            dimension_semantics=("arbitrary", "arbitrary"), vmem_limit_bytes=VMEM_LIMIT),
        name="mixer_prompt",
    )(sinks, pm, pc, pa, g, x2d, wout, *params)


def _sample_gates(hs_ref, gbias_row, m_ref):
    gb = hs_ref[:, MAIN_W:GATE_END] + gbias_row
    lf = pltpu.roll(_log_sigmoid(gb), LANES - M_HEADS, axis=1)
    m0 = m_ref[...]
    m_new = jnp.maximum(lf + m0, gb)
    wgt = jnp.exp(gb - m_new)
    w0 = jnp.exp(lf + m0 - m_new)
    return m_new, wgt, w0


def _sample_mixer_kernel(layer, hs_ref, c_ref, n_ref, m_ref, conv_ref, ckt_ref, cvt_ref,
                         gbias_ref, mng_ref, convw_ref, convb_ref, clng_ref, clnb_ref, sink_ref,
                         merged_ref, nout_ref, mout_ref, convout_ref, kout_ref, vout_ref):
    lrow = slice(layer, layer + 1)
    m_new, wgt, w0 = _sample_gates(hs_ref, gbias_ref[lrow, :], m_ref)
    einv = jnp.exp(-m_new)
    mout_ref[...] = m_new
    for h in range(M_HEADS):
        cs = slice(h * M_DK, (h + 1) * M_DK)
        q = hs_ref[:, cs]
        k = hs_ref[:, M_WIDTH + h * M_DK:M_WIDTH + (h + 1) * M_DK] * K_SCALE
        v = hs_ref[:, 2 * M_WIDTH + h * M_DK:2 * M_WIDTH + (h + 1) * M_DK]
        q_t = q.T
        wh = wgt[:, h:h + 1]
        w0h = w0[:, h:h + 1]
        n0 = n_ref[:, h, :]
        s = jnp.sum(q * k, axis=1, keepdims=True) * wh
        qc = jnp.concatenate(
            [jnp.sum(q_t[:, b:b + 1] * c_ref[b, h], axis=0, keepdims=True) for b in range(SB)],
            axis=0)
        num = s * v + w0h * qc
        den = s + w0h * jnp.sum(q * n0, axis=1, keepdims=True)
        hh = num / jnp.maximum(jnp.abs(den), einv[:, h:h + 1])
        nout_ref[:, h, :] = w0h * n0 + wh * k
        mu = jnp.mean(hh, axis=1, keepdims=True)
        hc = hh - mu
        var = jnp.mean(hc * hc, axis=1, keepdims=True)
        hn = hc * lax.rsqrt(var + LN_EPS) * mng_ref[lrow, cs]
        o = hs_ref[:, 3 * M_WIDTH + h * M_DK:3 * M_WIDTH + (h + 1) * M_DK]
        z = hs_ref[:, 4 * M_WIDTH + h * M_DK:4 * M_WIDTH + (h + 1) * M_DK]
        merged_ref[:, cs] = hn * _sigmoid(o) * _silu(z)

    cu = hs_ref[:, HM_W:HM_W + C_WIDTH]
    cg = hs_ref[:, HM_W + C_WIDTH:HM_W + 2 * C_WIDTH]
    cz = hs_ref[:, HM_W + 2 * C_WIDTH:HM_W + 3 * C_WIDTH]
    a = cu * _sigmoid(cg)
    yc = a * convw_ref[layer, CONV_BUF:CONV_K, :] + convb_ref[lrow, :]
    for t in range(CONV_BUF):
        yc = yc + conv_ref[t] * convw_ref[layer, t:t + 1, :]
    for t in range(CONV_BUF - 1):
        convout_ref[t] = conv_ref[t + 1]
    convout_ref[CONV_BUF - 1] = a
    yc = _layer_norm(yc, clng_ref[lrow, :], clnb_ref[lrow, :])
    merged_ref[:, M_WIDTH:M_WIDTH + C_WIDTH] = _silu(yc) * _silu(cz)

    ab = HM_W + HC_W
    lane = lax.broadcasted_iota(jnp.int32, (1, LANES), 1)
    head_lane = lax.broadcasted_iota(jnp.int32, (A_HEADS, LANES), 1)
    head_row = lax.broadcasted_iota(jnp.int32, (A_HEADS, LANES), 0)
    wcol = lax.broadcasted_iota(jnp.int32, (KV_WIDTH, WINDOW), 1)
    sink_col = jnp.sum(jnp.where(head_lane == head_row, sink_ref[lrow, :], 0.0),
                       axis=1, keepdims=True)
    knew_t = hs_ref[:, ab + HA_K:ab + HA_K + KV_WIDTH].T
    vnew_t = hs_ref[:, ab + HA_V:ab + HA_V + KV_WIDTH].T
    for b in range(SB):
        knew = hs_ref[b:b + 1, ab + HA_K:ab + HA_K + KV_WIDTH]
        vnew = hs_ref[b:b + 1, ab + HA_V:ab + HA_V + KV_WIDTH]
        ckt = ckt_ref[b]
        cvt = cvt_ref[b]
        qrows = []
        for hq in range(A_HEADS):
            kvh = hq // GROUP
            t, u = hq // 2, hq % 2
            tile = hs_ref[b:b + 1, ab + t * LANES:ab + (t + 1) * LANES]
            if u != kvh:
                tile = pltpu.roll(tile, HEAD_DIM, axis=1)
            qrows.append(jnp.where((lane // HEAD_DIM) == kvh, tile, 0.0))
        qsel = jnp.concatenate(qrows, axis=0)
        sc = _dot(qsel.astype(BF16), ckt.astype(BF16)) * A_SCALE
        sc_new = jnp.sum(qsel * knew, axis=1, keepdims=True) * A_SCALE
        smax = jnp.maximum(jnp.maximum(jnp.max(sc, axis=1, keepdims=True), sc_new), sink_col)
        e = jnp.exp(sc - smax)
        e_new = jnp.exp(sc_new - smax)
        denom = jnp.sum(e, axis=1, keepdims=True) + e_new + jnp.exp(sink_col - smax)
        ob = (_dot_nt((e / denom).astype(BF16), cvt.astype(BF16))
              + (e_new / denom) * vnew)
        tiles = []
        for t in range(A_HEADS // 2):
            halves = []
            for u in range(2):
                hq = 2 * t + u
                kvh = hq // GROUP
                x = ob[hq:hq + 1, :]
                if u != kvh:
                    x = pltpu.roll(x, HEAD_DIM, axis=1)
                halves.append(x)
            tiles.append(jnp.where(lane < HEAD_DIM, halves[0], halves[1]))
        ao = jnp.concatenate(tiles, axis=1)
        az = hs_ref[b:b + 1, ab + HA_Z:ab + HA_Z + A_WIDTH]
        merged_ref[b:b + 1, M_WIDTH + C_WIDTH:D_MODEL] = ao * _silu(az)
        kout_ref[b] = jnp.where(wcol == WINDOW - 1, knew_t[:, b:b + 1],
                                pltpu.roll(ckt, WINDOW - 1, axis=1))
        vout_ref[b] = jnp.where(wcol == WINDOW - 1, vnew_t[:, b:b + 1],
                                pltpu.roll(cvt, WINDOW - 1, axis=1))


def _sample_mixer(layer, hs, c, n, m, conv, ckt, cvt, prm):
    nb = hs.shape[0]
    row2 = lambda i: (i, 0)

    def state_specs(lead, pre):
        return [
            pl.BlockSpec(lead + (SB, M_HEADS, M_DK), lambda i: pre + (i, 0, 0)),
            pl.BlockSpec(lead + (SB, LANES), lambda i: pre + (i, 0)),
            pl.BlockSpec(lead + (CONV_BUF, SB, C_WIDTH), lambda i: pre + (0, i, 0)),
            pl.BlockSpec(lead + (SB, KV_WIDTH, WINDOW), lambda i: pre + (i, 0, 0)),
            pl.BlockSpec(lead + (SB, KV_WIDTH, WINDOW), lambda i: pre + (i, 0, 0)),
        ]

    c_spec = pl.BlockSpec((None, SB, M_HEADS, M_DK, M_DK), lambda i: (layer, i, 0, 0, 0))
    state_shapes = [
        (nb, M_HEADS, M_DK), (nb, LANES),
        (CONV_BUF, nb, C_WIDTH), (nb, KV_WIDTH, WINDOW), (nb, KV_WIDTH, WINDOW)]
    params = [prm["gbias"], prm["mng"], prm["convw"], prm["convb"], prm["clng"], prm["clnb"],
              prm["sink_row"]]
    return pl.pallas_call(
        functools.partial(_sample_mixer_kernel, layer),
        grid=(nb // SB,),
        in_specs=([pl.BlockSpec((SB, PROJ_W), row2), c_spec] + state_specs((None,), (layer,))
                  + [_whole(a) for a in params]),
        out_specs=[pl.BlockSpec((SB, D_MODEL), row2)] + state_specs((), ()),
        out_shape=[jax.ShapeDtypeStruct((nb, D_MODEL), F32)]
        + [jax.ShapeDtypeStruct(s, F32) for s in state_shapes],
        compiler_params=pltpu.CompilerParams(
            dimension_semantics=("arbitrary",), vmem_limit_bytes=VMEM_LIMIT),
        name="mixer_sample",
    )(hs, c, n, m, conv, ckt, cvt, *params)


def _update_c_kernel(hs_ref, c_ref, m_ref, gbias_ref, cout_ref):
    layer = pl.program_id(0)
    _, wgt, w0 = _sample_gates(hs_ref, gbias_ref[pl.ds(layer, 1), :], m_ref)
    for h in range(M_HEADS):
        k_t = (hs_ref[:, M_WIDTH + h * M_DK:M_WIDTH + (h + 1) * M_DK] * K_SCALE).T
        v = hs_ref[:, 2 * M_WIDTH + h * M_DK:2 * M_WIDTH + (h + 1) * M_DK]
        gv = wgt[:, h:h + 1] * v
        w0h = w0[:, h:h + 1]
        for b in range(SB):
            cout_ref[b, h] = w0h[b:b + 1, :] * c_ref[b, h] + k_t[:, b:b + 1] * gv[b:b + 1, :]


def _update_c(hs_all, c, m, prm):
    depth, nb = hs_all.shape[:2]
    per = lambda l, i: (l, i, 0)
    c_spec = pl.BlockSpec((None, SB, M_HEADS, M_DK, M_DK), lambda l, i: (l, i, 0, 0, 0))
    return pl.pallas_call(
        _update_c_kernel,
        grid=(depth, nb // SB),
        in_specs=[pl.BlockSpec((None, SB, PROJ_W), per), c_spec,
                  pl.BlockSpec((None, SB, LANES), per), _whole(prm["gbias"])],
        out_specs=c_spec,
        out_shape=jax.ShapeDtypeStruct(c.shape, F32),
        compiler_params=pltpu.CompilerParams(
            dimension_semantics=("arbitrary", "arbitrary"), vmem_limit_bytes=VMEM_LIMIT),
        name="update_matrix_memory",
    )(hs_all, c, m, prm["gbias"])


def _outproj_sample_kernel(layer, mg_ref, x_ref, wout_ref, lng_ref, lnb_ref, y_ref):
    lrow = slice(layer, layer + 1)
    y = _dot(mg_ref[...].astype(BF16), wout_ref[...])
    y_ref[...] = _layer_norm(ALPHA * x_ref[...] + y, lng_ref[lrow, :], lnb_ref[lrow, :])


def _outproj_sample(mg, xs, wout, prm, layer):
    nb = xs.shape[0]
    full = lambda shape: pl.BlockSpec(shape, lambda i: (0, 0))
    return pl.pallas_call(
        functools.partial(_outproj_sample_kernel, layer),
        grid=(1,),
        in_specs=[full((nb, D_MODEL)), full((nb, D_MODEL)),
                  pl.BlockSpec((None, D_MODEL, D_MODEL), lambda i: (layer, 0, 0)),
                  _whole(prm["lng"]), _whole(prm["lnb"])],
        out_specs=full((nb, D_MODEL)),
        out_shape=jax.ShapeDtypeStruct((nb, D_MODEL), F32),
        compiler_params=pltpu.CompilerParams(
            dimension_semantics=("arbitrary",), vmem_limit_bytes=VMEM_LIMIT),
        name="outproj_sample",
    )(mg, xs, wout, prm["lng"], prm["lnb"])


def _cast_kernel(w_ref, o_ref):
    o_ref[...] = w_ref[...].astype(BF16)


def _cast_w_out(w_out):
    depth, d, n = w_out.shape
    rows = 512
    return pl.pallas_call(
        _cast_kernel,
        grid=(depth, d // rows),
        in_specs=[pl.BlockSpec((None, rows, n), lambda l, i: (l, i, 0))],
        out_specs=pl.BlockSpec((None, rows, n), lambda l, i: (l, i, 0)),
        out_shape=jax.ShapeDtypeStruct((depth, d, n), BF16),
        compiler_params=pltpu.CompilerParams(dimension_semantics=("arbitrary", "arbitrary")),
        name="cast_w_out",
    )(w_out)


def _lane_rows(*mats):
    v = jnp.concatenate([a.astype(F32) for a in mats], axis=1)
    return jnp.pad(v, ((0, 0), (0, LANES - v.shape[1])))


def _cache_rows_by_window(cache):
    depth, nb = cache.shape[:2]
    return jnp.transpose(cache, (0, 1, 3, 4, 2)).reshape(depth, nb, KV_WIDTH, WINDOW)


def _cache_window_major(cache_t):
    depth, nb = cache_t.shape[:2]
    return jnp.transpose(cache_t.reshape(depth, nb, KV_HEADS, HEAD_DIM, WINDOW), (0, 1, 4, 2, 3))


def kernel(x_prompt, x_sample, state_C, state_n, state_m, state_conv, cache_k, cache_v, w_in, w_out,
           b_igate, b_fgate, m_norm_g, conv_w, conv_b, conv_ln_g, conv_ln_b, sinks, ln_g, ln_b):
    batch, seq, _ = x_prompt.shape
    nb = x_sample.shape[0]
    xp = x_prompt.reshape(batch * seq, D_MODEL)
    xs = x_sample.reshape(nb, D_MODEL)
    wt = jnp.swapaxes(w_in, 1, 2)
    wout = _cast_w_out(w_out)
    prm = {
        "gbias": _lane_rows(b_igate, b_fgate),
        "sink_row": _lane_rows(sinks),
        "mng": m_norm_g, "convw": conv_w, "convb": conv_b, "clng": conv_ln_g, "clnb": conv_ln_b,
        "lng": ln_g, "lnb": ln_b,
    }
    m_in = jnp.pad(state_m, ((0, 0), (0, 0), (0, LANES - M_HEADS)))
    sconv = jnp.swapaxes(state_conv, 1, 2)
    ckt = _cache_rows_by_window(cache_k)
    cvt = _cache_rows_by_window(cache_v)
    st_p = []
    st_s = []
    hs_all = []
    for l in range(DEPTH):
        pm, pc, pa, g, conv_p, hs = _inproj(xp, xs, wt, prm, l, batch)
        xp, c_p, n_p, m_p, k_p, v_p = _mixer(l, sinks, pm, pc, pa, g, xp, wout, prm, batch, seq)
        st_p.append((c_p, n_p, m_p, conv_p, k_p, v_p))
        mg, *states = _sample_mixer(l, hs, state_C, state_n, m_in, sconv, ckt, cvt, prm)
        st_s.append(states)
        hs_all.append(hs)
        xs = _outproj_sample(mg, xs, wout, prm, l)
    c_s = _update_c(jnp.stack(hs_all, axis=0), state_C, m_in, prm)

    def stacked(per_layer):
        return [jnp.stack(a, axis=0) for a in zip(*per_layer)]

    c_p, n_p, m_p, conv_p, k_p, v_p = stacked(st_p)
    n_s, m_s, conv_s, k_s, v_s = stacked(st_s)
    return (xp.reshape(batch, seq, D_MODEL), xs.reshape(nb, 1, D_MODEL),
            c_p, n_p, m_p.reshape(DEPTH, batch, LANES)[:, :, :M_HEADS], conv_p,
            k_p.reshape(DEPTH, batch, WINDOW, KV_HEADS, HEAD_DIM),
            v_p.reshape(DEPTH, batch, WINDOW, KV_HEADS, HEAD_DIM),
            c_s, n_s, m_s[:, :, :M_HEADS], jnp.swapaxes(conv_s, 1, 2),
            _cache_window_major(k_s), _cache_window_major(v_s))
```

```python
import functools

import jax
import jax.numpy as jnp
from jax import lax
from jax.experimental import pallas as pl
from jax.experimental.pallas import tpu as pltpu

F32 = jnp.float32
BF16 = jnp.bfloat16

D_MODEL = 2048
DEPTH = 2
M_WIDTH = 1024
M_HEADS = 4
M_DK = 256
C_WIDTH = 512
CONV_K = 31
CONV_BUF = CONV_K - 1
A_WIDTH = 512
HEAD_DIM = 64
A_HEADS = 8
KV_HEADS = 2
GROUP = A_HEADS // KV_HEADS
KV_WIDTH = KV_HEADS * HEAD_DIM
WINDOW = 128
ALPHA = (2 * DEPTH) ** 0.25
LN_EPS = 1e-5
K_SCALE = M_DK ** -0.5
A_SCALE = HEAD_DIM ** -0.5

LANES = 128
SUBLANES = 8

HM_W = 5 * M_WIDTH
HC_W = 3 * C_WIDTH
HA_W = 2 * A_WIDTH + 2 * KV_WIDTH
MAIN_W = HM_W + HC_W + HA_W
GATE_END = MAIN_W + LANES
PROJ_W = GATE_END
GATE_OFF = 4 * M_WIDTH
N_GATES = 2 * M_HEADS
W_CHUNK = 256
W_SLOTS = 4
PLAIN_CHUNKS = GATE_OFF // W_CHUNK
MAIN_CHUNKS = MAIN_W // W_CHUNK
HA_K = A_WIDTH
HA_V = A_WIDTH + KV_WIDTH
HA_Z = A_WIDTH + 2 * KV_WIDTH

TM_PROJ = 256
TB = 256
A_BLK = 128
CONV_HALO = 32
SB = 8
VMEM_LIMIT = 56 * 1024 * 1024
INPROJ_VMEM_LIMIT = 60 * 1024 * 1024


def _sigmoid(x):
    return 1.0 / (1.0 + jnp.exp(-x))


def _silu(x):
    return x * _sigmoid(x)


def _log_sigmoid(x):
    return jnp.minimum(x, 0.0) - jnp.log1p(jnp.exp(-jnp.abs(x)))


def _layer_norm(x, g, b):
    mu = jnp.mean(x, axis=-1, keepdims=True)
    xc = x - mu
    var = jnp.mean(xc * xc, axis=-1, keepdims=True)
    return xc * lax.rsqrt(var + LN_EPS) * g + b


def _dot(a, b):
    return jnp.dot(a, b, preferred_element_type=F32)


def _dot_nt(a, b):
    return lax.dot_general(a, b, (((1,), (1,)), ((), ())), preferred_element_type=F32)


def _dot_tn(a, b):
    return lax.dot_general(a, b, (((0,), (0,)), ((), ())), preferred_element_type=F32)


def _whole(arr):
    zeros = (0,) * arr.ndim
    return pl.BlockSpec(arr.shape, lambda *_: zeros)


PM_W = 4 * M_WIDTH
PA_QKV = A_WIDTH + 2 * KV_WIDTH
QKV_CHUNK = 512
SHIFTED_ROWS = CONV_HALO + TM_PROJ - SUBLANES


def _stage_rows(src_hbm, dst_ref, stage, sem, n_chunks, src_row):
    def chunk_copy(c, slot):
        return pltpu.make_async_copy(
            src_hbm.at[pl.ds(pl.multiple_of(src_row(c), SUBLANES), W_CHUNK), :],
            stage.at[slot], sem.at[slot])

    for c in range(W_SLOTS - 1):
        chunk_copy(c, c).start()

    def body(c, carry):
        slot = c % W_SLOTS
        chunk_copy(c, slot).wait()
        ahead = c + W_SLOTS - 1

        @pl.when(ahead < n_chunks)
        def _():
            chunk_copy(ahead, ahead % W_SLOTS).start()

        dst_ref[pl.ds(pl.multiple_of(c * W_CHUNK, W_CHUNK), W_CHUNK), :] = stage[slot].astype(BF16)
        return carry

    lax.fori_loop(0, n_chunks, body, 0)


def _stage_w_in(w_hbm, wt_ref, stage, sem):
    _stage_rows(w_hbm, wt_ref, stage, sem, MAIN_CHUNKS,
                lambda c: jnp.where(c < PLAIN_CHUNKS, c * W_CHUNK, c * W_CHUNK + N_GATES))
    stage[0, 0:LANES, :] = jnp.zeros((LANES, D_MODEL), F32)
    gate_copy = pltpu.make_async_copy(
        w_hbm.at[GATE_OFF:GATE_OFF + N_GATES, :], stage.at[0, 0:N_GATES, :], sem.at[0])
    gate_copy.start()
    gate_copy.wait()
    wt_ref[MAIN_W:GATE_END, :] = stage[0, 0:LANES, :].astype(BF16)


def _inproj_kernel(layer, blocks_per_seq, x_ref, xs_ref, w_hbm, convw_ref, convb_ref, clng_ref,
                   clnb_ref, pm_ref, pc_ref, pa_ref, g_ref, conv_ref, hs_ref,
                   wt_ref, stage, sem, xbuf, abuf, sbuf, ybuf, zbuf):
    i = pl.program_id(0)
    lrow = slice(layer, layer + 1)

    @pl.when(i == 0)
    def _():
        _stage_w_in(w_hbm.at[layer], wt_ref, stage, sem)
        hs_ref[...] = _dot_nt(xs_ref[...].astype(BF16), wt_ref[...])

    @pl.when(i % blocks_per_seq == 0)
    def _():
        abuf[0:CONV_HALO, :] = jnp.zeros((CONV_HALO, C_WIDTH), F32)

    xbuf[...] = x_ref[...].astype(BF16)

    def proj(c0, c1):
        return _dot_nt(xbuf[...], wt_ref[c0:c1, :])

    for c in range(0, 3 * M_WIDTH, QKV_CHUNK):
        pm_ref[:, c:c + QKV_CHUNK] = proj(c, c + QKV_CHUNK).astype(BF16)
    for h in range(M_HEADS):
        o = proj(3 * M_WIDTH + h * M_DK, 3 * M_WIDTH + (h + 1) * M_DK)
        z = proj(4 * M_WIDTH + h * M_DK, 4 * M_WIDTH + (h + 1) * M_DK)
        pm_ref[:, 3 * M_WIDTH + h * M_DK:3 * M_WIDTH + (h + 1) * M_DK] = (
            _sigmoid(o) * _silu(z)).astype(BF16)

    cu = proj(HM_W, HM_W + C_WIDTH)
    cg = proj(HM_W + C_WIDTH, HM_W + 2 * C_WIDTH)
    abuf[CONV_HALO:CONV_HALO + TM_PROJ, :] = cu * _sigmoid(cg)
    zbuf[...] = _silu(proj(HM_W + 2 * C_WIDTH, HM_W + 3 * C_WIDTH))
    for p in range(1, SUBLANES):
        sbuf[p - 1] = abuf[p:p + SHIFTED_ROWS, :]
    tap0 = CONV_HALO - CONV_BUF
    for r in range(0, TM_PROJ, 128):
        for c in range(0, C_WIDTH, LANES):
            acc = jnp.zeros((128, LANES), F32)
            for t in range(CONV_K):
                p = (tap0 + t) % SUBLANES
                base = r + tap0 + t - p
                src = abuf if p == 0 else sbuf.at[p - 1]
                acc = acc + src[base:base + 128, c:c + LANES] * convw_ref[layer, t:t + 1, c:c + LANES]
            ybuf[r:r + 128, c:c + LANES] = acc + convb_ref[lrow, c:c + LANES]
    conv_ref[0] = abuf[CONV_HALO + TM_PROJ - CONV_BUF:CONV_HALO + TM_PROJ, :]
    abuf[0:CONV_HALO, :] = abuf[TM_PROJ:TM_PROJ + CONV_HALO, :]
    yc = _layer_norm(ybuf[...], clng_ref[lrow, :], clnb_ref[lrow, :])
    pc_ref[...] = (_silu(yc) * zbuf[...]).astype(BF16)

    base = HM_W + HC_W
    pa_ref[:, 0:PA_QKV] = proj(base, base + PA_QKV).astype(BF16)
    pa_ref[:, PA_QKV:HA_W] = _silu(proj(base + PA_QKV, base + HA_W)).astype(BF16)
    g_ref[...] = proj(MAIN_W, GATE_END)


def _inproj(x2d, xs, w_in_t, prm, layer, batch):
    m = x2d.shape[0]
    ns = xs.shape[0]
    nblk = m // TM_PROJ
    row = lambda i: (i, 0)
    once = lambda i: (0, 0)
    params = [prm["convw"], prm["convb"], prm["clng"], prm["clnb"]]
    return pl.pallas_call(
        functools.partial(_inproj_kernel, layer, nblk // batch),
        grid=(nblk,),
        in_specs=[
            pl.BlockSpec((TM_PROJ, D_MODEL), row),
            pl.BlockSpec((ns, D_MODEL), once),
            pl.BlockSpec(memory_space=pl.ANY),
        ] + [_whole(a) for a in params],
        out_specs=[
            pl.BlockSpec((TM_PROJ, PM_W), row),
            pl.BlockSpec((TM_PROJ, C_WIDTH), row),
            pl.BlockSpec((TM_PROJ, HA_W), row),
            pl.BlockSpec((TM_PROJ, LANES), row),
            pl.BlockSpec((1, CONV_BUF, C_WIDTH), lambda i: (i // (nblk // batch), 0, 0)),
            pl.BlockSpec((ns, PROJ_W), once),
        ],
        out_shape=[
            jax.ShapeDtypeStruct((m, PM_W), BF16),
            jax.ShapeDtypeStruct((m, C_WIDTH), BF16),
            jax.ShapeDtypeStruct((m, HA_W), BF16),
            jax.ShapeDtypeStruct((m, LANES), F32),
            jax.ShapeDtypeStruct((batch, CONV_BUF, C_WIDTH), F32),
            jax.ShapeDtypeStruct((ns, PROJ_W), F32),
        ],
        scratch_shapes=[
            pltpu.VMEM((PROJ_W, D_MODEL), BF16),
            pltpu.VMEM((W_SLOTS, W_CHUNK, D_MODEL), F32),
            pltpu.SemaphoreType.DMA((W_SLOTS,)),
            pltpu.VMEM((TM_PROJ, D_MODEL), BF16),
            pltpu.VMEM((CONV_HALO + TM_PROJ, C_WIDTH), F32),
            pltpu.VMEM((SUBLANES - 1, SHIFTED_ROWS, C_WIDTH), F32),
            pltpu.VMEM((TM_PROJ, C_WIDTH), F32),
            pltpu.VMEM((TM_PROJ, C_WIDTH), F32),
        ],
        compiler_params=pltpu.CompilerParams(
            dimension_semantics=("arbitrary",), vmem_limit_bytes=INPROJ_VMEM_LIMIT),
        name="inproj_prompt",
    )(x2d, xs, w_in_t, *params)


def _cumsum_rows(x):
    n = x.shape[0]
    row = lax.broadcasted_iota(jnp.int32, x.shape, 0)
    s = 1
    while s < n:
        x = x + jnp.where(row >= s, pltpu.roll(x, s, axis=0), 0.0)
        s *= 2
    return x


def _mixer_kernel(layer, sinks_ref, hm_ref, pc_ref, ha_ref, g_ref, x_ref, wout_ref,
                  gbias_ref, mng_ref, lng_ref, lnb_ref,
                  y_ref, c_ref, n_ref, m_ref, k_ref, v_ref,
                  kprev, vprev, merged):
    j = pl.program_id(1)
    lrow = slice(layer, layer + 1)

    @pl.when(j == 0)
    def _():
        c_ref[...] = jnp.zeros_like(c_ref)
        n_ref[...] = jnp.zeros_like(n_ref)
        m_ref[...] = jnp.zeros_like(m_ref)
        kprev[...] = jnp.zeros_like(kprev)
        vprev[...] = jnp.zeros_like(vprev)

    gb = g_ref[...] + gbias_ref[lrow, :]
    bsum = _cumsum_rows(_log_sigmoid(gb))
    bsum = pltpu.roll(bsum, LANES - M_HEADS, axis=1)
    a_all = gb - bsum
    a_t = a_all.T
    m_prev = m_ref[0]
    row_i = lax.broadcasted_iota(jnp.int32, (TB, TB), 0)
    col_i = lax.broadcasted_iota(jnp.int32, (TB, TB), 1)
    causal = col_i <= row_i
    lane_i = lax.broadcasted_iota(jnp.int32, (1, LANES), 1)
    m_new = m_prev

    for h in range(M_HEADS):
        cs = slice(h * M_DK, (h + 1) * M_DK)
        q = hm_ref[:, cs]
        k = hm_ref[:, M_WIDTH + h * M_DK:M_WIDTH + (h + 1) * M_DK]
        v = hm_ref[:, 2 * M_WIDTH + h * M_DK:2 * M_WIDTH + (h + 1) * M_DK]
        logit = jnp.where(causal, a_t[h:h + 1, :], -jnp.inf)
        m0 = m_prev[:, h:h + 1]
        mx = jnp.maximum(jnp.max(logit, axis=1, keepdims=True), m0)
        w = jnp.exp(logit - mx)
        w0 = jnp.exp(m0 - mx)
        s = _dot_nt(q, k) * (w * K_SCALE)
        c0 = c_ref[0, h]
        n0 = n_ref[0, h:h + 1, :]
        num = _dot(s.astype(BF16), v) + w0 * _dot(q, c0.astype(BF16))
        den = (jnp.sum(s, axis=1, keepdims=True)
               + w0 * jnp.sum(q.astype(F32) * n0, axis=1, keepdims=True))
        m_t = bsum[:, h:h + 1] + mx
        hh = num / jnp.maximum(jnp.abs(den), jnp.exp(-m_t))
        mu = jnp.mean(hh, axis=1, keepdims=True)
        hc = hh - mu
        var = jnp.mean(hc * hc, axis=1, keepdims=True)
        hn = hc * lax.rsqrt(var + LN_EPS) * mng_ref[lrow, cs]
        gate = hm_ref[:, 3 * M_WIDTH + h * M_DK:3 * M_WIDTH + (h + 1) * M_DK].astype(F32)
        merged[:, cs] = (hn * gate).astype(BF16)
        mx_end = mx[TB - 1:TB, :]
        g_col = jnp.exp(a_all[:, h:h + 1] - mx_end) * K_SCALE
        g0 = jnp.exp(m0 - mx_end)
        kg = k.astype(F32) * g_col
        c_ref[0, h] = g0 * c0 + _dot_tn(kg.astype(BF16), v)
        n_ref[0, h:h + 1, :] = g0 * n0 + jnp.sum(kg, axis=0, keepdims=True)
        m_end = bsum[TB - 1:TB, h:h + 1] + mx_end
        m_new = jnp.where(lane_i == h, m_end, m_new)
    m_ref[0] = m_new

    merged[:, M_WIDTH:M_WIDTH + C_WIDTH] = pc_ref[...]

    arow = lax.broadcasted_iota(jnp.int32, (A_BLK, 2 * A_BLK), 0)
    acol = lax.broadcasted_iota(jnp.int32, (A_BLK, 2 * A_BLK), 1)
    rel = A_BLK + arow - acol
    band = (rel >= 0) & (rel <= WINDOW)
    for sub in range(TB // A_BLK):
        r0 = sub * A_BLK
        if sub == 0:
            kcat = jnp.concatenate([kprev[...], ha_ref[0:A_BLK, HA_K:HA_K + KV_WIDTH]], axis=0)
            vcat = jnp.concatenate([vprev[...], ha_ref[0:A_BLK, HA_V:HA_V + KV_WIDTH]], axis=0)
            mask = band & (acol >= jnp.where(j > 0, 0, A_BLK))
        else:
            kcat = ha_ref[r0 - A_BLK:r0 + A_BLK, HA_K:HA_K + KV_WIDTH]
            vcat = ha_ref[r0 - A_BLK:r0 + A_BLK, HA_V:HA_V + KV_WIDTH]
            mask = band
        ones = jnp.ones((2 * A_BLK, HEAD_DIM), BF16)
        vext = [jnp.concatenate([vcat[:, kvh * HEAD_DIM:(kvh + 1) * HEAD_DIM], ones], axis=1)
                for kvh in range(KV_HEADS)]
        for hp in range(A_HEADS // 2):
            outs = []
            for hq in (2 * hp, 2 * hp + 1):
                kvh = hq // GROUP
                qh = ha_ref[r0:r0 + A_BLK, hq * HEAD_DIM:(hq + 1) * HEAD_DIM]
                kh = kcat[:, kvh * HEAD_DIM:(kvh + 1) * HEAD_DIM]
                sc = jnp.where(mask, _dot_nt(qh, kh) * A_SCALE, -jnp.inf)
                sink = sinks_ref[layer, hq]
                smax = jnp.maximum(jnp.max(sc, axis=1, keepdims=True), sink)
                pv = _dot(jnp.exp(sc - smax).astype(BF16), vext[kvh])
                denom = pv[:, HEAD_DIM:HEAD_DIM + 1] + jnp.exp(sink - smax)
                outs.append(pv[:, 0:HEAD_DIM] / denom)
            ao = jnp.concatenate(outs, axis=1)
            gate = ha_ref[r0:r0 + A_BLK, HA_Z + hp * LANES:HA_Z + (hp + 1) * LANES].astype(F32)
            col = M_WIDTH + C_WIDTH + hp * LANES
            merged[r0:r0 + A_BLK, col:col + LANES] = (ao * gate).astype(BF16)
    kprev[...] = ha_ref[TB - A_BLK:TB, HA_K:HA_K + KV_WIDTH]
    vprev[...] = ha_ref[TB - A_BLK:TB, HA_V:HA_V + KV_WIDTH]
    k_ref[0] = ha_ref[TB - WINDOW:TB, HA_K:HA_K + KV_WIDTH].astype(F32)
    v_ref[0] = ha_ref[TB - WINDOW:TB, HA_V:HA_V + KV_WIDTH].astype(F32)

    y = _dot(merged[...], wout_ref[...])
    y_ref[...] = _layer_norm(ALPHA * x_ref[...] + y, lng_ref[lrow, :], lnb_ref[lrow, :])


def _mixer(layer, sinks, pm, pc, pa, g, x2d, wout, prm, batch, seq):
    nb = seq // TB
    row = lambda b, j: (b * nb + j, 0)
    per_b3 = lambda b, j: (b, 0, 0)
    state_shapes = [
        (batch, M_HEADS, M_DK, M_DK), (batch, M_HEADS, M_DK), (batch, 1, LANES),
        (batch, WINDOW, KV_WIDTH), (batch, WINDOW, KV_WIDTH)]
    params = [prm["gbias"], prm["mng"], prm["lng"], prm["lnb"]]
    return pl.pallas_call(
        functools.partial(_mixer_kernel, layer),
        grid=(batch, nb),
        in_specs=[
            pl.BlockSpec(memory_space=pltpu.SMEM),
            pl.BlockSpec((TB, PM_W), row),
            pl.BlockSpec((TB, C_WIDTH), row),
            pl.BlockSpec((TB, HA_W), row),
            pl.BlockSpec((TB, LANES), row),
            pl.BlockSpec((TB, D_MODEL), row),
            pl.BlockSpec((None, D_MODEL, D_MODEL), lambda b, j: (layer, 0, 0)),
        ] + [_whole(a) for a in params],
        out_specs=[
            pl.BlockSpec((TB, D_MODEL), row),
            pl.BlockSpec((1, M_HEADS, M_DK, M_DK), lambda b, j: (b, 0, 0, 0)),
            pl.BlockSpec((1, M_HEADS, M_DK), per_b3),
            pl.BlockSpec((1, 1, LANES), per_b3),
            pl.BlockSpec((1, WINDOW, KV_WIDTH), per_b3),
            pl.BlockSpec((1, WINDOW, KV_WIDTH), per_b3),
        ],
        out_shape=[jax.ShapeDtypeStruct((batch * seq, D_MODEL), F32)]
        + [jax.ShapeDtypeStruct(s, F32) for s in state_shapes],
        scratch_shapes=[
            pltpu.VMEM((A_BLK, KV_WIDTH), BF16),
            pltpu.VMEM((A_BLK, KV_WIDTH), BF16),
            pltpu.VMEM((TB, D_MODEL), BF16),
        ],
        compiler_params=pltpu.CompilerParams(
            dimension_semantics=("arbitrary", "arbitrary"), vmem_limit_bytes=VMEM_LIMIT),
        name="mixer_prompt",
    )(sinks, pm, pc, pa, g, x2d, wout, *params)


def _sample_gates(hs_ref, gbias_row, m_ref):
    gb = hs_ref[:, MAIN_W:GATE_END] + gbias_row
    lf = pltpu.roll(_log_sigmoid(gb), LANES - M_HEADS, axis=1)
    m0 = m_ref[...]
    m_new = jnp.maximum(lf + m0, gb)
    wgt = jnp.exp(gb - m_new)
    w0 = jnp.exp(lf + m0 - m_new)
    return m_new, wgt, w0


def _sample_mixer_kernel(layer, hs_ref, c_ref, n_ref, m_ref, conv_ref, ckt_ref, cvt_ref,
                         gbias_ref, mng_ref, convw_ref, convb_ref, clng_ref, clnb_ref, sink_ref,
                         merged_ref, nout_ref, mout_ref, convout_ref, kout_ref, vout_ref):
    lrow = slice(layer, layer + 1)
    m_new, wgt, w0 = _sample_gates(hs_ref, gbias_ref[lrow, :], m_ref)
    einv = jnp.exp(-m_new)
    mout_ref[...] = m_new
    for h in range(M_HEADS):
        cs = slice(h * M_DK, (h + 1) * M_DK)
        q = hs_ref[:, cs]
        k = hs_ref[:, M_WIDTH + h * M_DK:M_WIDTH + (h + 1) * M_DK] * K_SCALE
        v = hs_ref[:, 2 * M_WIDTH + h * M_DK:2 * M_WIDTH + (h + 1) * M_DK]
        q_t = q.T
        wh = wgt[:, h:h + 1]
        w0h = w0[:, h:h + 1]
        n0 = n_ref[:, h, :]
        s = jnp.sum(q * k, axis=1, keepdims=True) * wh
        qc = jnp.concatenate(
            [jnp.sum(q_t[:, b:b + 1] * c_ref[b, h], axis=0, keepdims=True) for b in range(SB)],
            axis=0)
        num = s * v + w0h * qc
        den = s + w0h * jnp.sum(q * n0, axis=1, keepdims=True)
        hh = num / jnp.maximum(jnp.abs(den), einv[:, h:h + 1])
        nout_ref[:, h, :] = w0h * n0 + wh * k
        mu = jnp.mean(hh, axis=1, keepdims=True)
        hc = hh - mu
        var = jnp.mean(hc * hc, axis=1, keepdims=True)
        hn = hc * lax.rsqrt(var + LN_EPS) * mng_ref[lrow, cs]
        o = hs_ref[:, 3 * M_WIDTH + h * M_DK:3 * M_WIDTH + (h + 1) * M_DK]
        z = hs_ref[:, 4 * M_WIDTH + h * M_DK:4 * M_WIDTH + (h + 1) * M_DK]
        merged_ref[:, cs] = hn * _sigmoid(o) * _silu(z)

    cu = hs_ref[:, HM_W:HM_W + C_WIDTH]
    cg = hs_ref[:, HM_W + C_WIDTH:HM_W + 2 * C_WIDTH]
    cz = hs_ref[:, HM_W + 2 * C_WIDTH:HM_W + 3 * C_WIDTH]
    a = cu * _sigmoid(cg)
    yc = a * convw_ref[layer, CONV_BUF:CONV_K, :] + convb_ref[lrow, :]
    for t in range(CONV_BUF):
        yc = yc + conv_ref[t] * convw_ref[layer, t:t + 1, :]
    for t in range(CONV_BUF - 1):
        convout_ref[t] = conv_ref[t + 1]
    convout_ref[CONV_BUF - 1] = a
    yc = _layer_norm(yc, clng_ref[lrow, :], clnb_ref[lrow, :])
    merged_ref[:, M_WIDTH:M_WIDTH + C_WIDTH] = _silu(yc) * _silu(cz)

    ab = HM_W + HC_W
    lane = lax.broadcasted_iota(jnp.int32, (1, LANES), 1)
    rows_b = lax.broadcasted_iota(jnp.int32, (A_HEADS * SB, LANES), 0) % SB
    wcol = lax.broadcasted_iota(jnp.int32, (KV_WIDTH, WINDOW), 1)
    knew = hs_ref[:, ab + HA_K:ab + HA_K + KV_WIDTH]
    vnew = hs_ref[:, ab + HA_V:ab + HA_V + KV_WIDTH]
    q_parts, scn_parts, sink_parts = [], [], []
    for hq in range(A_HEADS):
        kvh = hq // GROUP
        t, u = hq // 2, hq % 2
        tile = hs_ref[:, ab + t * LANES:ab + (t + 1) * LANES]
        if u != kvh:
            tile = pltpu.roll(tile, HEAD_DIM, axis=1)
        qh = jnp.where((lane // HEAD_DIM) == kvh, tile, 0.0)
        q_parts.append(qh)
        scn_parts.append(jnp.sum(qh * knew, axis=1, keepdims=True))
        sink_parts.append(jnp.broadcast_to(sink_ref[lrow, hq:hq + 1], (SB, 1)))
    q_all = jnp.concatenate(q_parts, axis=0).astype(BF16)
    sc_new = jnp.concatenate(scn_parts, axis=0) * A_SCALE
    sink_all = jnp.concatenate(sink_parts, axis=0)
    sc = jnp.zeros((A_HEADS * SB, WINDOW), F32)
    for b in range(SB):
        sc = jnp.where(rows_b == b, _dot(q_all, ckt_ref[b].astype(BF16)), sc)
    sc = sc * A_SCALE
    smax = jnp.maximum(jnp.maximum(jnp.max(sc, axis=1, keepdims=True), sc_new), sink_all)
    e = jnp.exp(sc - smax)
    e_new = jnp.exp(sc_new - smax)
    denom = jnp.sum(e, axis=1, keepdims=True) + e_new + jnp.exp(sink_all - smax)
    p = (e / denom).astype(BF16)
    ob = (e_new / denom) * jnp.concatenate([vnew] * A_HEADS, axis=0)
    for b in range(SB):
        ob = ob + jnp.where(rows_b == b, _dot_nt(p, cvt_ref[b].astype(BF16)), 0.0)
    tiles = []
    for t in range(A_HEADS // 2):
        halves = []
        for u in range(2):
            hq = 2 * t + u
            x = ob[hq * SB:(hq + 1) * SB, :]
            if u != hq // GROUP:
                x = pltpu.roll(x, HEAD_DIM, axis=1)
            halves.append(x)
        tiles.append(jnp.where(lane < HEAD_DIM, halves[0], halves[1]))
    az = hs_ref[:, ab + HA_Z:ab + HA_Z + A_WIDTH]
    merged_ref[:, M_WIDTH + C_WIDTH:D_MODEL] = jnp.concatenate(tiles, axis=1) * _silu(az)
    knew_t = knew.T
    vnew_t = vnew.T
    for b in range(SB):
        kout_ref[b] = jnp.where(wcol == WINDOW - 1, knew_t[:, b:b + 1],
                                pltpu.roll(ckt_ref[b], WINDOW - 1, axis=1))
        vout_ref[b] = jnp.where(wcol == WINDOW - 1, vnew_t[:, b:b + 1],
                                pltpu.roll(cvt_ref[b], WINDOW - 1, axis=1))


def _sample_mixer(layer, hs, c, n, m, conv, ckt, cvt, prm):
    nb = hs.shape[0]
    row2 = lambda i: (i, 0)

    def state_specs(lead, pre):
        return [
            pl.BlockSpec(lead + (SB, M_HEADS, M_DK), lambda i: pre + (i, 0, 0)),
            pl.BlockSpec(lead + (SB, LANES), lambda i: pre + (i, 0)),
            pl.BlockSpec(lead + (CONV_BUF, SB, C_WIDTH), lambda i: pre + (0, i, 0)),
            pl.BlockSpec(lead + (SB, KV_WIDTH, WINDOW), lambda i: pre + (i, 0, 0)),
            pl.BlockSpec(lead + (SB, KV_WIDTH, WINDOW), lambda i: pre + (i, 0, 0)),
        ]

    c_spec = pl.BlockSpec((None, SB, M_HEADS, M_DK, M_DK), lambda i: (layer, i, 0, 0, 0))
    state_shapes = [
        (nb, M_HEADS, M_DK), (nb, LANES),
        (CONV_BUF, nb, C_WIDTH), (nb, KV_WIDTH, WINDOW), (nb, KV_WIDTH, WINDOW)]
    params = [prm["gbias"], prm["mng"], prm["convw"], prm["convb"], prm["clng"], prm["clnb"],
              prm["sink_row"]]
    return pl.pallas_call(
        functools.partial(_sample_mixer_kernel, layer),
        grid=(nb // SB,),
        in_specs=([pl.BlockSpec((SB, PROJ_W), row2), c_spec] + state_specs((None,), (layer,))
                  + [_whole(a) for a in params]),
        out_specs=[pl.BlockSpec((SB, D_MODEL), row2)] + state_specs((), ()),
        out_shape=[jax.ShapeDtypeStruct((nb, D_MODEL), F32)]
        + [jax.ShapeDtypeStruct(s, F32) for s in state_shapes],
        compiler_params=pltpu.CompilerParams(
            dimension_semantics=("arbitrary",), vmem_limit_bytes=VMEM_LIMIT),
        name="mixer_sample",
    )(hs, c, n, m, conv, ckt, cvt, *params)


def _update_c_kernel(hs_ref, c_ref, m_ref, gbias_ref, cout_ref):
    layer = pl.program_id(0)
    _, wgt, w0 = _sample_gates(hs_ref, gbias_ref[pl.ds(layer, 1), :], m_ref)
    for h in range(M_HEADS):
        k_t = (hs_ref[:, M_WIDTH + h * M_DK:M_WIDTH + (h + 1) * M_DK] * K_SCALE).T
        v = hs_ref[:, 2 * M_WIDTH + h * M_DK:2 * M_WIDTH + (h + 1) * M_DK]
        gv = wgt[:, h:h + 1] * v
        w0h = w0[:, h:h + 1]
        for b in range(SB):
            cout_ref[b, h] = w0h[b:b + 1, :] * c_ref[b, h] + k_t[:, b:b + 1] * gv[b:b + 1, :]


def _update_c(hs_all, c, m, prm):
    depth, nb = hs_all.shape[:2]
    per = lambda l, i: (l, i, 0)
    c_spec = pl.BlockSpec((None, SB, M_HEADS, M_DK, M_DK), lambda l, i: (l, i, 0, 0, 0))
    return pl.pallas_call(
        _update_c_kernel,
        grid=(depth, nb // SB),
        in_specs=[pl.BlockSpec((None, SB, PROJ_W), per), c_spec,
                  pl.BlockSpec((None, SB, LANES), per), _whole(prm["gbias"])],
        out_specs=c_spec,
        out_shape=jax.ShapeDtypeStruct(c.shape, F32),
        compiler_params=pltpu.CompilerParams(
            dimension_semantics=("arbitrary", "arbitrary"), vmem_limit_bytes=VMEM_LIMIT),
        name="update_matrix_memory",
    )(hs_all, c, m, prm["gbias"])


def _outproj_sample_kernel(layer, mg_ref, x_ref, wout_ref, lng_ref, lnb_ref, y_ref):
    lrow = slice(layer, layer + 1)
    y = _dot(mg_ref[...].astype(BF16), wout_ref[...])
    y_ref[...] = _layer_norm(ALPHA * x_ref[...] + y, lng_ref[lrow, :], lnb_ref[lrow, :])


def _outproj_sample(mg, xs, wout, prm, layer):
    nb = xs.shape[0]
    full = lambda shape: pl.BlockSpec(shape, lambda i: (0, 0))
    return pl.pallas_call(
        functools.partial(_outproj_sample_kernel, layer),
        grid=(1,),
        in_specs=[full((nb, D_MODEL)), full((nb, D_MODEL)),
                  pl.BlockSpec((None, D_MODEL, D_MODEL), lambda i: (layer, 0, 0)),
                  _whole(prm["lng"]), _whole(prm["lnb"])],
        out_specs=full((nb, D_MODEL)),
        out_shape=jax.ShapeDtypeStruct((nb, D_MODEL), F32),
        compiler_params=pltpu.CompilerParams(
            dimension_semantics=("arbitrary",), vmem_limit_bytes=VMEM_LIMIT),
        name="outproj_sample",
    )(mg, xs, wout, prm["lng"], prm["lnb"])


def _cast_kernel(w_ref, o_ref):
    o_ref[...] = w_ref[...].astype(BF16)


def _cast_w_out(w_out):
    depth, d, n = w_out.shape
    rows = 512
    return pl.pallas_call(
        _cast_kernel,
        grid=(depth, d // rows),
        in_specs=[pl.BlockSpec((None, rows, n), lambda l, i: (l, i, 0))],
        out_specs=pl.BlockSpec((None, rows, n), lambda l, i: (l, i, 0)),
        out_shape=jax.ShapeDtypeStruct((depth, d, n), BF16),
        compiler_params=pltpu.CompilerParams(dimension_semantics=("arbitrary", "arbitrary")),
        name="cast_w_out",
    )(w_out)


def _lane_rows(*mats):
    v = jnp.concatenate([a.astype(F32) for a in mats], axis=1)
    return jnp.pad(v, ((0, 0), (0, LANES - v.shape[1])))


def _cache_rows_by_window(cache):
    depth, nb = cache.shape[:2]
    return jnp.transpose(cache, (0, 1, 3, 4, 2)).reshape(depth, nb, KV_WIDTH, WINDOW)


def _cache_window_major(cache_t):
    depth, nb = cache_t.shape[:2]
    return jnp.transpose(cache_t.reshape(depth, nb, KV_HEADS, HEAD_DIM, WINDOW), (0, 1, 4, 2, 3))


def kernel(x_prompt, x_sample, state_C, state_n, state_m, state_conv, cache_k, cache_v, w_in, w_out,
           b_igate, b_fgate, m_norm_g, conv_w, conv_b, conv_ln_g, conv_ln_b, sinks, ln_g, ln_b):
    batch, seq, _ = x_prompt.shape
    nb = x_sample.shape[0]
    xp = x_prompt.reshape(batch * seq, D_MODEL)
    xs = x_sample.reshape(nb, D_MODEL)
    wt = jnp.swapaxes(w_in, 1, 2)
    wout = _cast_w_out(w_out)
    prm = {
        "gbias": _lane_rows(b_igate, b_fgate),
        "sink_row": _lane_rows(sinks),
        "mng": m_norm_g, "convw": conv_w, "convb": conv_b, "clng": conv_ln_g, "clnb": conv_ln_b,
        "lng": ln_g, "lnb": ln_b,
    }
    m_in = jnp.pad(state_m, ((0, 0), (0, 0), (0, LANES - M_HEADS)))
    sconv = jnp.swapaxes(state_conv, 1, 2)
    ckt = _cache_rows_by_window(cache_k)
    cvt = _cache_rows_by_window(cache_v)
    st_p = []
    st_s = []
    hs_all = []
    for l in range(DEPTH):
        pm, pc, pa, g, conv_p, hs = _inproj(xp, xs, wt, prm, l, batch)
        xp, c_p, n_p, m_p, k_p, v_p = _mixer(l, sinks, pm, pc, pa, g, xp, wout, prm, batch, seq)
        st_p.append((c_p, n_p, m_p, conv_p, k_p, v_p))
        mg, *states = _sample_mixer(l, hs, state_C, state_n, m_in, sconv, ckt, cvt, prm)
        st_s.append(states)
        hs_all.append(hs)
        xs = _outproj_sample(mg, xs, wout, prm, l)
    c_s = _update_c(jnp.stack(hs_all, axis=0), state_C, m_in, prm)

    def stacked(per_layer):
        return [jnp.stack(a, axis=0) for a in zip(*per_layer)]

    c_p, n_p, m_p, conv_p, k_p, v_p = stacked(st_p)
    n_s, m_s, conv_s, k_s, v_s = stacked(st_s)
    return (xp.reshape(batch, seq, D_MODEL), xs.reshape(nb, 1, D_MODEL),
            c_p, n_p, m_p.reshape(DEPTH, batch, LANES)[:, :, :M_HEADS], conv_p,
            k_p.reshape(DEPTH, batch, WINDOW, KV_HEADS, HEAD_DIM),
            v_p.reshape(DEPTH, batch, WINDOW, KV_HEADS, HEAD_DIM),
            c_s, n_s, m_s[:, :, :M_HEADS], jnp.swapaxes(conv_s, 1, 2),
            _cache_window_major(k_s), _cache_window_major(v_s))
```

```python
import functools

import jax
import jax.numpy as jnp
from jax import lax
from jax.experimental import pallas as pl
from jax.experimental.pallas import tpu as pltpu

F32 = jnp.float32
BF16 = jnp.bfloat16

D_MODEL = 2048
DEPTH = 2
M_WIDTH = 1024
M_HEADS = 4
M_DK = 256
C_WIDTH = 512
CONV_K = 31
CONV_BUF = CONV_K - 1
A_WIDTH = 512
HEAD_DIM = 64
A_HEADS = 8
KV_HEADS = 2
GROUP = A_HEADS // KV_HEADS
KV_WIDTH = KV_HEADS * HEAD_DIM
WINDOW = 128
ALPHA = (2 * DEPTH) ** 0.25
LN_EPS = 1e-5
K_SCALE = M_DK ** -0.5
A_SCALE = HEAD_DIM ** -0.5

LANES = 128
SUBLANES = 8

HM_W = 5 * M_WIDTH
HC_W = 3 * C_WIDTH
HA_W = 2 * A_WIDTH + 2 * KV_WIDTH
MAIN_W = HM_W + HC_W + HA_W
GATE_END = MAIN_W + LANES
PROJ_W = GATE_END
GATE_OFF = 4 * M_WIDTH
N_GATES = 2 * M_HEADS
W_CHUNK = 256
W_SLOTS = 4
PLAIN_CHUNKS = GATE_OFF // W_CHUNK
MAIN_CHUNKS = MAIN_W // W_CHUNK
HA_K = A_WIDTH
HA_V = A_WIDTH + KV_WIDTH
HA_Z = A_WIDTH + 2 * KV_WIDTH

TM_PROJ = 256
TB = 256
A_BLK = 128
CONV_HALO = 32
SB = 8
VMEM_LIMIT = 56 * 1024 * 1024
INPROJ_VMEM_LIMIT = 60 * 1024 * 1024


def _sigmoid(x):
    return 1.0 / (1.0 + jnp.exp(-x))


def _silu(x):
    return x * _sigmoid(x)


def _log_sigmoid(x):
    return jnp.minimum(x, 0.0) - jnp.log1p(jnp.exp(-jnp.abs(x)))


def _layer_norm(x, g, b):
    mu = jnp.mean(x, axis=-1, keepdims=True)
    xc = x - mu
    var = jnp.mean(xc * xc, axis=-1, keepdims=True)
    return xc * lax.rsqrt(var + LN_EPS) * g + b


def _dot(a, b):
    return jnp.dot(a, b, preferred_element_type=F32)


def _dot_nt(a, b):
    return lax.dot_general(a, b, (((1,), (1,)), ((), ())), preferred_element_type=F32)


def _dot_tn(a, b):
    return lax.dot_general(a, b, (((0,), (0,)), ((), ())), preferred_element_type=F32)


def _whole(arr):
    zeros = (0,) * arr.ndim
    return pl.BlockSpec(arr.shape, lambda *_: zeros)


PM_W = 4 * M_WIDTH
PA_QKV = A_WIDTH + 2 * KV_WIDTH
QKV_CHUNK = 512
SHIFTED_ROWS = CONV_HALO + TM_PROJ - SUBLANES


def _stage_rows(src_hbm, dst_ref, stage, sem, n_chunks, src_row):
    def chunk_copy(c, slot):
        return pltpu.make_async_copy(
            src_hbm.at[pl.ds(pl.multiple_of(src_row(c), SUBLANES), W_CHUNK), :],
            stage.at[slot], sem.at[slot])

    for c in range(W_SLOTS - 1):
        chunk_copy(c, c).start()

    def body(c, carry):
        slot = c % W_SLOTS
        chunk_copy(c, slot).wait()
        ahead = c + W_SLOTS - 1

        @pl.when(ahead < n_chunks)
        def _():
            chunk_copy(ahead, ahead % W_SLOTS).start()

        dst_ref[pl.ds(pl.multiple_of(c * W_CHUNK, W_CHUNK), W_CHUNK), :] = stage[slot].astype(BF16)
        return carry

    lax.fori_loop(0, n_chunks, body, 0)


def _stage_w_in(w_hbm, wt_ref, stage, sem):
    _stage_rows(w_hbm, wt_ref, stage, sem, MAIN_CHUNKS,
                lambda c: jnp.where(c < PLAIN_CHUNKS, c * W_CHUNK, c * W_CHUNK + N_GATES))
    stage[0, 0:LANES, :] = jnp.zeros((LANES, D_MODEL), F32)
    gate_copy = pltpu.make_async_copy(
        w_hbm.at[GATE_OFF:GATE_OFF + N_GATES, :], stage.at[0, 0:N_GATES, :], sem.at[0])
    gate_copy.start()
    gate_copy.wait()
    wt_ref[MAIN_W:GATE_END, :] = stage[0, 0:LANES, :].astype(BF16)


def _inproj_kernel(layer, blocks_per_seq, x_ref, xs_ref, w_hbm, convw_ref, convb_ref, clng_ref,
                   clnb_ref, pm_ref, pc_ref, pa_ref, g_ref, conv_ref, hs_ref,
                   wt_ref, stage, sem, xbuf, abuf, sbuf, ybuf, zbuf):
    i = pl.program_id(0)
    lrow = slice(layer, layer + 1)

    n_sample = xs_ref.shape[0]

    @pl.when(i % blocks_per_seq == 0)
    def _():
        abuf[0:CONV_HALO, :] = jnp.zeros((CONV_HALO, C_WIDTH), F32)

    xbuf[0:TM_PROJ, :] = x_ref[...].astype(BF16)

    def project_block(n_extra):
        n_rows = TM_PROJ + n_extra

        def proj(c0, c1):
            full = _dot_nt(xbuf[0:n_rows, :], wt_ref[c0:c1, :])
            if n_extra:
                hs_ref[:, c0:c1] = full[TM_PROJ:n_rows, :]
                return full[0:TM_PROJ, :]
            return full

        for c in range(0, 3 * M_WIDTH, QKV_CHUNK):
            pm_ref[:, c:c + QKV_CHUNK] = proj(c, c + QKV_CHUNK).astype(BF16)
        for h in range(M_HEADS):
            o = proj(3 * M_WIDTH + h * M_DK, 3 * M_WIDTH + (h + 1) * M_DK)
            z = proj(4 * M_WIDTH + h * M_DK, 4 * M_WIDTH + (h + 1) * M_DK)
            pm_ref[:, 3 * M_WIDTH + h * M_DK:3 * M_WIDTH + (h + 1) * M_DK] = (
                _sigmoid(o) * _silu(z)).astype(BF16)

        cu = proj(HM_W, HM_W + C_WIDTH)
        cg = proj(HM_W + C_WIDTH, HM_W + 2 * C_WIDTH)
        abuf[CONV_HALO:CONV_HALO + TM_PROJ, :] = cu * _sigmoid(cg)
        zbuf[...] = _silu(proj(HM_W + 2 * C_WIDTH, HM_W + 3 * C_WIDTH))
        for p in range(1, SUBLANES):
            sbuf[p - 1] = abuf[p:p + SHIFTED_ROWS, :]
        tap0 = CONV_HALO - CONV_BUF
        for r in range(0, TM_PROJ, 128):
            for c in range(0, C_WIDTH, LANES):
                acc = jnp.zeros((128, LANES), F32)
                for t in range(CONV_K):
                    p = (tap0 + t) % SUBLANES
                    base = r + tap0 + t - p
                    src = abuf if p == 0 else sbuf.at[p - 1]
                    acc = acc + (src[base:base + 128, c:c + LANES]
                                 * convw_ref[layer, t:t + 1, c:c + LANES])
                ybuf[r:r + 128, c:c + LANES] = acc + convb_ref[lrow, c:c + LANES]
        conv_ref[0] = abuf[CONV_HALO + TM_PROJ - CONV_BUF:CONV_HALO + TM_PROJ, :]
        abuf[0:CONV_HALO, :] = abuf[TM_PROJ:TM_PROJ + CONV_HALO, :]
        yc = _layer_norm(ybuf[...], clng_ref[lrow, :], clnb_ref[lrow, :])
        pc_ref[...] = (_silu(yc) * zbuf[...]).astype(BF16)

        base = HM_W + HC_W
        pa_ref[:, 0:PA_QKV] = proj(base, base + PA_QKV).astype(BF16)
        pa_ref[:, PA_QKV:HA_W] = _silu(proj(base + PA_QKV, base + HA_W)).astype(BF16)
        g_ref[...] = proj(MAIN_W, GATE_END)

    @pl.when(i == 0)
    def _():
        _stage_w_in(w_hbm.at[layer], wt_ref, stage, sem)
        xbuf[TM_PROJ:TM_PROJ + n_sample, :] = xs_ref[...].astype(BF16)
        project_block(n_sample)

    @pl.when(i > 0)
    def _():
        project_block(0)


def _inproj(x2d, xs, w_in_t, prm, layer, batch):
    m = x2d.shape[0]
    ns = xs.shape[0]
    nblk = m // TM_PROJ
    row = lambda i: (i, 0)
    once = lambda i: (0, 0)
    params = [prm["convw"], prm["convb"], prm["clng"], prm["clnb"]]
    return pl.pallas_call(
        functools.partial(_inproj_kernel, layer, nblk // batch),
        grid=(nblk,),
        in_specs=[
            pl.BlockSpec((TM_PROJ, D_MODEL), row),
            pl.BlockSpec((ns, D_MODEL), once),
            pl.BlockSpec(memory_space=pl.ANY),
        ] + [_whole(a) for a in params],
        out_specs=[
            pl.BlockSpec((TM_PROJ, PM_W), row),
            pl.BlockSpec((TM_PROJ, C_WIDTH), row),
            pl.BlockSpec((TM_PROJ, HA_W), row),
            pl.BlockSpec((TM_PROJ, LANES), row),
            pl.BlockSpec((1, CONV_BUF, C_WIDTH), lambda i: (i // (nblk // batch), 0, 0)),
            pl.BlockSpec((ns, PROJ_W), once),
        ],
        out_shape=[
            jax.ShapeDtypeStruct((m, PM_W), BF16),
            jax.ShapeDtypeStruct((m, C_WIDTH), BF16),
            jax.ShapeDtypeStruct((m, HA_W), BF16),
            jax.ShapeDtypeStruct((m, LANES), F32),
            jax.ShapeDtypeStruct((batch, CONV_BUF, C_WIDTH), F32),
            jax.ShapeDtypeStruct((ns, PROJ_W), F32),
        ],
        scratch_shapes=[
            pltpu.VMEM((PROJ_W, D_MODEL), BF16),
            pltpu.VMEM((W_SLOTS, W_CHUNK, D_MODEL), F32),
            pltpu.SemaphoreType.DMA((W_SLOTS,)),
            pltpu.VMEM((TM_PROJ + ns, D_MODEL), BF16),
            pltpu.VMEM((CONV_HALO + TM_PROJ, C_WIDTH), F32),
            pltpu.VMEM((SUBLANES - 1, SHIFTED_ROWS, C_WIDTH), F32),
            pltpu.VMEM((TM_PROJ, C_WIDTH), F32),
            pltpu.VMEM((TM_PROJ, C_WIDTH), F32),
        ],
        compiler_params=pltpu.CompilerParams(
            dimension_semantics=("arbitrary",), vmem_limit_bytes=INPROJ_VMEM_LIMIT),
        name="inproj_prompt",
    )(x2d, xs, w_in_t, *params)


def _cumsum_rows(x):
    n = x.shape[0]
    row = lax.broadcasted_iota(jnp.int32, x.shape, 0)
    s = 1
    while s < n:
        x = x + jnp.where(row >= s, pltpu.roll(x, s, axis=0), 0.0)
        s *= 2
    return x


def _mixer_kernel(layer, sinks_ref, hm_ref, pc_ref, ha_ref, g_ref, x_ref, wout_ref,
                  gbias_ref, mng_ref, lng_ref, lnb_ref,
                  y_ref, c_ref, n_ref, m_ref, k_ref, v_ref,
                  kprev, vprev, merged):
    j = pl.program_id(1)
    lrow = slice(layer, layer + 1)

    @pl.when(j == 0)
    def _():
        c_ref[...] = jnp.zeros_like(c_ref)
        n_ref[...] = jnp.zeros_like(n_ref)
        m_ref[...] = jnp.zeros_like(m_ref)
        kprev[...] = jnp.zeros_like(kprev)
        vprev[...] = jnp.zeros_like(vprev)

    gb = g_ref[...] + gbias_ref[lrow, :]
    bsum = _cumsum_rows(_log_sigmoid(gb))
    bsum = pltpu.roll(bsum, LANES - M_HEADS, axis=1)
    a_all = gb - bsum
    a_t = a_all.T
    m_prev = m_ref[0]
    row_i = lax.broadcasted_iota(jnp.int32, (TB, TB), 0)
    col_i = lax.broadcasted_iota(jnp.int32, (TB, TB), 1)
    causal = col_i <= row_i
    lane_i = lax.broadcasted_iota(jnp.int32, (1, LANES), 1)
    m_new = m_prev

    for h in range(M_HEADS):
        cs = slice(h * M_DK, (h + 1) * M_DK)
        q = hm_ref[:, cs]
        k = hm_ref[:, M_WIDTH + h * M_DK:M_WIDTH + (h + 1) * M_DK]
        v = hm_ref[:, 2 * M_WIDTH + h * M_DK:2 * M_WIDTH + (h + 1) * M_DK]
        logit = jnp.where(causal, a_t[h:h + 1, :], -jnp.inf)
        m0 = m_prev[:, h:h + 1]
        mx = jnp.maximum(jnp.max(logit, axis=1, keepdims=True), m0)
        w = jnp.exp(logit - mx)
        w0 = jnp.exp(m0 - mx)
        s = _dot_nt(q, k) * (w * K_SCALE)
        c0 = c_ref[0, h]
        n0 = n_ref[0, h:h + 1, :]
        num = _dot(s.astype(BF16), v) + w0 * _dot(q, c0.astype(BF16))
        den = (jnp.sum(s, axis=1, keepdims=True)
               + w0 * jnp.sum(q.astype(F32) * n0, axis=1, keepdims=True))
        m_t = bsum[:, h:h + 1] + mx
        hh = num / jnp.maximum(jnp.abs(den), jnp.exp(-m_t))
        mu = jnp.mean(hh, axis=1, keepdims=True)
        hc = hh - mu
        var = jnp.mean(hc * hc, axis=1, keepdims=True)
        hn = hc * lax.rsqrt(var + LN_EPS) * mng_ref[lrow, cs]
        gate = hm_ref[:, 3 * M_WIDTH + h * M_DK:3 * M_WIDTH + (h + 1) * M_DK].astype(F32)
        merged[:, cs] = (hn * gate).astype(BF16)
        mx_end = mx[TB - 1:TB, :]
        g_col = jnp.exp(a_all[:, h:h + 1] - mx_end) * K_SCALE
        g0 = jnp.exp(m0 - mx_end)
        kg = k.astype(F32) * g_col
        c_ref[0, h] = g0 * c0 + _dot_tn(kg.astype(BF16), v)
        n_ref[0, h:h + 1, :] = g0 * n0 + jnp.sum(kg, axis=0, keepdims=True)
        m_end = bsum[TB - 1:TB, h:h + 1] + mx_end
        m_new = jnp.where(lane_i == h, m_end, m_new)
    m_ref[0] = m_new

    merged[:, M_WIDTH:M_WIDTH + C_WIDTH] = pc_ref[...]

    arow = lax.broadcasted_iota(jnp.int32, (A_BLK, 2 * A_BLK), 0)
    acol = lax.broadcasted_iota(jnp.int32, (A_BLK, 2 * A_BLK), 1)
    rel = A_BLK + arow - acol
    band = (rel >= 0) & (rel <= WINDOW)
    for sub in range(TB // A_BLK):
        r0 = sub * A_BLK
        if sub == 0:
            kcat = jnp.concatenate([kprev[...], ha_ref[0:A_BLK, HA_K:HA_K + KV_WIDTH]], axis=0)
            vcat = jnp.concatenate([vprev[...], ha_ref[0:A_BLK, HA_V:HA_V + KV_WIDTH]], axis=0)
            mask = band & (acol >= jnp.where(j > 0, 0, A_BLK))
        else:
            kcat = ha_ref[r0 - A_BLK:r0 + A_BLK, HA_K:HA_K + KV_WIDTH]
            vcat = ha_ref[r0 - A_BLK:r0 + A_BLK, HA_V:HA_V + KV_WIDTH]
            mask = band
        ones = jnp.ones((2 * A_BLK, HEAD_DIM), BF16)
        vext = [jnp.concatenate([vcat[:, kvh * HEAD_DIM:(kvh + 1) * HEAD_DIM], ones], axis=1)
                for kvh in range(KV_HEADS)]
        for hp in range(A_HEADS // 2):
            outs = []
            for hq in (2 * hp, 2 * hp + 1):
                kvh = hq // GROUP
                qh = ha_ref[r0:r0 + A_BLK, hq * HEAD_DIM:(hq + 1) * HEAD_DIM]
                kh = kcat[:, kvh * HEAD_DIM:(kvh + 1) * HEAD_DIM]
                sc = jnp.where(mask, _dot_nt(qh, kh) * A_SCALE, -jnp.inf)
                sink = sinks_ref[layer, hq]
                smax = jnp.maximum(jnp.max(sc, axis=1, keepdims=True), sink)
                pv = _dot(jnp.exp(sc - smax).astype(BF16), vext[kvh])
                denom = pv[:, HEAD_DIM:HEAD_DIM + 1] + jnp.exp(sink - smax)
                outs.append(pv[:, 0:HEAD_DIM] / denom)
            ao = jnp.concatenate(outs, axis=1)
            gate = ha_ref[r0:r0 + A_BLK, HA_Z + hp * LANES:HA_Z + (hp + 1) * LANES].astype(F32)
            col = M_WIDTH + C_WIDTH + hp * LANES
            merged[r0:r0 + A_BLK, col:col + LANES] = (ao * gate).astype(BF16)
    kprev[...] = ha_ref[TB - A_BLK:TB, HA_K:HA_K + KV_WIDTH]
    vprev[...] = ha_ref[TB - A_BLK:TB, HA_V:HA_V + KV_WIDTH]
    k_ref[0] = ha_ref[TB - WINDOW:TB, HA_K:HA_K + KV_WIDTH].astype(F32)
    v_ref[0] = ha_ref[TB - WINDOW:TB, HA_V:HA_V + KV_WIDTH].astype(F32)

    y = _dot(merged[...], wout_ref[...])
    y_ref[...] = _layer_norm(ALPHA * x_ref[...] + y, lng_ref[lrow, :], lnb_ref[lrow, :])


def _mixer(layer, sinks, pm, pc, pa, g, x2d, wout, prm, batch, seq):
    nb = seq // TB
    row = lambda b, j: (b * nb + j, 0)
    per_b3 = lambda b, j: (b, 0, 0)
    state_shapes = [
        (batch, M_HEADS, M_DK, M_DK), (batch, M_HEADS, M_DK), (batch, 1, LANES),
        (batch, WINDOW, KV_WIDTH), (batch, WINDOW, KV_WIDTH)]
    params = [prm["gbias"], prm["mng"], prm["lng"], prm["lnb"]]
    return pl.pallas_call(
        functools.partial(_mixer_kernel, layer),
        grid=(batch, nb),
        in_specs=[
            pl.BlockSpec(memory_space=pltpu.SMEM),
            pl.BlockSpec((TB, PM_W), row),
            pl.BlockSpec((TB, C_WIDTH), row),
            pl.BlockSpec((TB, HA_W), row),
            pl.BlockSpec((TB, LANES), row),
            pl.BlockSpec((TB, D_MODEL), row),
            pl.BlockSpec((None, D_MODEL, D_MODEL), lambda b, j: (layer, 0, 0)),
        ] + [_whole(a) for a in params],
        out_specs=[
            pl.BlockSpec((TB, D_MODEL), row),
            pl.BlockSpec((1, M_HEADS, M_DK, M_DK), lambda b, j: (b, 0, 0, 0)),
            pl.BlockSpec((1, M_HEADS, M_DK), per_b3),
            pl.BlockSpec((1, 1, LANES), per_b3),
            pl.BlockSpec((1, WINDOW, KV_WIDTH), per_b3),
            pl.BlockSpec((1, WINDOW, KV_WIDTH), per_b3),
        ],
        out_shape=[jax.ShapeDtypeStruct((batch * seq, D_MODEL), F32)]
        + [jax.ShapeDtypeStruct(s, F32) for s in state_shapes],
        scratch_shapes=[
            pltpu.VMEM((A_BLK, KV_WIDTH), BF16),
            pltpu.VMEM((A_BLK, KV_WIDTH), BF16),
            pltpu.VMEM((TB, D_MODEL), BF16),
        ],
        compiler_params=pltpu.CompilerParams(
            dimension_semantics=("arbitrary", "arbitrary"), vmem_limit_bytes=VMEM_LIMIT),
        name="mixer_prompt",
    )(sinks, pm, pc, pa, g, x2d, wout, *params)


def _sample_gates(hs_ref, gbias_row, m_ref):
    gb = hs_ref[:, MAIN_W:GATE_END] + gbias_row
    lf = pltpu.roll(_log_sigmoid(gb), LANES - M_HEADS, axis=1)
    m0 = m_ref[...]
    m_new = jnp.maximum(lf + m0, gb)
    wgt = jnp.exp(gb - m_new)
    w0 = jnp.exp(lf + m0 - m_new)
    return m_new, wgt, w0


def _sample_mixer_kernel(layer, hs_ref, c_ref, n_ref, m_ref, conv_ref, ckt_ref, cvt_ref,
                         gbias_ref, mng_ref, convw_ref, convb_ref, clng_ref, clnb_ref, sink_ref,
                         merged_ref, nout_ref, mout_ref, convout_ref, kout_ref, vout_ref):
    lrow = slice(layer, layer + 1)
    m_new, wgt, w0 = _sample_gates(hs_ref, gbias_ref[lrow, :], m_ref)
    einv = jnp.exp(-m_new)
    mout_ref[...] = m_new
    for h in range(M_HEADS):
        cs = slice(h * M_DK, (h + 1) * M_DK)
        q = hs_ref[:, cs]
        k = hs_ref[:, M_WIDTH + h * M_DK:M_WIDTH + (h + 1) * M_DK] * K_SCALE
        v = hs_ref[:, 2 * M_WIDTH + h * M_DK:2 * M_WIDTH + (h + 1) * M_DK]
        q_t = q.T
        wh = wgt[:, h:h + 1]
        w0h = w0[:, h:h + 1]
        n0 = n_ref[:, h, :]
        s = jnp.sum(q * k, axis=1, keepdims=True) * wh
        qc = jnp.concatenate(
            [jnp.sum(q_t[:, b:b + 1] * c_ref[b, h], axis=0, keepdims=True) for b in range(SB)],
            axis=0)
        num = s * v + w0h * qc
        den = s + w0h * jnp.sum(q * n0, axis=1, keepdims=True)
        hh = num / jnp.maximum(jnp.abs(den), einv[:, h:h + 1])
        nout_ref[:, h, :] = w0h * n0 + wh * k
        mu = jnp.mean(hh, axis=1, keepdims=True)
        hc = hh - mu
        var = jnp.mean(hc * hc, axis=1, keepdims=True)
        hn = hc * lax.rsqrt(var + LN_EPS) * mng_ref[lrow, cs]
        o = hs_ref[:, 3 * M_WIDTH + h * M_DK:3 * M_WIDTH + (h + 1) * M_DK]
        z = hs_ref[:, 4 * M_WIDTH + h * M_DK:4 * M_WIDTH + (h + 1) * M_DK]
        merged_ref[:, cs] = hn * _sigmoid(o) * _silu(z)

    cu = hs_ref[:, HM_W:HM_W + C_WIDTH]
    cg = hs_ref[:, HM_W + C_WIDTH:HM_W + 2 * C_WIDTH]
    cz = hs_ref[:, HM_W + 2 * C_WIDTH:HM_W + 3 * C_WIDTH]
    a = cu * _sigmoid(cg)
    yc = a * convw_ref[layer, CONV_BUF:CONV_K, :] + convb_ref[lrow, :]
    for t in range(CONV_BUF):
        yc = yc + conv_ref[t] * convw_ref[layer, t:t + 1, :]
    for t in range(CONV_BUF - 1):
        convout_ref[t] = conv_ref[t + 1]
    convout_ref[CONV_BUF - 1] = a
    yc = _layer_norm(yc, clng_ref[lrow, :], clnb_ref[lrow, :])
    merged_ref[:, M_WIDTH:M_WIDTH + C_WIDTH] = _silu(yc) * _silu(cz)

    ab = HM_W + HC_W
    lane = lax.broadcasted_iota(jnp.int32, (1, LANES), 1)
    rows_b = lax.broadcasted_iota(jnp.int32, (A_HEADS * SB, LANES), 0) % SB
    wcol = lax.broadcasted_iota(jnp.int32, (KV_WIDTH, WINDOW), 1)
    knew = hs_ref[:, ab + HA_K:ab + HA_K + KV_WIDTH]
    vnew = hs_ref[:, ab + HA_V:ab + HA_V + KV_WIDTH]
    q_parts, scn_parts, sink_parts = [], [], []
    for hq in range(A_HEADS):
        kvh = hq // GROUP
        t, u = hq // 2, hq % 2
        tile = hs_ref[:, ab + t * LANES:ab + (t + 1) * LANES]
        if u != kvh:
            tile = pltpu.roll(tile, HEAD_DIM, axis=1)
        qh = jnp.where((lane // HEAD_DIM) == kvh, tile, 0.0)
        q_parts.append(qh)
        scn_parts.append(jnp.sum(qh * knew, axis=1, keepdims=True))
        sink_parts.append(jnp.broadcast_to(sink_ref[lrow, hq:hq + 1], (SB, 1)))
    q_all = jnp.concatenate(q_parts, axis=0).astype(BF16)
    sc_new = jnp.concatenate(scn_parts, axis=0) * A_SCALE
    sink_all = jnp.concatenate(sink_parts, axis=0)
    sc = jnp.zeros((A_HEADS * SB, WINDOW), F32)
    for b in range(SB):
        sc = jnp.where(rows_b == b, _dot(q_all, ckt_ref[b].astype(BF16)), sc)
    sc = sc * A_SCALE
    smax = jnp.maximum(jnp.maximum(jnp.max(sc, axis=1, keepdims=True), sc_new), sink_all)
    e = jnp.exp(sc - smax)
    e_new = jnp.exp(sc_new - smax)
    denom = jnp.sum(e, axis=1, keepdims=True) + e_new + jnp.exp(sink_all - smax)
    p = (e / denom).astype(BF16)
    ob = (e_new / denom) * jnp.concatenate([vnew] * A_HEADS, axis=0)
    for b in range(SB):
        ob = ob + jnp.where(rows_b == b, _dot_nt(p, cvt_ref[b].astype(BF16)), 0.0)
    tiles = []
    for t in range(A_HEADS // 2):
        halves = []
        for u in range(2):
            hq = 2 * t + u
            x = ob[hq * SB:(hq + 1) * SB, :]
            if u != hq // GROUP:
                x = pltpu.roll(x, HEAD_DIM, axis=1)
            halves.append(x)
        tiles.append(jnp.where(lane < HEAD_DIM, halves[0], halves[1]))
    az = hs_ref[:, ab + HA_Z:ab + HA_Z + A_WIDTH]
    merged_ref[:, M_WIDTH + C_WIDTH:D_MODEL] = jnp.concatenate(tiles, axis=1) * _silu(az)
    knew_t = knew.T
    vnew_t = vnew.T
    for b in range(SB):
        kout_ref[b] = jnp.where(wcol == WINDOW - 1, knew_t[:, b:b + 1],
                                pltpu.roll(ckt_ref[b], WINDOW - 1, axis=1))
        vout_ref[b] = jnp.where(wcol == WINDOW - 1, vnew_t[:, b:b + 1],
                                pltpu.roll(cvt_ref[b], WINDOW - 1, axis=1))


def _sample_mixer(layer, hs, c, n, m, conv, ckt, cvt, prm):
    nb = hs.shape[0]
    row2 = lambda i: (i, 0)

    def state_specs(lead, pre):
        return [
            pl.BlockSpec(lead + (SB, M_HEADS, M_DK), lambda i: pre + (i, 0, 0)),
            pl.BlockSpec(lead + (SB, LANES), lambda i: pre + (i, 0)),
            pl.BlockSpec(lead + (CONV_BUF, SB, C_WIDTH), lambda i: pre + (0, i, 0)),
            pl.BlockSpec(lead + (SB, KV_WIDTH, WINDOW), lambda i: pre + (i, 0, 0)),
            pl.BlockSpec(lead + (SB, KV_WIDTH, WINDOW), lambda i: pre + (i, 0, 0)),
        ]

    c_spec = pl.BlockSpec((None, SB, M_HEADS, M_DK, M_DK), lambda i: (layer, i, 0, 0, 0))
    state_shapes = [
        (nb, M_HEADS, M_DK), (nb, LANES),
        (CONV_BUF, nb, C_WIDTH), (nb, KV_WIDTH, WINDOW), (nb, KV_WIDTH, WINDOW)]
    params = [prm["gbias"], prm["mng"], prm["convw"], prm["convb"], prm["clng"], prm["clnb"],
              prm["sink_row"]]
    return pl.pallas_call(
        functools.partial(_sample_mixer_kernel, layer),
        grid=(nb // SB,),
        in_specs=([pl.BlockSpec((SB, PROJ_W), row2), c_spec] + state_specs((None,), (layer,))
                  + [_whole(a) for a in params]),
        out_specs=[pl.BlockSpec((SB, D_MODEL), row2)] + state_specs((), ()),
        out_shape=[jax.ShapeDtypeStruct((nb, D_MODEL), F32)]
        + [jax.ShapeDtypeStruct(s, F32) for s in state_shapes],
        compiler_params=pltpu.CompilerParams(
            dimension_semantics=("arbitrary",), vmem_limit_bytes=VMEM_LIMIT),
        name="mixer_sample",
    )(hs, c, n, m, conv, ckt, cvt, *params)


def _update_c_kernel(hs_ref, c_ref, m_ref, gbias_ref, cout_ref):
    layer = pl.program_id(0)
    _, wgt, w0 = _sample_gates(hs_ref, gbias_ref[pl.ds(layer, 1), :], m_ref)
    for h in range(M_HEADS):
        k_t = (hs_ref[:, M_WIDTH + h * M_DK:M_WIDTH + (h + 1) * M_DK] * K_SCALE).T
        v = hs_ref[:, 2 * M_WIDTH + h * M_DK:2 * M_WIDTH + (h + 1) * M_DK]
        gv = wgt[:, h:h + 1] * v
        w0h = w0[:, h:h + 1]
        for b in range(SB):
            cout_ref[b, h] = w0h[b:b + 1, :] * c_ref[b, h] + k_t[:, b:b + 1] * gv[b:b + 1, :]


def _update_c(hs_all, c, m, prm):
    depth, nb = hs_all.shape[:2]
    per = lambda l, i: (l, i, 0)
    c_spec = pl.BlockSpec((None, SB, M_HEADS, M_DK, M_DK), lambda l, i: (l, i, 0, 0, 0))
    return pl.pallas_call(
        _update_c_kernel,
        grid=(depth, nb // SB),
        in_specs=[pl.BlockSpec((None, SB, PROJ_W), per), c_spec,
                  pl.BlockSpec((None, SB, LANES), per), _whole(prm["gbias"])],
        out_specs=c_spec,
        out_shape=jax.ShapeDtypeStruct(c.shape, F32),
        compiler_params=pltpu.CompilerParams(
            dimension_semantics=("arbitrary", "arbitrary"), vmem_limit_bytes=VMEM_LIMIT),
        name="update_matrix_memory",
    )(hs_all, c, m, prm["gbias"])


def _outproj_sample_kernel(layer, mg_ref, x_ref, wout_ref, lng_ref, lnb_ref, y_ref):
    lrow = slice(layer, layer + 1)
    y = _dot(mg_ref[...].astype(BF16), wout_ref[...])
    y_ref[...] = _layer_norm(ALPHA * x_ref[...] + y, lng_ref[lrow, :], lnb_ref[lrow, :])


def _outproj_sample(mg, xs, wout, prm, layer):
    nb = xs.shape[0]
    full = lambda shape: pl.BlockSpec(shape, lambda i: (0, 0))
    return pl.pallas_call(
        functools.partial(_outproj_sample_kernel, layer),
        grid=(1,),
        in_specs=[full((nb, D_MODEL)), full((nb, D_MODEL)),
                  pl.BlockSpec((None, D_MODEL, D_MODEL), lambda i: (layer, 0, 0)),
                  _whole(prm["lng"]), _whole(prm["lnb"])],
        out_specs=full((nb, D_MODEL)),
        out_shape=jax.ShapeDtypeStruct((nb, D_MODEL), F32),
        compiler_params=pltpu.CompilerParams(
            dimension_semantics=("arbitrary",), vmem_limit_bytes=VMEM_LIMIT),
        name="outproj_sample",
    )(mg, xs, wout, prm["lng"], prm["lnb"])


def _cast_kernel(w_ref, o_ref):
    o_ref[...] = w_ref[...].astype(BF16)


def _cast_w_out(w_out):
    depth, d, n = w_out.shape
    rows = 512
    return pl.pallas_call(
        _cast_kernel,
        grid=(depth, d // rows),
        in_specs=[pl.BlockSpec((None, rows, n), lambda l, i: (l, i, 0))],
        out_specs=pl.BlockSpec((None, rows, n), lambda l, i: (l, i, 0)),
        out_shape=jax.ShapeDtypeStruct((depth, d, n), BF16),
        compiler_params=pltpu.CompilerParams(dimension_semantics=("arbitrary", "arbitrary")),
        name="cast_w_out",
    )(w_out)


def _lane_rows(*mats):
    v = jnp.concatenate([a.astype(F32) for a in mats], axis=1)
    return jnp.pad(v, ((0, 0), (0, LANES - v.shape[1])))


def _cache_rows_by_window(cache):
    depth, nb = cache.shape[:2]
    return jnp.transpose(cache, (0, 1, 3, 4, 2)).reshape(depth, nb, KV_WIDTH, WINDOW)


def _cache_window_major(cache_t):
    depth, nb = cache_t.shape[:2]
    return jnp.transpose(cache_t.reshape(depth, nb, KV_HEADS, HEAD_DIM, WINDOW), (0, 1, 4, 2, 3))


def kernel(x_prompt, x_sample, state_C, state_n, state_m, state_conv, cache_k, cache_v, w_in, w_out,
           b_igate, b_fgate, m_norm_g, conv_w, conv_b, conv_ln_g, conv_ln_b, sinks, ln_g, ln_b):
    batch, seq, _ = x_prompt.shape
    nb = x_sample.shape[0]
    xp = x_prompt.reshape(batch * seq, D_MODEL)
    xs = x_sample.reshape(nb, D_MODEL)
    wt = jnp.swapaxes(w_in, 1, 2)
    wout = _cast_w_out(w_out)
    prm = {
        "gbias": _lane_rows(b_igate, b_fgate),
        "sink_row": _lane_rows(sinks),
        "mng": m_norm_g, "convw": conv_w, "convb": conv_b, "clng": conv_ln_g, "clnb": conv_ln_b,
        "lng": ln_g, "lnb": ln_b,
    }
    m_in = jnp.pad(state_m, ((0, 0), (0, 0), (0, LANES - M_HEADS)))
    sconv = jnp.swapaxes(state_conv, 1, 2)
    ckt = _cache_rows_by_window(cache_k)
    cvt = _cache_rows_by_window(cache_v)
    st_p = []
    st_s = []
    hs_all = []
    for l in range(DEPTH):
        pm, pc, pa, g, conv_p, hs = _inproj(xp, xs, wt, prm, l, batch)
        xp, c_p, n_p, m_p, k_p, v_p = _mixer(l, sinks, pm, pc, pa, g, xp, wout, prm, batch, seq)
        st_p.append((c_p, n_p, m_p, conv_p, k_p, v_p))
        mg, *states = _sample_mixer(l, hs, state_C, state_n, m_in, sconv, ckt, cvt, prm)
        st_s.append(states)
        hs_all.append(hs)
        xs = _outproj_sample(mg, xs, wout, prm, l)
    c_s = _update_c(jnp.stack(hs_all, axis=0), state_C, m_in, prm)

    def stacked(per_layer):
        return [jnp.stack(a, axis=0) for a in zip(*per_layer)]

    c_p, n_p, m_p, conv_p, k_p, v_p = stacked(st_p)
    n_s, m_s, conv_s, k_s, v_s = stacked(st_s)
    return (xp.reshape(batch, seq, D_MODEL), xs.reshape(nb, 1, D_MODEL),
            c_p, n_p, m_p.reshape(DEPTH, batch, LANES)[:, :, :M_HEADS], conv_p,
            k_p.reshape(DEPTH, batch, WINDOW, KV_HEADS, HEAD_DIM),
            v_p.reshape(DEPTH, batch, WINDOW, KV_HEADS, HEAD_DIM),
            c_s, n_s, m_s[:, :, :M_HEADS], jnp.swapaxes(conv_s, 1, 2),
            _cache_window_major(k_s), _cache_window_major(v_s))
```

```python
import functools

import jax
import jax.numpy as jnp
from jax import lax
from jax.experimental import pallas as pl
from jax.experimental.pallas import tpu as pltpu

F32 = jnp.float32
BF16 = jnp.bfloat16

D_MODEL = 2048
DEPTH = 2
M_WIDTH = 1024
M_HEADS = 4
M_DK = 256
C_WIDTH = 512
CONV_K = 31
CONV_BUF = CONV_K - 1
A_WIDTH = 512
HEAD_DIM = 64
A_HEADS = 8
KV_HEADS = 2
GROUP = A_HEADS // KV_HEADS
KV_WIDTH = KV_HEADS * HEAD_DIM
WINDOW = 128
ALPHA = (2 * DEPTH) ** 0.25
LN_EPS = 1e-5
K_SCALE = M_DK ** -0.5
A_SCALE = HEAD_DIM ** -0.5

LANES = 128
SUBLANES = 8

HM_W = 5 * M_WIDTH
HC_W = 3 * C_WIDTH
HA_W = 2 * A_WIDTH + 2 * KV_WIDTH
MAIN_W = HM_W + HC_W + HA_W
GATE_END = MAIN_W + LANES
PROJ_W = GATE_END
GATE_OFF = 4 * M_WIDTH
N_GATES = 2 * M_HEADS
W_CHUNK = 256
W_SLOTS = 4
PLAIN_CHUNKS = GATE_OFF // W_CHUNK
MAIN_CHUNKS = MAIN_W // W_CHUNK
HA_K = A_WIDTH
HA_V = A_WIDTH + KV_WIDTH
HA_Z = A_WIDTH + 2 * KV_WIDTH

TM_PROJ = 256
TB = 256
A_BLK = 128
CONV_HALO = 32
SB = 8
VMEM_LIMIT = 56 * 1024 * 1024
INPROJ_VMEM_LIMIT = 60 * 1024 * 1024


def _sigmoid(x):
    return 1.0 / (1.0 + jnp.exp(-x))


def _silu(x):
    return x * _sigmoid(x)


def _log_sigmoid(x):
    return jnp.minimum(x, 0.0) - jnp.log1p(jnp.exp(-jnp.abs(x)))


def _layer_norm(x, g, b):
    mu = jnp.mean(x, axis=-1, keepdims=True)
    xc = x - mu
    var = jnp.mean(xc * xc, axis=-1, keepdims=True)
    return xc * lax.rsqrt(var + LN_EPS) * g + b


def _dot(a, b):
    return jnp.dot(a, b, preferred_element_type=F32)


def _dot_nt(a, b):
    return lax.dot_general(a, b, (((1,), (1,)), ((), ())), preferred_element_type=F32)


def _dot_tn(a, b):
    return lax.dot_general(a, b, (((0,), (0,)), ((), ())), preferred_element_type=F32)


def _whole(arr):
    zeros = (0,) * arr.ndim
    return pl.BlockSpec(arr.shape, lambda *_: zeros)


PM_W = 4 * M_WIDTH
PA_QKV = A_WIDTH + 2 * KV_WIDTH
QKV_CHUNK = 512
SHIFTED_ROWS = CONV_HALO + TM_PROJ - SUBLANES


def _stage_rows(src_hbm, dst_ref, stage, sem, n_chunks, src_row):
    def chunk_copy(c, slot):
        return pltpu.make_async_copy(
            src_hbm.at[pl.ds(pl.multiple_of(src_row(c), SUBLANES), W_CHUNK), :],
            stage.at[slot], sem.at[slot])

    for c in range(W_SLOTS - 1):
        chunk_copy(c, c).start()

    def body(c, carry):
        slot = c % W_SLOTS
        chunk_copy(c, slot).wait()
        ahead = c + W_SLOTS - 1

        @pl.when(ahead < n_chunks)
        def _():
            chunk_copy(ahead, ahead % W_SLOTS).start()

        dst_ref[pl.ds(pl.multiple_of(c * W_CHUNK, W_CHUNK), W_CHUNK), :] = stage[slot].astype(BF16)
        return carry

    lax.fori_loop(0, n_chunks, body, 0)


def _stage_w_in(w_hbm, wt_ref, stage, sem):
    _stage_rows(w_hbm, wt_ref, stage, sem, MAIN_CHUNKS,
                lambda c: jnp.where(c < PLAIN_CHUNKS, c * W_CHUNK, c * W_CHUNK + N_GATES))
    stage[0, 0:LANES, :] = jnp.zeros((LANES, D_MODEL), F32)
    gate_copy = pltpu.make_async_copy(
        w_hbm.at[GATE_OFF:GATE_OFF + N_GATES, :], stage.at[0, 0:N_GATES, :], sem.at[0])
    gate_copy.start()
    gate_copy.wait()
    wt_ref[MAIN_W:GATE_END, :] = stage[0, 0:LANES, :].astype(BF16)


def _inproj_kernel(layer, blocks_per_seq, x_ref, xs_ref, w_hbm, convw_ref, convb_ref, clng_ref,
                   clnb_ref, pm_ref, pc_ref, pa_ref, g_ref, conv_ref, hs_ref,
                   wt_ref, stage, sem, xbuf, abuf, sbuf, ybuf, zbuf):
    i = pl.program_id(0)
    lrow = slice(layer, layer + 1)

    n_sample = xs_ref.shape[0]

    @pl.when(i % blocks_per_seq == 0)
    def _():
        abuf[0:CONV_HALO, :] = jnp.zeros((CONV_HALO, C_WIDTH), F32)

    xbuf[0:TM_PROJ, :] = x_ref[...].astype(BF16)

    def project_block(n_extra):
        n_rows = TM_PROJ + n_extra

        def proj(c0, c1):
            full = _dot_nt(xbuf[0:n_rows, :], wt_ref[c0:c1, :])
            if n_extra:
                hs_ref[:, c0:c1] = full[TM_PROJ:n_rows, :]
                return full[0:TM_PROJ, :]
            return full

        for c in range(0, 3 * M_WIDTH, QKV_CHUNK):
            pm_ref[:, c:c + QKV_CHUNK] = proj(c, c + QKV_CHUNK).astype(BF16)
        for h in range(M_HEADS):
            o = proj(3 * M_WIDTH + h * M_DK, 3 * M_WIDTH + (h + 1) * M_DK)
            z = proj(4 * M_WIDTH + h * M_DK, 4 * M_WIDTH + (h + 1) * M_DK)
            pm_ref[:, 3 * M_WIDTH + h * M_DK:3 * M_WIDTH + (h + 1) * M_DK] = (
                _sigmoid(o) * _silu(z)).astype(BF16)

        cu = proj(HM_W, HM_W + C_WIDTH)
        cg = proj(HM_W + C_WIDTH, HM_W + 2 * C_WIDTH)
        abuf[CONV_HALO:CONV_HALO + TM_PROJ, :] = cu * _sigmoid(cg)
        zbuf[...] = _silu(proj(HM_W + 2 * C_WIDTH, HM_W + 3 * C_WIDTH))
        for p in range(1, SUBLANES):
            sbuf[p - 1] = abuf[p:p + SHIFTED_ROWS, :]
        tap0 = CONV_HALO - CONV_BUF
        for r in range(0, TM_PROJ, 128):
            for c in range(0, C_WIDTH, LANES):
                acc = jnp.zeros((128, LANES), F32)
                for t in range(CONV_K):
                    p = (tap0 + t) % SUBLANES
                    base = r + tap0 + t - p
                    src = abuf if p == 0 else sbuf.at[p - 1]
                    acc = acc + (src[base:base + 128, c:c + LANES]
                                 * convw_ref[layer, t:t + 1, c:c + LANES])
                ybuf[r:r + 128, c:c + LANES] = acc + convb_ref[lrow, c:c + LANES]
        conv_ref[0] = abuf[CONV_HALO + TM_PROJ - CONV_BUF:CONV_HALO + TM_PROJ, :]
        abuf[0:CONV_HALO, :] = abuf[TM_PROJ:TM_PROJ + CONV_HALO, :]
        yc = _layer_norm(ybuf[...], clng_ref[lrow, :], clnb_ref[lrow, :])
        pc_ref[...] = (_silu(yc) * zbuf[...]).astype(BF16)

        base = HM_W + HC_W
        pa_ref[:, 0:PA_QKV] = proj(base, base + PA_QKV).astype(BF16)
        pa_ref[:, PA_QKV:HA_W] = _silu(proj(base + PA_QKV, base + HA_W)).astype(BF16)
        g_ref[...] = proj(MAIN_W, GATE_END)

    @pl.when(i == 0)
    def _():
        _stage_w_in(w_hbm.at[layer], wt_ref, stage, sem)
        xbuf[TM_PROJ:TM_PROJ + n_sample, :] = xs_ref[...].astype(BF16)
        project_block(n_sample)

    @pl.when(i > 0)
    def _():
        project_block(0)


def _inproj(x2d, xs, w_in_t, prm, layer, batch):
    m = x2d.shape[0]
    ns = xs.shape[0]
    nblk = m // TM_PROJ
    row = lambda i: (i, 0)
    once = lambda i: (0, 0)
    params = [prm["convw"], prm["convb"], prm["clng"], prm["clnb"]]
    return pl.pallas_call(
        functools.partial(_inproj_kernel, layer, nblk // batch),
        grid=(nblk,),
        in_specs=[
            pl.BlockSpec((TM_PROJ, D_MODEL), row),
            pl.BlockSpec((ns, D_MODEL), once),
            pl.BlockSpec(memory_space=pl.ANY),
        ] + [_whole(a) for a in params],
        out_specs=[
            pl.BlockSpec((TM_PROJ, PM_W), row),
            pl.BlockSpec((TM_PROJ, C_WIDTH), row),
            pl.BlockSpec((TM_PROJ, HA_W), row),
            pl.BlockSpec((TM_PROJ, LANES), row),
            pl.BlockSpec((1, CONV_BUF, C_WIDTH), lambda i: (i // (nblk // batch), 0, 0)),
            pl.BlockSpec((ns, PROJ_W), once),
        ],
        out_shape=[
            jax.ShapeDtypeStruct((m, PM_W), BF16),
            jax.ShapeDtypeStruct((m, C_WIDTH), BF16),
            jax.ShapeDtypeStruct((m, HA_W), BF16),
            jax.ShapeDtypeStruct((m, LANES), F32),
            jax.ShapeDtypeStruct((batch, CONV_BUF, C_WIDTH), F32),
            jax.ShapeDtypeStruct((ns, PROJ_W), F32),
        ],
        scratch_shapes=[
            pltpu.VMEM((PROJ_W, D_MODEL), BF16),
            pltpu.VMEM((W_SLOTS, W_CHUNK, D_MODEL), F32),
            pltpu.SemaphoreType.DMA((W_SLOTS,)),
            pltpu.VMEM((TM_PROJ + ns, D_MODEL), BF16),
            pltpu.VMEM((CONV_HALO + TM_PROJ, C_WIDTH), F32),
            pltpu.VMEM((SUBLANES - 1, SHIFTED_ROWS, C_WIDTH), F32),
            pltpu.VMEM((TM_PROJ, C_WIDTH), F32),
            pltpu.VMEM((TM_PROJ, C_WIDTH), F32),
        ],
        compiler_params=pltpu.CompilerParams(
            dimension_semantics=("arbitrary",), vmem_limit_bytes=INPROJ_VMEM_LIMIT),
        name="inproj_prompt",
    )(x2d, xs, w_in_t, *params)


def _cumsum_rows(x):
    n = x.shape[0]
    row = lax.broadcasted_iota(jnp.int32, x.shape, 0)
    s = 1
    while s < n:
        x = x + jnp.where(row >= s, pltpu.roll(x, s, axis=0), 0.0)
        s *= 2
    return x


def _mixer_kernel(layer, sinks_ref, hm_ref, pc_ref, ha_ref, g_ref, x_ref, wout_ref,
                  gbias_ref, mng_ref, lng_ref, lnb_ref,
                  y_ref, c_ref, n_ref, m_ref, k_ref, v_ref,
                  kprev, vprev, merged):
    j = pl.program_id(1)
    lrow = slice(layer, layer + 1)

    @pl.when(j == 0)
    def _():
        c_ref[...] = jnp.zeros_like(c_ref)
        n_ref[...] = jnp.zeros_like(n_ref)
        m_ref[...] = jnp.zeros_like(m_ref)
        kprev[...] = jnp.zeros_like(kprev)
        vprev[...] = jnp.zeros_like(vprev)

    gb = g_ref[...] + gbias_ref[lrow, :]
    bsum = _cumsum_rows(_log_sigmoid(gb))
    bsum = pltpu.roll(bsum, LANES - M_HEADS, axis=1)
    a_all = gb - bsum
    a_t = a_all.T
    m_prev = m_ref[0]
    row_i = lax.broadcasted_iota(jnp.int32, (TB, TB), 0)
    col_i = lax.broadcasted_iota(jnp.int32, (TB, TB), 1)
    causal = col_i <= row_i
    lane_i = lax.broadcasted_iota(jnp.int32, (1, LANES), 1)
    m_new = m_prev

    heads = range(M_HEADS)
    cs = [slice(h * M_DK, (h + 1) * M_DK) for h in heads]
    q_of = lambda h: hm_ref[:, cs[h]]
    k_of = lambda h: hm_ref[:, M_WIDTH + h * M_DK:M_WIDTH + (h + 1) * M_DK]
    v_of = lambda h: hm_ref[:, 2 * M_WIDTH + h * M_DK:2 * M_WIDTH + (h + 1) * M_DK]
    m0 = [m_prev[:, h:h + 1] for h in heads]
    n0 = [n_ref[0, h:h + 1, :] for h in heads]
    logit = [jnp.where(causal, a_t[h:h + 1, :], -jnp.inf) for h in heads]
    mx = [jnp.maximum(jnp.max(logit[h], axis=1, keepdims=True), m0[h]) for h in heads]
    w0 = [jnp.exp(m0[h] - mx[h]) for h in heads]
    s = [_dot_nt(q_of(h), k_of(h)) * (jnp.exp(logit[h] - mx[h]) * K_SCALE) for h in heads]
    num = [_dot(s[h].astype(BF16), v_of(h)) + w0[h] * _dot(q_of(h), c_ref[0, h].astype(BF16))
           for h in heads]
    den = [jnp.sum(s[h], axis=1, keepdims=True)
           + w0[h] * jnp.sum(q_of(h).astype(F32) * n0[h], axis=1, keepdims=True) for h in heads]
    hh = [num[h] / jnp.maximum(jnp.abs(den[h]), jnp.exp(-(bsum[:, h:h + 1] + mx[h])))
          for h in heads]
    mu = [jnp.mean(hh[h], axis=1, keepdims=True) for h in heads]
    hc = [hh[h] - mu[h] for h in heads]
    var = [jnp.mean(hc[h] * hc[h], axis=1, keepdims=True) for h in heads]
    for h in heads:
        hn = hc[h] * lax.rsqrt(var[h] + LN_EPS) * mng_ref[lrow, cs[h]]
        gate = hm_ref[:, 3 * M_WIDTH + h * M_DK:3 * M_WIDTH + (h + 1) * M_DK].astype(F32)
        merged[:, cs[h]] = (hn * gate).astype(BF16)
    for h in heads:
        mx_end = mx[h][TB - 1:TB, :]
        g_col = jnp.exp(a_all[:, h:h + 1] - mx_end) * K_SCALE
        g0 = jnp.exp(m0[h] - mx_end)
        kg = k_of(h).astype(F32) * g_col
        c_ref[0, h] = g0 * c_ref[0, h] + _dot_tn(kg.astype(BF16), v_of(h))
        n_ref[0, h:h + 1, :] = g0 * n0[h] + jnp.sum(kg, axis=0, keepdims=True)
        m_end = bsum[TB - 1:TB, h:h + 1] + mx_end
        m_new = jnp.where(lane_i == h, m_end, m_new)
    m_ref[0] = m_new

    merged[:, M_WIDTH:M_WIDTH + C_WIDTH] = pc_ref[...]

    arow = lax.broadcasted_iota(jnp.int32, (A_BLK, 2 * A_BLK), 0)
    acol = lax.broadcasted_iota(jnp.int32, (A_BLK, 2 * A_BLK), 1)
    rel = A_BLK + arow - acol
    band = (rel >= 0) & (rel <= WINDOW)
    n_sub = TB // A_BLK
    kcat, vext, mask = [], [], []
    ones = jnp.ones((2 * A_BLK, HEAD_DIM), BF16)
    for sub in range(n_sub):
        r0 = sub * A_BLK
        if sub == 0:
            kcat.append(jnp.concatenate(
                [kprev[...], ha_ref[0:A_BLK, HA_K:HA_K + KV_WIDTH]], axis=0))
            vcat = jnp.concatenate([vprev[...], ha_ref[0:A_BLK, HA_V:HA_V + KV_WIDTH]], axis=0)
            mask.append(band & (acol >= jnp.where(j > 0, 0, A_BLK)))
        else:
            kcat.append(ha_ref[r0 - A_BLK:r0 + A_BLK, HA_K:HA_K + KV_WIDTH])
            vcat = ha_ref[r0 - A_BLK:r0 + A_BLK, HA_V:HA_V + KV_WIDTH]
            mask.append(band)
        vext.append([jnp.concatenate([vcat[:, kvh * HEAD_DIM:(kvh + 1) * HEAD_DIM], ones], axis=1)
                     for kvh in range(KV_HEADS)])
    units = [(sub, hq) for sub in range(n_sub) for hq in range(A_HEADS)]
    sink = {hq: sinks_ref[layer, hq] for hq in range(A_HEADS)}
    sc = {}
    for sub, hq in units:
        kvh = hq // GROUP
        qh = ha_ref[sub * A_BLK:(sub + 1) * A_BLK, hq * HEAD_DIM:(hq + 1) * HEAD_DIM]
        kh = kcat[sub][:, kvh * HEAD_DIM:(kvh + 1) * HEAD_DIM]
        sc[sub, hq] = jnp.where(mask[sub], _dot_nt(qh, kh) * A_SCALE, -jnp.inf)
    smax = {u: jnp.maximum(jnp.max(sc[u], axis=1, keepdims=True), sink[u[1]]) for u in units}
    pv = {u: _dot(jnp.exp(sc[u] - smax[u]).astype(BF16), vext[u[0]][u[1] // GROUP])
          for u in units}
    outs = {u: pv[u][:, 0:HEAD_DIM]
            / (pv[u][:, HEAD_DIM:HEAD_DIM + 1] + jnp.exp(sink[u[1]] - smax[u]))
            for u in units}
    for sub in range(n_sub):
        r0 = sub * A_BLK
        for hp in range(A_HEADS // 2):
            ao = jnp.concatenate([outs[sub, 2 * hp], outs[sub, 2 * hp + 1]], axis=1)
            gate = ha_ref[r0:r0 + A_BLK, HA_Z + hp * LANES:HA_Z + (hp + 1) * LANES].astype(F32)
            col = M_WIDTH + C_WIDTH + hp * LANES
            merged[r0:r0 + A_BLK, col:col + LANES] = (ao * gate).astype(BF16)
    kprev[...] = ha_ref[TB - A_BLK:TB, HA_K:HA_K + KV_WIDTH]
    vprev[...] = ha_ref[TB - A_BLK:TB, HA_V:HA_V + KV_WIDTH]
    k_ref[0] = ha_ref[TB - WINDOW:TB, HA_K:HA_K + KV_WIDTH].astype(F32)
    v_ref[0] = ha_ref[TB - WINDOW:TB, HA_V:HA_V + KV_WIDTH].astype(F32)

    y = _dot(merged[...], wout_ref[...])
    y_ref[...] = _layer_norm(ALPHA * x_ref[...] + y, lng_ref[lrow, :], lnb_ref[lrow, :])


def _mixer(layer, sinks, pm, pc, pa, g, x2d, wout, prm, batch, seq):
    nb = seq // TB
    row = lambda b, j: (b * nb + j, 0)
    per_b3 = lambda b, j: (b, 0, 0)
    state_shapes = [
        (batch, M_HEADS, M_DK, M_DK), (batch, M_HEADS, M_DK), (batch, 1, LANES),
        (batch, WINDOW, KV_WIDTH), (batch, WINDOW, KV_WIDTH)]
    params = [prm["gbias"], prm["mng"], prm["lng"], prm["lnb"]]
    return pl.pallas_call(
        functools.partial(_mixer_kernel, layer),
        grid=(batch, nb),
        in_specs=[
            pl.BlockSpec(memory_space=pltpu.SMEM),
            pl.BlockSpec((TB, PM_W), row),
            pl.BlockSpec((TB, C_WIDTH), row),
            pl.BlockSpec((TB, HA_W), row),
            pl.BlockSpec((TB, LANES), row),
            pl.BlockSpec((TB, D_MODEL), row),
            pl.BlockSpec((None, D_MODEL, D_MODEL), lambda b, j: (layer, 0, 0)),
        ] + [_whole(a) for a in params],
        out_specs=[
            pl.BlockSpec((TB, D_MODEL), row),
            pl.BlockSpec((1, M_HEADS, M_DK, M_DK), lambda b, j: (b, 0, 0, 0)),
            pl.BlockSpec((1, M_HEADS, M_DK), per_b3),
            pl.BlockSpec((1, 1, LANES), per_b3),
            pl.BlockSpec((1, WINDOW, KV_WIDTH), per_b3),
            pl.BlockSpec((1, WINDOW, KV_WIDTH), per_b3),
        ],
        out_shape=[jax.ShapeDtypeStruct((batch * seq, D_MODEL), F32)]
        + [jax.ShapeDtypeStruct(s, F32) for s in state_shapes],
        scratch_shapes=[
            pltpu.VMEM((A_BLK, KV_WIDTH), BF16),
            pltpu.VMEM((A_BLK, KV_WIDTH), BF16),
            pltpu.VMEM((TB, D_MODEL), BF16),
        ],
        compiler_params=pltpu.CompilerParams(
            dimension_semantics=("arbitrary", "arbitrary"), vmem_limit_bytes=VMEM_LIMIT),
        name="mixer_prompt",
    )(sinks, pm, pc, pa, g, x2d, wout, *params)


def _sample_gates(hs_ref, gbias_row, m_ref):
    gb = hs_ref[:, MAIN_W:GATE_END] + gbias_row
    lf = pltpu.roll(_log_sigmoid(gb), LANES - M_HEADS, axis=1)
    m0 = m_ref[...]
    m_new = jnp.maximum(lf + m0, gb)
    wgt = jnp.exp(gb - m_new)
    w0 = jnp.exp(lf + m0 - m_new)
    return m_new, wgt, w0


def _sample_mixer_kernel(layer, hs_ref, c_ref, n_ref, m_ref, conv_ref, ckt_ref, cvt_ref,
                         gbias_ref, mng_ref, convw_ref, convb_ref, clng_ref, clnb_ref, sink_ref,
                         merged_ref, nout_ref, mout_ref, convout_ref, kout_ref, vout_ref):
    lrow = slice(layer, layer + 1)
    m_new, wgt, w0 = _sample_gates(hs_ref, gbias_ref[lrow, :], m_ref)
    einv = jnp.exp(-m_new)
    mout_ref[...] = m_new
    for h in range(M_HEADS):
        cs = slice(h * M_DK, (h + 1) * M_DK)
        q = hs_ref[:, cs]
        k = hs_ref[:, M_WIDTH + h * M_DK:M_WIDTH + (h + 1) * M_DK] * K_SCALE
        v = hs_ref[:, 2 * M_WIDTH + h * M_DK:2 * M_WIDTH + (h + 1) * M_DK]
        q_t = q.T
        wh = wgt[:, h:h + 1]
        w0h = w0[:, h:h + 1]
        n0 = n_ref[:, h, :]
        s = jnp.sum(q * k, axis=1, keepdims=True) * wh
        qc = jnp.concatenate(
            [jnp.sum(q_t[:, b:b + 1] * c_ref[b, h], axis=0, keepdims=True) for b in range(SB)],
            axis=0)
        num = s * v + w0h * qc
        den = s + w0h * jnp.sum(q * n0, axis=1, keepdims=True)
        hh = num / jnp.maximum(jnp.abs(den), einv[:, h:h + 1])
        nout_ref[:, h, :] = w0h * n0 + wh * k
        mu = jnp.mean(hh, axis=1, keepdims=True)
        hc = hh - mu
        var = jnp.mean(hc * hc, axis=1, keepdims=True)
        hn = hc * lax.rsqrt(var + LN_EPS) * mng_ref[lrow, cs]
        o = hs_ref[:, 3 * M_WIDTH + h * M_DK:3 * M_WIDTH + (h + 1) * M_DK]
        z = hs_ref[:, 4 * M_WIDTH + h * M_DK:4 * M_WIDTH + (h + 1) * M_DK]
        merged_ref[:, cs] = hn * _sigmoid(o) * _silu(z)

    cu = hs_ref[:, HM_W:HM_W + C_WIDTH]
    cg = hs_ref[:, HM_W + C_WIDTH:HM_W + 2 * C_WIDTH]
    cz = hs_ref[:, HM_W + 2 * C_WIDTH:HM_W + 3 * C_WIDTH]
    a = cu * _sigmoid(cg)
    yc = a * convw_ref[layer, CONV_BUF:CONV_K, :] + convb_ref[lrow, :]
    for t in range(CONV_BUF):
        yc = yc + conv_ref[t] * convw_ref[layer, t:t + 1, :]
    for t in range(CONV_BUF - 1):
        convout_ref[t] = conv_ref[t + 1]
    convout_ref[CONV_BUF - 1] = a
    yc = _layer_norm(yc, clng_ref[lrow, :], clnb_ref[lrow, :])
    merged_ref[:, M_WIDTH:M_WIDTH + C_WIDTH] = _silu(yc) * _silu(cz)

    ab = HM_W + HC_W
    lane = lax.broadcasted_iota(jnp.int32, (1, LANES), 1)
    rows_b = lax.broadcasted_iota(jnp.int32, (A_HEADS * SB, LANES), 0) % SB
    wcol = lax.broadcasted_iota(jnp.int32, (KV_WIDTH, WINDOW), 1)
    knew = hs_ref[:, ab + HA_K:ab + HA_K + KV_WIDTH]
    vnew = hs_ref[:, ab + HA_V:ab + HA_V + KV_WIDTH]
    q_parts, scn_parts, sink_parts = [], [], []
    for hq in range(A_HEADS):
        kvh = hq // GROUP
        t, u = hq // 2, hq % 2
        tile = hs_ref[:, ab + t * LANES:ab + (t + 1) * LANES]
        if u != kvh:
            tile = pltpu.roll(tile, HEAD_DIM, axis=1)
        qh = jnp.where((lane // HEAD_DIM) == kvh, tile, 0.0)
        q_parts.append(qh)
        scn_parts.append(jnp.sum(qh * knew, axis=1, keepdims=True))
        sink_parts.append(jnp.broadcast_to(sink_ref[lrow, hq:hq + 1], (SB, 1)))
    q_all = jnp.concatenate(q_parts, axis=0).astype(BF16)
    sc_new = jnp.concatenate(scn_parts, axis=0) * A_SCALE
    sink_all = jnp.concatenate(sink_parts, axis=0)
    sc = jnp.zeros((A_HEADS * SB, WINDOW), F32)
    for b in range(SB):
        sc = jnp.where(rows_b == b, _dot(q_all, ckt_ref[b].astype(BF16)), sc)
    sc = sc * A_SCALE
    smax = jnp.maximum(jnp.maximum(jnp.max(sc, axis=1, keepdims=True), sc_new), sink_all)
    e = jnp.exp(sc - smax)
    e_new = jnp.exp(sc_new - smax)
    denom = jnp.sum(e, axis=1, keepdims=True) + e_new + jnp.exp(sink_all - smax)
    p = (e / denom).astype(BF16)
    ob = (e_new / denom) * jnp.concatenate([vnew] * A_HEADS, axis=0)
    for b in range(SB):
        ob = ob + jnp.where(rows_b == b, _dot_nt(p, cvt_ref[b].astype(BF16)), 0.0)
    tiles = []
    for t in range(A_HEADS // 2):
        halves = []
        for u in range(2):
            hq = 2 * t + u
            x = ob[hq * SB:(hq + 1) * SB, :]
            if u != hq // GROUP:
                x = pltpu.roll(x, HEAD_DIM, axis=1)
            halves.append(x)
        tiles.append(jnp.where(lane < HEAD_DIM, halves[0], halves[1]))
    az = hs_ref[:, ab + HA_Z:ab + HA_Z + A_WIDTH]
    merged_ref[:, M_WIDTH + C_WIDTH:D_MODEL] = jnp.concatenate(tiles, axis=1) * _silu(az)
    knew_t = knew.T
    vnew_t = vnew.T
    for b in range(SB):
        kout_ref[b] = jnp.where(wcol == WINDOW - 1, knew_t[:, b:b + 1],
                                pltpu.roll(ckt_ref[b], WINDOW - 1, axis=1))
        vout_ref[b] = jnp.where(wcol == WINDOW - 1, vnew_t[:, b:b + 1],
                                pltpu.roll(cvt_ref[b], WINDOW - 1, axis=1))


def _sample_mixer(layer, hs, c, n, m, conv, ckt, cvt, prm):
    nb = hs.shape[0]
    row2 = lambda i: (i, 0)

    def state_specs(lead, pre):
        return [
            pl.BlockSpec(lead + (SB, M_HEADS, M_DK), lambda i: pre + (i, 0, 0)),
            pl.BlockSpec(lead + (SB, LANES), lambda i: pre + (i, 0)),
            pl.BlockSpec(lead + (CONV_BUF, SB, C_WIDTH), lambda i: pre + (0, i, 0)),
            pl.BlockSpec(lead + (SB, KV_WIDTH, WINDOW), lambda i: pre + (i, 0, 0)),
            pl.BlockSpec(lead + (SB, KV_WIDTH, WINDOW), lambda i: pre + (i, 0, 0)),
        ]

    c_spec = pl.BlockSpec((None, SB, M_HEADS, M_DK, M_DK), lambda i: (layer, i, 0, 0, 0))
    state_shapes = [
        (nb, M_HEADS, M_DK), (nb, LANES),
        (CONV_BUF, nb, C_WIDTH), (nb, KV_WIDTH, WINDOW), (nb, KV_WIDTH, WINDOW)]
    params = [prm["gbias"], prm["mng"], prm["convw"], prm["convb"], prm["clng"], prm["clnb"],
              prm["sink_row"]]
    return pl.pallas_call(
        functools.partial(_sample_mixer_kernel, layer),
        grid=(nb // SB,),
        in_specs=([pl.BlockSpec((SB, PROJ_W), row2), c_spec] + state_specs((None,), (layer,))
                  + [_whole(a) for a in params]),
        out_specs=[pl.BlockSpec((SB, D_MODEL), row2)] + state_specs((), ()),
        out_shape=[jax.ShapeDtypeStruct((nb, D_MODEL), F32)]
        + [jax.ShapeDtypeStruct(s, F32) for s in state_shapes],
        compiler_params=pltpu.CompilerParams(
            dimension_semantics=("arbitrary",), vmem_limit_bytes=VMEM_LIMIT),
        name="mixer_sample",
    )(hs, c, n, m, conv, ckt, cvt, *params)


def _update_c_kernel(hs_ref, c_ref, m_ref, gbias_ref, cout_ref):
    layer = pl.program_id(0)
    _, wgt, w0 = _sample_gates(hs_ref, gbias_ref[pl.ds(layer, 1), :], m_ref)
    for h in range(M_HEADS):
        k_t = (hs_ref[:, M_WIDTH + h * M_DK:M_WIDTH + (h + 1) * M_DK] * K_SCALE).T
        v = hs_ref[:, 2 * M_WIDTH + h * M_DK:2 * M_WIDTH + (h + 1) * M_DK]
        gv = wgt[:, h:h + 1] * v
        w0h = w0[:, h:h + 1]
        for b in range(SB):
            cout_ref[b, h] = w0h[b:b + 1, :] * c_ref[b, h] + k_t[:, b:b + 1] * gv[b:b + 1, :]


def _update_c(hs_all, c, m, prm):
    depth, nb = hs_all.shape[:2]
    per = lambda l, i: (l, i, 0)
    c_spec = pl.BlockSpec((None, SB, M_HEADS, M_DK, M_DK), lambda l, i: (l, i, 0, 0, 0))
    return pl.pallas_call(
        _update_c_kernel,
        grid=(depth, nb // SB),
        in_specs=[pl.BlockSpec((None, SB, PROJ_W), per), c_spec,
                  pl.BlockSpec((None, SB, LANES), per), _whole(prm["gbias"])],
        out_specs=c_spec,
        out_shape=jax.ShapeDtypeStruct(c.shape, F32),
        compiler_params=pltpu.CompilerParams(
            dimension_semantics=("arbitrary", "arbitrary"), vmem_limit_bytes=VMEM_LIMIT),
        name="update_matrix_memory",
    )(hs_all, c, m, prm["gbias"])


def _outproj_sample_kernel(layer, mg_ref, x_ref, wout_ref, lng_ref, lnb_ref, y_ref):
    lrow = slice(layer, layer + 1)
    y = _dot(mg_ref[...].astype(BF16), wout_ref[...])
    y_ref[...] = _layer_norm(ALPHA * x_ref[...] + y, lng_ref[lrow, :], lnb_ref[lrow, :])


def _outproj_sample(mg, xs, wout, prm, layer):
    nb = xs.shape[0]
    full = lambda shape: pl.BlockSpec(shape, lambda i: (0, 0))
    return pl.pallas_call(
        functools.partial(_outproj_sample_kernel, layer),
        grid=(1,),
        in_specs=[full((nb, D_MODEL)), full((nb, D_MODEL)),
                  pl.BlockSpec((None, D_MODEL, D_MODEL), lambda i: (layer, 0, 0)),
                  _whole(prm["lng"]), _whole(prm["lnb"])],
        out_specs=full((nb, D_MODEL)),
        out_shape=jax.ShapeDtypeStruct((nb, D_MODEL), F32),
        compiler_params=pltpu.CompilerParams(
            dimension_semantics=("arbitrary",), vmem_limit_bytes=VMEM_LIMIT),
        name="outproj_sample",
    )(mg, xs, wout, prm["lng"], prm["lnb"])


def _cast_kernel(w_ref, o_ref):
    o_ref[...] = w_ref[...].astype(BF16)


def _cast_w_out(w_out):
    depth, d, n = w_out.shape
    rows = 512
    return pl.pallas_call(
        _cast_kernel,
        grid=(depth, d // rows),
        in_specs=[pl.BlockSpec((None, rows, n), lambda l, i: (l, i, 0))],
        out_specs=pl.BlockSpec((None, rows, n), lambda l, i: (l, i, 0)),
        out_shape=jax.ShapeDtypeStruct((depth, d, n), BF16),
        compiler_params=pltpu.CompilerParams(dimension_semantics=("arbitrary", "arbitrary")),
        name="cast_w_out",
    )(w_out)


def _lane_rows(*mats):
    v = jnp.concatenate([a.astype(F32) for a in mats], axis=1)
    return jnp.pad(v, ((0, 0), (0, LANES - v.shape[1])))


def _cache_rows_by_window(cache):
    depth, nb = cache.shape[:2]
    return jnp.transpose(cache, (0, 1, 3, 4, 2)).reshape(depth, nb, KV_WIDTH, WINDOW)


def _cache_window_major(cache_t):
    depth, nb = cache_t.shape[:2]
    return jnp.transpose(cache_t.reshape(depth, nb, KV_HEADS, HEAD_DIM, WINDOW), (0, 1, 4, 2, 3))


def kernel(x_prompt, x_sample, state_C, state_n, state_m, state_conv, cache_k, cache_v, w_in, w_out,
           b_igate, b_fgate, m_norm_g, conv_w, conv_b, conv_ln_g, conv_ln_b, sinks, ln_g, ln_b):
    batch, seq, _ = x_prompt.shape
    nb = x_sample.shape[0]
    xp = x_prompt.reshape(batch * seq, D_MODEL)
    xs = x_sample.reshape(nb, D_MODEL)
    wt = jnp.swapaxes(w_in, 1, 2)
    wout = _cast_w_out(w_out)
    prm = {
        "gbias": _lane_rows(b_igate, b_fgate),
        "sink_row": _lane_rows(sinks),
        "mng": m_norm_g, "convw": conv_w, "convb": conv_b, "clng": conv_ln_g, "clnb": conv_ln_b,
        "lng": ln_g, "lnb": ln_b,
    }
    m_in = jnp.pad(state_m, ((0, 0), (0, 0), (0, LANES - M_HEADS)))
    sconv = jnp.swapaxes(state_conv, 1, 2)
    ckt = _cache_rows_by_window(cache_k)
    cvt = _cache_rows_by_window(cache_v)
    st_p = []
    st_s = []
    hs_all = []
    for l in range(DEPTH):
        pm, pc, pa, g, conv_p, hs = _inproj(xp, xs, wt, prm, l, batch)
        xp, c_p, n_p, m_p, k_p, v_p = _mixer(l, sinks, pm, pc, pa, g, xp, wout, prm, batch, seq)
        st_p.append((c_p, n_p, m_p, conv_p, k_p, v_p))
        mg, *states = _sample_mixer(l, hs, state_C, state_n, m_in, sconv, ckt, cvt, prm)
        st_s.append(states)
        hs_all.append(hs)
        xs = _outproj_sample(mg, xs, wout, prm, l)
    c_s = _update_c(jnp.stack(hs_all, axis=0), state_C, m_in, prm)

    def stacked(per_layer):
        return [jnp.stack(a, axis=0) for a in zip(*per_layer)]

    c_p, n_p, m_p, conv_p, k_p, v_p = stacked(st_p)
    n_s, m_s, conv_s, k_s, v_s = stacked(st_s)
    return (xp.reshape(batch, seq, D_MODEL), xs.reshape(nb, 1, D_MODEL),
            c_p, n_p, m_p.reshape(DEPTH, batch, LANES)[:, :, :M_HEADS], conv_p,
            k_p.reshape(DEPTH, batch, WINDOW, KV_HEADS, HEAD_DIM),
            v_p.reshape(DEPTH, batch, WINDOW, KV_HEADS, HEAD_DIM),
            c_s, n_s, m_s[:, :, :M_HEADS], jnp.swapaxes(conv_s, 1, 2),
            _cache_window_major(k_s), _cache_window_major(v_s))
```

```python
import functools

import jax
import jax.numpy as jnp
from jax import lax
from jax.experimental import pallas as pl
from jax.experimental.pallas import tpu as pltpu

F32 = jnp.float32
BF16 = jnp.bfloat16

D_MODEL = 2048
DEPTH = 2
M_WIDTH = 1024
M_HEADS = 4
M_DK = 256
C_WIDTH = 512
CONV_K = 31
CONV_BUF = CONV_K - 1
A_WIDTH = 512
HEAD_DIM = 64
A_HEADS = 8
KV_HEADS = 2
GROUP = A_HEADS // KV_HEADS
KV_WIDTH = KV_HEADS * HEAD_DIM
WINDOW = 128
ALPHA = (2 * DEPTH) ** 0.25
LN_EPS = 1e-5
K_SCALE = M_DK ** -0.5
A_SCALE = HEAD_DIM ** -0.5

LANES = 128
SUBLANES = 8

HM_W = 5 * M_WIDTH
HC_W = 3 * C_WIDTH
HA_W = 2 * A_WIDTH + 2 * KV_WIDTH
MAIN_W = HM_W + HC_W + HA_W
GATE_END = MAIN_W + LANES
PROJ_W = GATE_END
GATE_OFF = 4 * M_WIDTH
N_GATES = 2 * M_HEADS
W_CHUNK = 256
W_SLOTS = 4
PLAIN_CHUNKS = GATE_OFF // W_CHUNK
MAIN_CHUNKS = MAIN_W // W_CHUNK
HA_K = A_WIDTH
HA_V = A_WIDTH + KV_WIDTH
HA_Z = A_WIDTH + 2 * KV_WIDTH

TM_PROJ = 256
TB = 256
A_BLK = 128
CONV_HALO = 32
SB = 8
VMEM_LIMIT = 56 * 1024 * 1024
INPROJ_VMEM_LIMIT = 60 * 1024 * 1024


def _sigmoid(x):
    return 1.0 / (1.0 + jnp.exp(-x))


def _silu(x):
    return x * _sigmoid(x)


def _log_sigmoid(x):
    return jnp.minimum(x, 0.0) - jnp.log1p(jnp.exp(-jnp.abs(x)))


def _layer_norm(x, g, b):
    mu = jnp.mean(x, axis=-1, keepdims=True)
    xc = x - mu
    var = jnp.mean(xc * xc, axis=-1, keepdims=True)
    return xc * lax.rsqrt(var + LN_EPS) * g + b


def _dot(a, b):
    return jnp.dot(a, b, preferred_element_type=F32)


def _dot_nt(a, b):
    return lax.dot_general(a, b, (((1,), (1,)), ((), ())), preferred_element_type=F32)


def _dot_tn(a, b):
    return lax.dot_general(a, b, (((0,), (0,)), ((), ())), preferred_element_type=F32)


def _whole(arr):
    zeros = (0,) * arr.ndim
    return pl.BlockSpec(arr.shape, lambda *_: zeros)


PM_W = 4 * M_WIDTH
PA_QKV = A_WIDTH + 2 * KV_WIDTH
QKV_CHUNK = 512
SHIFTED_ROWS = CONV_HALO + TM_PROJ - SUBLANES


def _stage_rows(src_hbm, dst_ref, stage, sem, n_chunks, src_row):
    def chunk_copy(c, slot):
        return pltpu.make_async_copy(
            src_hbm.at[pl.ds(pl.multiple_of(src_row(c), SUBLANES), W_CHUNK), :],
            stage.at[slot], sem.at[slot])

    for c in range(W_SLOTS - 1):
        chunk_copy(c, c).start()

    def body(c, carry):
        slot = c % W_SLOTS
        chunk_copy(c, slot).wait()
        ahead = c + W_SLOTS - 1

        @pl.when(ahead < n_chunks)
        def _():
            chunk_copy(ahead, ahead % W_SLOTS).start()

        dst_ref[pl.ds(pl.multiple_of(c * W_CHUNK, W_CHUNK), W_CHUNK), :] = stage[slot].astype(BF16)
        return carry

    lax.fori_loop(0, n_chunks, body, 0)


def _stage_w_in(w_hbm, wt_ref, stage, sem):
    _stage_rows(w_hbm, wt_ref, stage, sem, MAIN_CHUNKS,
                lambda c: jnp.where(c < PLAIN_CHUNKS, c * W_CHUNK, c * W_CHUNK + N_GATES))
    stage[0, 0:LANES, :] = jnp.zeros((LANES, D_MODEL), F32)
    gate_copy = pltpu.make_async_copy(
        w_hbm.at[GATE_OFF:GATE_OFF + N_GATES, :], stage.at[0, 0:N_GATES, :], sem.at[0])
    gate_copy.start()
    gate_copy.wait()
    wt_ref[MAIN_W:GATE_END, :] = stage[0, 0:LANES, :].astype(BF16)


def _inproj_kernel(layer, blocks_per_seq, x_ref, xs_ref, w_hbm, convw_ref, convb_ref, clng_ref,
                   clnb_ref, pm_ref, pc_ref, pa_ref, g_ref, conv_ref, hs_ref,
                   wt_ref, stage, sem, xbuf, abuf, sbuf, ybuf, zbuf):
    i = pl.program_id(0)
    lrow = slice(layer, layer + 1)

    n_sample = xs_ref.shape[0]

    @pl.when(i % blocks_per_seq == 0)
    def _():
        abuf[0:CONV_HALO, :] = jnp.zeros((CONV_HALO, C_WIDTH), F32)

    xbuf[0:TM_PROJ, :] = x_ref[...].astype(BF16)

    def project_block(n_extra):
        n_rows = TM_PROJ + n_extra

        def proj(c0, c1):
            full = _dot_nt(xbuf[0:n_rows, :], wt_ref[c0:c1, :])
            if n_extra:
                hs_ref[:, c0:c1] = full[TM_PROJ:n_rows, :]
                return full[0:TM_PROJ, :]
            return full

        for c in range(0, 3 * M_WIDTH, QKV_CHUNK):
            pm_ref[:, c:c + QKV_CHUNK] = proj(c, c + QKV_CHUNK).astype(BF16)
        for h in range(M_HEADS):
            o = proj(3 * M_WIDTH + h * M_DK, 3 * M_WIDTH + (h + 1) * M_DK)
            z = proj(4 * M_WIDTH + h * M_DK, 4 * M_WIDTH + (h + 1) * M_DK)
            pm_ref[:, 3 * M_WIDTH + h * M_DK:3 * M_WIDTH + (h + 1) * M_DK] = (
                _sigmoid(o) * _silu(z)).astype(BF16)

        cu = proj(HM_W, HM_W + C_WIDTH)
        cg = proj(HM_W + C_WIDTH, HM_W + 2 * C_WIDTH)
        abuf[CONV_HALO:CONV_HALO + TM_PROJ, :] = cu * _sigmoid(cg)
        zbuf[...] = _silu(proj(HM_W + 2 * C_WIDTH, HM_W + 3 * C_WIDTH))
        for p in range(1, SUBLANES):
            sbuf[p - 1] = abuf[p:p + SHIFTED_ROWS, :]
        tap0 = CONV_HALO - CONV_BUF
        for r in range(0, TM_PROJ, 128):
            for c in range(0, C_WIDTH, LANES):
                acc = jnp.zeros((128, LANES), F32)
                for t in range(CONV_K):
                    p = (tap0 + t) % SUBLANES
                    base = r + tap0 + t - p
                    src = abuf if p == 0 else sbuf.at[p - 1]
                    acc = acc + (src[base:base + 128, c:c + LANES]
                                 * convw_ref[layer, t:t + 1, c:c + LANES])
                ybuf[r:r + 128, c:c + LANES] = acc + convb_ref[lrow, c:c + LANES]
        conv_ref[0] = abuf[CONV_HALO + TM_PROJ - CONV_BUF:CONV_HALO + TM_PROJ, :]
        abuf[0:CONV_HALO, :] = abuf[TM_PROJ:TM_PROJ + CONV_HALO, :]
        yc = _layer_norm(ybuf[...], clng_ref[lrow, :], clnb_ref[lrow, :])
        pc_ref[...] = (_silu(yc) * zbuf[...]).astype(BF16)

        base = HM_W + HC_W
        pa_ref[:, 0:PA_QKV] = proj(base, base + PA_QKV).astype(BF16)
        pa_ref[:, PA_QKV:HA_W] = _silu(proj(base + PA_QKV, base + HA_W)).astype(BF16)
        g_ref[...] = proj(MAIN_W, GATE_END)

    @pl.when(i == 0)
    def _():
        _stage_w_in(w_hbm.at[layer], wt_ref, stage, sem)
        xbuf[TM_PROJ:TM_PROJ + n_sample, :] = xs_ref[...].astype(BF16)
        project_block(n_sample)

    @pl.when(i > 0)
    def _():
        project_block(0)


def _inproj(x2d, xs, w_in_t, prm, layer, batch):
    m = x2d.shape[0]
    ns = xs.shape[0]
    nblk = m // TM_PROJ
    row = lambda i: (i, 0)
    once = lambda i: (0, 0)
    params = [prm["convw"], prm["convb"], prm["clng"], prm["clnb"]]
    return pl.pallas_call(
        functools.partial(_inproj_kernel, layer, nblk // batch),
        grid=(nblk,),
        in_specs=[
            pl.BlockSpec((TM_PROJ, D_MODEL), row),
            pl.BlockSpec((ns, D_MODEL), once),
            pl.BlockSpec(memory_space=pl.ANY),
        ] + [_whole(a) for a in params],
        out_specs=[
            pl.BlockSpec((TM_PROJ, PM_W), row),
            pl.BlockSpec((TM_PROJ, C_WIDTH), row),
            pl.BlockSpec((TM_PROJ, HA_W), row),
            pl.BlockSpec((TM_PROJ, LANES), row),
            pl.BlockSpec((1, CONV_BUF, C_WIDTH), lambda i: (i // (nblk // batch), 0, 0)),
            pl.BlockSpec((ns, PROJ_W), once),
        ],
        out_shape=[
            jax.ShapeDtypeStruct((m, PM_W), BF16),
            jax.ShapeDtypeStruct((m, C_WIDTH), BF16),
            jax.ShapeDtypeStruct((m, HA_W), BF16),
            jax.ShapeDtypeStruct((m, LANES), F32),
            jax.ShapeDtypeStruct((batch, CONV_BUF, C_WIDTH), F32),
            jax.ShapeDtypeStruct((ns, PROJ_W), F32),
        ],
        scratch_shapes=[
            pltpu.VMEM((PROJ_W, D_MODEL), BF16),
            pltpu.VMEM((W_SLOTS, W_CHUNK, D_MODEL), F32),
            pltpu.SemaphoreType.DMA((W_SLOTS,)),
            pltpu.VMEM((TM_PROJ + ns, D_MODEL), BF16),
            pltpu.VMEM((CONV_HALO + TM_PROJ, C_WIDTH), F32),
            pltpu.VMEM((SUBLANES - 1, SHIFTED_ROWS, C_WIDTH), F32),
            pltpu.VMEM((TM_PROJ, C_WIDTH), F32),
            pltpu.VMEM((TM_PROJ, C_WIDTH), F32),
        ],
        compiler_params=pltpu.CompilerParams(
            dimension_semantics=("arbitrary",), vmem_limit_bytes=INPROJ_VMEM_LIMIT),
        name="inproj_prompt",
    )(x2d, xs, w_in_t, *params)


def _cumsum_rows(x):
    n = x.shape[0]
    row = lax.broadcasted_iota(jnp.int32, x.shape, 0)
    s = 1
    while s < n:
        x = x + jnp.where(row >= s, pltpu.roll(x, s, axis=0), 0.0)
        s *= 2
    return x


def _mixer_kernel(layer, n_blocks, blocks_per_seq, sinks_ref, hm_ref, pc_ref, ha_ref, g_ref,
                  xprev_ref, wout_ref, gbias_ref, mng_ref, lng_ref, lnb_ref,
                  y_ref, c_ref, n_ref, m_ref, k_ref, v_ref,
                  kprev, vprev, merged, merged_prev):
    step = pl.program_id(0)
    j = jnp.minimum(step, n_blocks - 1) % blocks_per_seq
    lrow = slice(layer, layer + 1)

    @pl.when(step == 0)
    def _():
        merged_prev[...] = jnp.zeros_like(merged_prev)

    @pl.when(j == 0)
    def _():
        c_ref[...] = jnp.zeros_like(c_ref)
        n_ref[...] = jnp.zeros_like(n_ref)
        m_ref[...] = jnp.zeros_like(m_ref)
        kprev[...] = jnp.zeros_like(kprev)
        vprev[...] = jnp.zeros_like(vprev)

    y = _dot(merged_prev[...], wout_ref[...])
    y_ref[...] = _layer_norm(ALPHA * xprev_ref[...] + y, lng_ref[lrow, :], lnb_ref[lrow, :])

    gb = g_ref[...] + gbias_ref[lrow, :]
    bsum = _cumsum_rows(_log_sigmoid(gb))
    bsum = pltpu.roll(bsum, LANES - M_HEADS, axis=1)
    a_all = gb - bsum
    a_t = a_all.T
    m_prev = m_ref[0]
    row_i = lax.broadcasted_iota(jnp.int32, (TB, TB), 0)
    col_i = lax.broadcasted_iota(jnp.int32, (TB, TB), 1)
    causal = col_i <= row_i
    lane_i = lax.broadcasted_iota(jnp.int32, (1, LANES), 1)
    m_new = m_prev

    heads = range(M_HEADS)
    cs = [slice(h * M_DK, (h + 1) * M_DK) for h in heads]
    q_of = lambda h: hm_ref[:, cs[h]]
    k_of = lambda h: hm_ref[:, M_WIDTH + h * M_DK:M_WIDTH + (h + 1) * M_DK]
    v_of = lambda h: hm_ref[:, 2 * M_WIDTH + h * M_DK:2 * M_WIDTH + (h + 1) * M_DK]
    m0 = [m_prev[:, h:h + 1] for h in heads]
    n0 = [n_ref[0, h:h + 1, :] for h in heads]
    logit = [jnp.where(causal, a_t[h:h + 1, :], -jnp.inf) for h in heads]
    mx = [jnp.maximum(jnp.max(logit[h], axis=1, keepdims=True), m0[h]) for h in heads]
    w0 = [jnp.exp(m0[h] - mx[h]) for h in heads]
    s = [_dot_nt(q_of(h), k_of(h)) * (jnp.exp(logit[h] - mx[h]) * K_SCALE) for h in heads]
    num = [_dot(s[h].astype(BF16), v_of(h)) + w0[h] * _dot(q_of(h), c_ref[0, h].astype(BF16))
           for h in heads]
    den = [jnp.sum(s[h], axis=1, keepdims=True)
           + w0[h] * jnp.sum(q_of(h).astype(F32) * n0[h], axis=1, keepdims=True) for h in heads]
    hh = [num[h] / jnp.maximum(jnp.abs(den[h]), jnp.exp(-(bsum[:, h:h + 1] + mx[h])))
          for h in heads]
    mu = [jnp.mean(hh[h], axis=1, keepdims=True) for h in heads]
    hc = [hh[h] - mu[h] for h in heads]
    var = [jnp.mean(hc[h] * hc[h], axis=1, keepdims=True) for h in heads]
    for h in heads:
        hn = hc[h] * lax.rsqrt(var[h] + LN_EPS) * mng_ref[lrow, cs[h]]
        gate = hm_ref[:, 3 * M_WIDTH + h * M_DK:3 * M_WIDTH + (h + 1) * M_DK].astype(F32)
        merged[:, cs[h]] = (hn * gate).astype(BF16)
    c_new, n_new = [], []
    for h in heads:
        mx_end = mx[h][TB - 1:TB, :]
        g_col = jnp.exp(a_all[:, h:h + 1] - mx_end) * K_SCALE
        g0 = jnp.exp(m0[h] - mx_end)
        kg = k_of(h).astype(F32) * g_col
        c_new.append(g0 * c_ref[0, h] + _dot_tn(kg.astype(BF16), v_of(h)))
        n_new.append(g0 * n0[h] + jnp.sum(kg, axis=0, keepdims=True))
        m_end = bsum[TB - 1:TB, h:h + 1] + mx_end
        m_new = jnp.where(lane_i == h, m_end, m_new)

    merged[:, M_WIDTH:M_WIDTH + C_WIDTH] = pc_ref[...]

    arow = lax.broadcasted_iota(jnp.int32, (A_BLK, 2 * A_BLK), 0)
    acol = lax.broadcasted_iota(jnp.int32, (A_BLK, 2 * A_BLK), 1)
    rel = A_BLK + arow - acol
    band = (rel >= 0) & (rel <= WINDOW)
    n_sub = TB // A_BLK
    kcat, vext, mask = [], [], []
    ones = jnp.ones((2 * A_BLK, HEAD_DIM), BF16)
    for sub in range(n_sub):
        r0 = sub * A_BLK
        if sub == 0:
            kcat.append(jnp.concatenate(
                [kprev[...], ha_ref[0:A_BLK, HA_K:HA_K + KV_WIDTH]], axis=0))
            vcat = jnp.concatenate([vprev[...], ha_ref[0:A_BLK, HA_V:HA_V + KV_WIDTH]], axis=0)
            mask.append(band & (acol >= jnp.where(j > 0, 0, A_BLK)))
        else:
            kcat.append(ha_ref[r0 - A_BLK:r0 + A_BLK, HA_K:HA_K + KV_WIDTH])
            vcat = ha_ref[r0 - A_BLK:r0 + A_BLK, HA_V:HA_V + KV_WIDTH]
            mask.append(band)
        vext.append([jnp.concatenate([vcat[:, kvh * HEAD_DIM:(kvh + 1) * HEAD_DIM], ones], axis=1)
                     for kvh in range(KV_HEADS)])
    units = [(sub, hq) for sub in range(n_sub) for hq in range(A_HEADS)]
    sink = {hq: sinks_ref[layer, hq] for hq in range(A_HEADS)}
    sc = {}
    for sub, hq in units:
        kvh = hq // GROUP
        qh = ha_ref[sub * A_BLK:(sub + 1) * A_BLK, hq * HEAD_DIM:(hq + 1) * HEAD_DIM]
        kh = kcat[sub][:, kvh * HEAD_DIM:(kvh + 1) * HEAD_DIM]
        sc[sub, hq] = jnp.where(mask[sub], _dot_nt(qh, kh) * A_SCALE, -jnp.inf)
    smax = {u: jnp.maximum(jnp.max(sc[u], axis=1, keepdims=True), sink[u[1]]) for u in units}
    pv = {u: _dot(jnp.exp(sc[u] - smax[u]).astype(BF16), vext[u[0]][u[1] // GROUP])
          for u in units}
    outs = {u: pv[u][:, 0:HEAD_DIM]
            / (pv[u][:, HEAD_DIM:HEAD_DIM + 1] + jnp.exp(sink[u[1]] - smax[u]))
            for u in units}
    for sub in range(n_sub):
        r0 = sub * A_BLK
        for hp in range(A_HEADS // 2):
            ao = jnp.concatenate([outs[sub, 2 * hp], outs[sub, 2 * hp + 1]], axis=1)
            gate = ha_ref[r0:r0 + A_BLK, HA_Z + hp * LANES:HA_Z + (hp + 1) * LANES].astype(F32)
            col = M_WIDTH + C_WIDTH + hp * LANES
            merged[r0:r0 + A_BLK, col:col + LANES] = (ao * gate).astype(BF16)
    merged_prev[...] = merged[...]

    @pl.when(step < n_blocks)
    def _():
        for h in heads:
            c_ref[0, h] = c_new[h]
        n_ref[0] = jnp.concatenate(n_new, axis=0)
        m_ref[0] = m_new
        kprev[...] = ha_ref[TB - A_BLK:TB, HA_K:HA_K + KV_WIDTH]
        vprev[...] = ha_ref[TB - A_BLK:TB, HA_V:HA_V + KV_WIDTH]
        k_ref[0] = ha_ref[TB - WINDOW:TB, HA_K:HA_K + KV_WIDTH].astype(F32)
        v_ref[0] = ha_ref[TB - WINDOW:TB, HA_V:HA_V + KV_WIDTH].astype(F32)


def _mixer(layer, sinks, pm, pc, pa, g, x2d, wout, prm, batch, seq):
    nb = seq // TB
    nblk = batch * nb
    mixed = lambda s: (jnp.minimum(s, nblk - 1), 0)
    done = lambda s: (jnp.maximum(s - 1, 0), 0)
    seq_of = lambda s: jnp.minimum(s, nblk - 1) // nb
    per_b3 = lambda s: (seq_of(s), 0, 0)
    state_shapes = [
        (batch, M_HEADS, M_DK, M_DK), (batch, M_HEADS, M_DK), (batch, 1, LANES),
        (batch, WINDOW, KV_WIDTH), (batch, WINDOW, KV_WIDTH)]
    params = [prm["gbias"], prm["mng"], prm["lng"], prm["lnb"]]
    return pl.pallas_call(
        functools.partial(_mixer_kernel, layer, nblk, nb),
        grid=(nblk + 1,),
        in_specs=[
            pl.BlockSpec(memory_space=pltpu.SMEM),
            pl.BlockSpec((TB, PM_W), mixed),
            pl.BlockSpec((TB, C_WIDTH), mixed),
            pl.BlockSpec((TB, HA_W), mixed),
            pl.BlockSpec((TB, LANES), mixed),
            pl.BlockSpec((TB, D_MODEL), done),
            pl.BlockSpec((None, D_MODEL, D_MODEL), lambda s: (layer, 0, 0)),
        ] + [_whole(a) for a in params],
        out_specs=[
            pl.BlockSpec((TB, D_MODEL), done),
            pl.BlockSpec((1, M_HEADS, M_DK, M_DK), lambda s: (seq_of(s), 0, 0, 0)),
            pl.BlockSpec((1, M_HEADS, M_DK), per_b3),
            pl.BlockSpec((1, 1, LANES), per_b3),
            pl.BlockSpec((1, WINDOW, KV_WIDTH), per_b3),
            pl.BlockSpec((1, WINDOW, KV_WIDTH), per_b3),
        ],
        out_shape=[jax.ShapeDtypeStruct((batch * seq, D_MODEL), F32)]
        + [jax.ShapeDtypeStruct(s, F32) for s in state_shapes],
        scratch_shapes=[
            pltpu.VMEM((A_BLK, KV_WIDTH), BF16),
            pltpu.VMEM((A_BLK, KV_WIDTH), BF16),
            pltpu.VMEM((TB, D_MODEL), BF16),
            pltpu.VMEM((TB, D_MODEL), BF16),
        ],
        compiler_params=pltpu.CompilerParams(
            dimension_semantics=("arbitrary",), vmem_limit_bytes=VMEM_LIMIT),
        name="mixer_prompt",
    )(sinks, pm, pc, pa, g, x2d, wout, *params)


def _sample_gates(hs_ref, gbias_row, m_ref):
    gb = hs_ref[:, MAIN_W:GATE_END] + gbias_row
    lf = pltpu.roll(_log_sigmoid(gb), LANES - M_HEADS, axis=1)
    m0 = m_ref[...]
    m_new = jnp.maximum(lf + m0, gb)
    wgt = jnp.exp(gb - m_new)
    w0 = jnp.exp(lf + m0 - m_new)
    return m_new, wgt, w0


def _sample_mixer_kernel(layer, hs_ref, c_ref, n_ref, m_ref, conv_ref, ckt_ref, cvt_ref,
                         gbias_ref, mng_ref, convw_ref, convb_ref, clng_ref, clnb_ref, sink_ref,
                         merged_ref, nout_ref, mout_ref, convout_ref, kout_ref, vout_ref):
    lrow = slice(layer, layer + 1)
    m_new, wgt, w0 = _sample_gates(hs_ref, gbias_ref[lrow, :], m_ref)
    einv = jnp.exp(-m_new)
    mout_ref[...] = m_new
    for h in range(M_HEADS):
        cs = slice(h * M_DK, (h + 1) * M_DK)
        q = hs_ref[:, cs]
        k = hs_ref[:, M_WIDTH + h * M_DK:M_WIDTH + (h + 1) * M_DK] * K_SCALE
        v = hs_ref[:, 2 * M_WIDTH + h * M_DK:2 * M_WIDTH + (h + 1) * M_DK]
        q_t = q.T
        wh = wgt[:, h:h + 1]
        w0h = w0[:, h:h + 1]
        n0 = n_ref[:, h, :]
        s = jnp.sum(q * k, axis=1, keepdims=True) * wh
        qc = jnp.concatenate(
            [jnp.sum(q_t[:, b:b + 1] * c_ref[b, h], axis=0, keepdims=True) for b in range(SB)],
            axis=0)
        num = s * v + w0h * qc
        den = s + w0h * jnp.sum(q * n0, axis=1, keepdims=True)
        hh = num / jnp.maximum(jnp.abs(den), einv[:, h:h + 1])
        nout_ref[:, h, :] = w0h * n0 + wh * k
        mu = jnp.mean(hh, axis=1, keepdims=True)
        hc = hh - mu
        var = jnp.mean(hc * hc, axis=1, keepdims=True)
        hn = hc * lax.rsqrt(var + LN_EPS) * mng_ref[lrow, cs]
        o = hs_ref[:, 3 * M_WIDTH + h * M_DK:3 * M_WIDTH + (h + 1) * M_DK]
        z = hs_ref[:, 4 * M_WIDTH + h * M_DK:4 * M_WIDTH + (h + 1) * M_DK]
        merged_ref[:, cs] = hn * _sigmoid(o) * _silu(z)

    cu = hs_ref[:, HM_W:HM_W + C_WIDTH]
    cg = hs_ref[:, HM_W + C_WIDTH:HM_W + 2 * C_WIDTH]
    cz = hs_ref[:, HM_W + 2 * C_WIDTH:HM_W + 3 * C_WIDTH]
    a = cu * _sigmoid(cg)
    yc = a * convw_ref[layer, CONV_BUF:CONV_K, :] + convb_ref[lrow, :]
    for t in range(CONV_BUF):
        yc = yc + conv_ref[t] * convw_ref[layer, t:t + 1, :]
    for t in range(CONV_BUF - 1):
        convout_ref[t] = conv_ref[t + 1]
    convout_ref[CONV_BUF - 1] = a
    yc = _layer_norm(yc, clng_ref[lrow, :], clnb_ref[lrow, :])
    merged_ref[:, M_WIDTH:M_WIDTH + C_WIDTH] = _silu(yc) * _silu(cz)

    ab = HM_W + HC_W
    lane = lax.broadcasted_iota(jnp.int32, (1, LANES), 1)
    rows_b = lax.broadcasted_iota(jnp.int32, (A_HEADS * SB, LANES), 0) % SB
    wcol = lax.broadcasted_iota(jnp.int32, (KV_WIDTH, WINDOW), 1)
    knew = hs_ref[:, ab + HA_K:ab + HA_K + KV_WIDTH]
    vnew = hs_ref[:, ab + HA_V:ab + HA_V + KV_WIDTH]
    q_parts, scn_parts, sink_parts = [], [], []
    for hq in range(A_HEADS):
        kvh = hq // GROUP
        t, u = hq // 2, hq % 2
        tile = hs_ref[:, ab + t * LANES:ab + (t + 1) * LANES]
        if u != kvh:
            tile = pltpu.roll(tile, HEAD_DIM, axis=1)
        qh = jnp.where((lane // HEAD_DIM) == kvh, tile, 0.0)
        q_parts.append(qh)
        scn_parts.append(jnp.sum(qh * knew, axis=1, keepdims=True))
        sink_parts.append(jnp.broadcast_to(sink_ref[lrow, hq:hq + 1], (SB, 1)))
    q_all = jnp.concatenate(q_parts, axis=0).astype(BF16)
    sc_new = jnp.concatenate(scn_parts, axis=0) * A_SCALE
    sink_all = jnp.concatenate(sink_parts, axis=0)
    sc = jnp.zeros((A_HEADS * SB, WINDOW), F32)
    for b in range(SB):
        sc = jnp.where(rows_b == b, _dot(q_all, ckt_ref[b].astype(BF16)), sc)
    sc = sc * A_SCALE
    smax = jnp.maximum(jnp.maximum(jnp.max(sc, axis=1, keepdims=True), sc_new), sink_all)
    e = jnp.exp(sc - smax)
    e_new = jnp.exp(sc_new - smax)
    denom = jnp.sum(e, axis=1, keepdims=True) + e_new + jnp.exp(sink_all - smax)
    p = (e / denom).astype(BF16)
    ob = (e_new / denom) * jnp.concatenate([vnew] * A_HEADS, axis=0)
    for b in range(SB):
        ob = ob + jnp.where(rows_b == b, _dot_nt(p, cvt_ref[b].astype(BF16)), 0.0)
    tiles = []
    for t in range(A_HEADS // 2):
        halves = []
        for u in range(2):
            hq = 2 * t + u
            x = ob[hq * SB:(hq + 1) * SB, :]
            if u != hq // GROUP:
                x = pltpu.roll(x, HEAD_DIM, axis=1)
            halves.append(x)
        tiles.append(jnp.where(lane < HEAD_DIM, halves[0], halves[1]))
    az = hs_ref[:, ab + HA_Z:ab + HA_Z + A_WIDTH]
    merged_ref[:, M_WIDTH + C_WIDTH:D_MODEL] = jnp.concatenate(tiles, axis=1) * _silu(az)
    knew_t = knew.T
    vnew_t = vnew.T
    for b in range(SB):
        kout_ref[b] = jnp.where(wcol == WINDOW - 1, knew_t[:, b:b + 1],
                                pltpu.roll(ckt_ref[b], WINDOW - 1, axis=1))
        vout_ref[b] = jnp.where(wcol == WINDOW - 1, vnew_t[:, b:b + 1],
                                pltpu.roll(cvt_ref[b], WINDOW - 1, axis=1))


def _sample_mixer(layer, hs, c, n, m, conv, ckt, cvt, prm):
    nb = hs.shape[0]
    row2 = lambda i: (i, 0)

    def state_specs(lead, pre):
        return [
            pl.BlockSpec(lead + (SB, M_HEADS, M_DK), lambda i: pre + (i, 0, 0)),
            pl.BlockSpec(lead + (SB, LANES), lambda i: pre + (i, 0)),
            pl.BlockSpec(lead + (CONV_BUF, SB, C_WIDTH), lambda i: pre + (0, i, 0)),
            pl.BlockSpec(lead + (SB, KV_WIDTH, WINDOW), lambda i: pre + (i, 0, 0)),
            pl.BlockSpec(lead + (SB, KV_WIDTH, WINDOW), lambda i: pre + (i, 0, 0)),
        ]

    c_spec = pl.BlockSpec((None, SB, M_HEADS, M_DK, M_DK), lambda i: (layer, i, 0, 0, 0))
    state_shapes = [
        (nb, M_HEADS, M_DK), (nb, LANES),
        (CONV_BUF, nb, C_WIDTH), (nb, KV_WIDTH, WINDOW), (nb, KV_WIDTH, WINDOW)]
    params = [prm["gbias"], prm["mng"], prm["convw"], prm["convb"], prm["clng"], prm["clnb"],
              prm["sink_row"]]
    return pl.pallas_call(
        functools.partial(_sample_mixer_kernel, layer),
        grid=(nb // SB,),
        in_specs=([pl.BlockSpec((SB, PROJ_W), row2), c_spec] + state_specs((None,), (layer,))
                  + [_whole(a) for a in params]),
        out_specs=[pl.BlockSpec((SB, D_MODEL), row2)] + state_specs((), ()),
        out_shape=[jax.ShapeDtypeStruct((nb, D_MODEL), F32)]
        + [jax.ShapeDtypeStruct(s, F32) for s in state_shapes],
        compiler_params=pltpu.CompilerParams(
            dimension_semantics=("arbitrary",), vmem_limit_bytes=VMEM_LIMIT),
        name="mixer_sample",
    )(hs, c, n, m, conv, ckt, cvt, *params)


def _update_c_kernel(hs_ref, c_ref, m_ref, gbias_ref, cout_ref):
    layer = pl.program_id(0)
    _, wgt, w0 = _sample_gates(hs_ref, gbias_ref[pl.ds(layer, 1), :], m_ref)
    for h in range(M_HEADS):
        k_t = (hs_ref[:, M_WIDTH + h * M_DK:M_WIDTH + (h + 1) * M_DK] * K_SCALE).T
        v = hs_ref[:, 2 * M_WIDTH + h * M_DK:2 * M_WIDTH + (h + 1) * M_DK]
        gv = wgt[:, h:h + 1] * v
        w0h = w0[:, h:h + 1]
        for b in range(SB):
            cout_ref[b, h] = w0h[b:b + 1, :] * c_ref[b, h] + k_t[:, b:b + 1] * gv[b:b + 1, :]


def _update_c(hs_all, c, m, prm):
    depth, nb = hs_all.shape[:2]
    per = lambda l, i: (l, i, 0)
    c_spec = pl.BlockSpec((None, SB, M_HEADS, M_DK, M_DK), lambda l, i: (l, i, 0, 0, 0))
    return pl.pallas_call(
        _update_c_kernel,
        grid=(depth, nb // SB),
        in_specs=[pl.BlockSpec((None, SB, PROJ_W), per), c_spec,
                  pl.BlockSpec((None, SB, LANES), per), _whole(prm["gbias"])],
        out_specs=c_spec,
        out_shape=jax.ShapeDtypeStruct(c.shape, F32),
        compiler_params=pltpu.CompilerParams(
            dimension_semantics=("arbitrary", "arbitrary"), vmem_limit_bytes=VMEM_LIMIT),
        name="update_matrix_memory",
    )(hs_all, c, m, prm["gbias"])


def _outproj_sample_kernel(layer, mg_ref, x_ref, wout_ref, lng_ref, lnb_ref, y_ref):
    lrow = slice(layer, layer + 1)
    y = _dot(mg_ref[...].astype(BF16), wout_ref[...])
    y_ref[...] = _layer_norm(ALPHA * x_ref[...] + y, lng_ref[lrow, :], lnb_ref[lrow, :])


def _outproj_sample(mg, xs, wout, prm, layer):
    nb = xs.shape[0]
    full = lambda shape: pl.BlockSpec(shape, lambda i: (0, 0))
    return pl.pallas_call(
        functools.partial(_outproj_sample_kernel, layer),
        grid=(1,),
        in_specs=[full((nb, D_MODEL)), full((nb, D_MODEL)),
                  pl.BlockSpec((None, D_MODEL, D_MODEL), lambda i: (layer, 0, 0)),
                  _whole(prm["lng"]), _whole(prm["lnb"])],
        out_specs=full((nb, D_MODEL)),
        out_shape=jax.ShapeDtypeStruct((nb, D_MODEL), F32),
        compiler_params=pltpu.CompilerParams(
            dimension_semantics=("arbitrary",), vmem_limit_bytes=VMEM_LIMIT),
        name="outproj_sample",
    )(mg, xs, wout, prm["lng"], prm["lnb"])


def _cast_kernel(w_ref, o_ref):
    o_ref[...] = w_ref[...].astype(BF16)


def _cast_w_out(w_out):
    depth, d, n = w_out.shape
    rows = 512
    return pl.pallas_call(
        _cast_kernel,
        grid=(depth, d // rows),
        in_specs=[pl.BlockSpec((None, rows, n), lambda l, i: (l, i, 0))],
        out_specs=pl.BlockSpec((None, rows, n), lambda l, i: (l, i, 0)),
        out_shape=jax.ShapeDtypeStruct((depth, d, n), BF16),
        compiler_params=pltpu.CompilerParams(dimension_semantics=("arbitrary", "arbitrary")),
        name="cast_w_out",
    )(w_out)


def _lane_rows(*mats):
    v = jnp.concatenate([a.astype(F32) for a in mats], axis=1)
    return jnp.pad(v, ((0, 0), (0, LANES - v.shape[1])))


def _cache_rows_by_window(cache):
    depth, nb = cache.shape[:2]
    return jnp.transpose(cache, (0, 1, 3, 4, 2)).reshape(depth, nb, KV_WIDTH, WINDOW)


def _cache_window_major(cache_t):
    depth, nb = cache_t.shape[:2]
    return jnp.transpose(cache_t.reshape(depth, nb, KV_HEADS, HEAD_DIM, WINDOW), (0, 1, 4, 2, 3))


def kernel(x_prompt, x_sample, state_C, state_n, state_m, state_conv, cache_k, cache_v, w_in, w_out,
           b_igate, b_fgate, m_norm_g, conv_w, conv_b, conv_ln_g, conv_ln_b, sinks, ln_g, ln_b):
    batch, seq, _ = x_prompt.shape
    nb = x_sample.shape[0]
    xp = x_prompt.reshape(batch * seq, D_MODEL)
    xs = x_sample.reshape(nb, D_MODEL)
    wt = jnp.swapaxes(w_in, 1, 2)
    wout = _cast_w_out(w_out)
    prm = {
        "gbias": _lane_rows(b_igate, b_fgate),
        "sink_row": _lane_rows(sinks),
        "mng": m_norm_g, "convw": conv_w, "convb": conv_b, "clng": conv_ln_g, "clnb": conv_ln_b,
        "lng": ln_g, "lnb": ln_b,
    }
    m_in = jnp.pad(state_m, ((0, 0), (0, 0), (0, LANES - M_HEADS)))
    sconv = jnp.swapaxes(state_conv, 1, 2)
    ckt = _cache_rows_by_window(cache_k)
    cvt = _cache_rows_by_window(cache_v)
    st_p = []
    st_s = []
    hs_all = []
    for l in range(DEPTH):
        pm, pc, pa, g, conv_p, hs = _inproj(xp, xs, wt, prm, l, batch)
        xp, c_p, n_p, m_p, k_p, v_p = _mixer(l, sinks, pm, pc, pa, g, xp, wout, prm, batch, seq)
        st_p.append((c_p, n_p, m_p, conv_p, k_p, v_p))
        mg, *states = _sample_mixer(l, hs, state_C, state_n, m_in, sconv, ckt, cvt, prm)
        st_s.append(states)
        hs_all.append(hs)
        xs = _outproj_sample(mg, xs, wout, prm, l)
    c_s = _update_c(jnp.stack(hs_all, axis=0), state_C, m_in, prm)

    def stacked(per_layer):
        return [jnp.stack(a, axis=0) for a in zip(*per_layer)]

    c_p, n_p, m_p, conv_p, k_p, v_p = stacked(st_p)
    n_s, m_s, conv_s, k_s, v_s = stacked(st_s)
    return (xp.reshape(batch, seq, D_MODEL), xs.reshape(nb, 1, D_MODEL),
            c_p, n_p, m_p.reshape(DEPTH, batch, LANES)[:, :, :M_HEADS], conv_p,
            k_p.reshape(DEPTH, batch, WINDOW, KV_HEADS, HEAD_DIM),
            v_p.reshape(DEPTH, batch, WINDOW, KV_HEADS, HEAD_DIM),
            c_s, n_s, m_s[:, :, :M_HEADS], jnp.swapaxes(conv_s, 1, 2),
            _cache_window_major(k_s), _cache_window_major(v_s))
```

```python
import functools

import jax
import jax.numpy as jnp
from jax import lax
from jax.experimental import pallas as pl
from jax.experimental.pallas import tpu as pltpu

F32 = jnp.float32
BF16 = jnp.bfloat16

D_MODEL = 2048
DEPTH = 2
M_WIDTH = 1024
M_HEADS = 4
M_DK = 256
C_WIDTH = 512
CONV_K = 31
CONV_BUF = CONV_K - 1
A_WIDTH = 512
HEAD_DIM = 64
A_HEADS = 8
KV_HEADS = 2
GROUP = A_HEADS // KV_HEADS
KV_WIDTH = KV_HEADS * HEAD_DIM
WINDOW = 128
ALPHA = (2 * DEPTH) ** 0.25
LN_EPS = 1e-5
K_SCALE = M_DK ** -0.5
A_SCALE = HEAD_DIM ** -0.5

LANES = 128
SUBLANES = 8

HM_W = 5 * M_WIDTH
HC_W = 3 * C_WIDTH
HA_W = 2 * A_WIDTH + 2 * KV_WIDTH
MAIN_W = HM_W + HC_W + HA_W
GATE_END = MAIN_W + LANES
PROJ_W = GATE_END
GATE_OFF = 4 * M_WIDTH
N_GATES = 2 * M_HEADS
W_CHUNK = 256
W_SLOTS = 4
PLAIN_CHUNKS = GATE_OFF // W_CHUNK
MAIN_CHUNKS = MAIN_W // W_CHUNK
HA_K = A_WIDTH
HA_V = A_WIDTH + KV_WIDTH
HA_Z = A_WIDTH + 2 * KV_WIDTH

TM_PROJ = 256
TB = 256
A_BLK = 128
CONV_HALO = 32
SB = 8
VMEM_LIMIT = 56 * 1024 * 1024
INPROJ_VMEM_LIMIT = 60 * 1024 * 1024


def _sigmoid(x):
    return 1.0 / (1.0 + jnp.exp(-x))


def _silu(x):
    return x * _sigmoid(x)


def _log_sigmoid(x):
    return jnp.minimum(x, 0.0) - jnp.log1p(jnp.exp(-jnp.abs(x)))


def _layer_norm(x, g, b):
    mu = jnp.mean(x, axis=-1, keepdims=True)
    xc = x - mu
    var = jnp.mean(xc * xc, axis=-1, keepdims=True)
    return xc * lax.rsqrt(var + LN_EPS) * g + b


def _dot(a, b):
    return jnp.dot(a, b, preferred_element_type=F32)


def _dot_nt(a, b):
    return lax.dot_general(a, b, (((1,), (1,)), ((), ())), preferred_element_type=F32)


def _dot_tn(a, b):
    return lax.dot_general(a, b, (((0,), (0,)), ((), ())), preferred_element_type=F32)


def _whole(arr):
    zeros = (0,) * arr.ndim
    return pl.BlockSpec(arr.shape, lambda *_: zeros)


PM_W = 4 * M_WIDTH
PA_QKV = A_WIDTH + 2 * KV_WIDTH
QKV_CHUNK = 512
SHIFTED_ROWS = CONV_HALO + TM_PROJ - SUBLANES


def _stage_rows(src_hbm, dst_ref, stage, sem, n_chunks, src_row):
    def chunk_copy(c, slot):
        return pltpu.make_async_copy(
            src_hbm.at[pl.ds(pl.multiple_of(src_row(c), SUBLANES), W_CHUNK), :],
            stage.at[slot], sem.at[slot])

    for c in range(W_SLOTS - 1):
        chunk_copy(c, c).start()

    def body(c, carry):
        slot = c % W_SLOTS
        chunk_copy(c, slot).wait()
        ahead = c + W_SLOTS - 1

        @pl.when(ahead < n_chunks)
        def _():
            chunk_copy(ahead, ahead % W_SLOTS).start()

        dst_ref[pl.ds(pl.multiple_of(c * W_CHUNK, W_CHUNK), W_CHUNK), :] = stage[slot].astype(BF16)
        return carry

    lax.fori_loop(0, n_chunks, body, 0)


def _stage_w_in(w_hbm, wt_ref, stage, sem):
    _stage_rows(w_hbm, wt_ref, stage, sem, MAIN_CHUNKS,
                lambda c: jnp.where(c < PLAIN_CHUNKS, c * W_CHUNK, c * W_CHUNK + N_GATES))
    stage[0, 0:LANES, :] = jnp.zeros((LANES, D_MODEL), F32)
    gate_copy = pltpu.make_async_copy(
        w_hbm.at[GATE_OFF:GATE_OFF + N_GATES, :], stage.at[0, 0:N_GATES, :], sem.at[0])
    gate_copy.start()
    gate_copy.wait()
    wt_ref[MAIN_W:GATE_END, :] = stage[0, 0:LANES, :].astype(BF16)


def _inproj_kernel(layer, blocks_per_seq, x_ref, xs_ref, w_hbm, convw_ref, convb_ref, clng_ref,
                   clnb_ref, pm_ref, pc_ref, pa_ref, g_ref, conv_ref, hs_ref,
                   wt_ref, stage, sem, xbuf, abuf, sbuf, ybuf, zbuf):
    i = pl.program_id(0)
    lrow = slice(layer, layer + 1)

    n_sample = xs_ref.shape[0]

    @pl.when(i % blocks_per_seq == 0)
    def _():
        abuf[0:CONV_HALO, :] = jnp.zeros((CONV_HALO, C_WIDTH), F32)

    xbuf[0:TM_PROJ, :] = x_ref[...].astype(BF16)

    def project_block(n_extra):
        n_rows = TM_PROJ + n_extra

        def proj(c0, c1):
            full = _dot_nt(xbuf[0:n_rows, :], wt_ref[c0:c1, :])
            if n_extra:
                hs_ref[:, c0:c1] = full[TM_PROJ:n_rows, :]
                return full[0:TM_PROJ, :]
            return full

        for c in range(0, 3 * M_WIDTH, QKV_CHUNK):
            pm_ref[:, c:c + QKV_CHUNK] = proj(c, c + QKV_CHUNK).astype(BF16)
        for h in range(M_HEADS):
            o = proj(3 * M_WIDTH + h * M_DK, 3 * M_WIDTH + (h + 1) * M_DK)
            z = proj(4 * M_WIDTH + h * M_DK, 4 * M_WIDTH + (h + 1) * M_DK)
            pm_ref[:, 3 * M_WIDTH + h * M_DK:3 * M_WIDTH + (h + 1) * M_DK] = (
                _sigmoid(o) * _silu(z)).astype(BF16)

        cu = proj(HM_W, HM_W + C_WIDTH)
        cg = proj(HM_W + C_WIDTH, HM_W + 2 * C_WIDTH)
        abuf[CONV_HALO:CONV_HALO + TM_PROJ, :] = cu * _sigmoid(cg)
        zbuf[...] = _silu(proj(HM_W + 2 * C_WIDTH, HM_W + 3 * C_WIDTH))
        for p in range(1, SUBLANES):
            sbuf[p - 1] = abuf[p:p + SHIFTED_ROWS, :]
        tap0 = CONV_HALO - CONV_BUF
        for r in range(0, TM_PROJ, 128):
            for c in range(0, C_WIDTH, LANES):
                acc = jnp.zeros((128, LANES), F32)
                for t in range(CONV_K):
                    p = (tap0 + t) % SUBLANES
                    base = r + tap0 + t - p
                    src = abuf if p == 0 else sbuf.at[p - 1]
                    acc = acc + (src[base:base + 128, c:c + LANES]
                                 * convw_ref[layer, t:t + 1, c:c + LANES])
                ybuf[r:r + 128, c:c + LANES] = acc + convb_ref[lrow, c:c + LANES]
        conv_ref[0] = abuf[CONV_HALO + TM_PROJ - CONV_BUF:CONV_HALO + TM_PROJ, :]
        abuf[0:CONV_HALO, :] = abuf[TM_PROJ:TM_PROJ + CONV_HALO, :]
        yc = _layer_norm(ybuf[...], clng_ref[lrow, :], clnb_ref[lrow, :])
        pc_ref[...] = (_silu(yc) * zbuf[...]).astype(BF16)

        base = HM_W + HC_W
        pa_ref[:, 0:PA_QKV] = proj(base, base + PA_QKV).astype(BF16)
        pa_ref[:, PA_QKV:HA_W] = _silu(proj(base + PA_QKV, base + HA_W)).astype(BF16)
        g_ref[...] = proj(MAIN_W, GATE_END)

    @pl.when(i == 0)
    def _():
        _stage_w_in(w_hbm.at[layer], wt_ref, stage, sem)
        xbuf[TM_PROJ:TM_PROJ + n_sample, :] = xs_ref[...].astype(BF16)
        project_block(n_sample)

    @pl.when(i > 0)
    def _():
        project_block(0)


def _inproj(x2d, xs, w_in_t, prm, layer, batch):
    m = x2d.shape[0]
    ns = xs.shape[0]
    nblk = m // TM_PROJ
    row = lambda i: (i, 0)
    once = lambda i: (0, 0)
    params = [prm["convw"], prm["convb"], prm["clng"], prm["clnb"]]
    return pl.pallas_call(
        functools.partial(_inproj_kernel, layer, nblk // batch),
        grid=(nblk,),
        in_specs=[
            pl.BlockSpec((TM_PROJ, D_MODEL), row),
            pl.BlockSpec((ns, D_MODEL), once),
            pl.BlockSpec(memory_space=pl.ANY),
        ] + [_whole(a) for a in params],
        out_specs=[
            pl.BlockSpec((TM_PROJ, PM_W), row),
            pl.BlockSpec((TM_PROJ, C_WIDTH), row),
            pl.BlockSpec((TM_PROJ, HA_W), row),
            pl.BlockSpec((TM_PROJ, LANES), row),
            pl.BlockSpec((1, CONV_BUF, C_WIDTH), lambda i: (i // (nblk // batch), 0, 0)),
            pl.BlockSpec((ns, PROJ_W), once),
        ],
        out_shape=[
            jax.ShapeDtypeStruct((m, PM_W), BF16),
            jax.ShapeDtypeStruct((m, C_WIDTH), BF16),
            jax.ShapeDtypeStruct((m, HA_W), BF16),
            jax.ShapeDtypeStruct((m, LANES), F32),
            jax.ShapeDtypeStruct((batch, CONV_BUF, C_WIDTH), F32),
            jax.ShapeDtypeStruct((ns, PROJ_W), F32),
        ],
        scratch_shapes=[
            pltpu.VMEM((PROJ_W, D_MODEL), BF16),
            pltpu.VMEM((W_SLOTS, W_CHUNK, D_MODEL), F32),
            pltpu.SemaphoreType.DMA((W_SLOTS,)),
            pltpu.VMEM((TM_PROJ + ns, D_MODEL), BF16),
            pltpu.VMEM((CONV_HALO + TM_PROJ, C_WIDTH), F32),
            pltpu.VMEM((SUBLANES - 1, SHIFTED_ROWS, C_WIDTH), F32),
            pltpu.VMEM((TM_PROJ, C_WIDTH), F32),
            pltpu.VMEM((TM_PROJ, C_WIDTH), F32),
        ],
        compiler_params=pltpu.CompilerParams(
            dimension_semantics=("arbitrary",), vmem_limit_bytes=INPROJ_VMEM_LIMIT),
        name="inproj_prompt",
    )(x2d, xs, w_in_t, *params)


def _cumsum_rows(x):
    n = x.shape[0]
    row = lax.broadcasted_iota(jnp.int32, x.shape, 0)
    s = 1
    while s < n:
        x = x + jnp.where(row >= s, pltpu.roll(x, s, axis=0), 0.0)
        s *= 2
    return x


def _mixer_kernel(layer, n_blocks, blocks_per_seq, sinks_ref, hm_ref, pc_ref, ha_ref, g_ref,
                  xprev_ref, wout_ref, gbias_ref, mng_ref, lng_ref, lnb_ref,
                  y_ref, c_ref, n_ref, m_ref, k_ref, v_ref,
                  kprev, vprev, merged, merged_prev):
    step = pl.program_id(0)
    j = jnp.minimum(step, n_blocks - 1) % blocks_per_seq
    lrow = slice(layer, layer + 1)

    @pl.when(step == 0)
    def _():
        merged_prev[...] = jnp.zeros_like(merged_prev)

    @pl.when(j == 0)
    def _():
        c_ref[...] = jnp.zeros_like(c_ref)
        n_ref[...] = jnp.zeros_like(n_ref)
        m_ref[...] = jnp.zeros_like(m_ref)
        kprev[...] = jnp.zeros_like(kprev)
        vprev[...] = jnp.zeros_like(vprev)

    y = _dot(merged_prev[...], wout_ref[...])
    y_ref[...] = _layer_norm(ALPHA * xprev_ref[...] + y, lng_ref[lrow, :], lnb_ref[lrow, :])

    gb = g_ref[...] + gbias_ref[lrow, :]
    bsum = _cumsum_rows(_log_sigmoid(gb))
    bsum = pltpu.roll(bsum, LANES - M_HEADS, axis=1)
    a_all = gb - bsum
    a_t = a_all.T
    m_prev = m_ref[0]
    row_i = lax.broadcasted_iota(jnp.int32, (TB, TB), 0)
    col_i = lax.broadcasted_iota(jnp.int32, (TB, TB), 1)
    causal = col_i <= row_i
    lane_i = lax.broadcasted_iota(jnp.int32, (1, LANES), 1)
    m_new = m_prev

    heads = range(M_HEADS)
    cs = [slice(h * M_DK, (h + 1) * M_DK) for h in heads]
    q_of = lambda h: hm_ref[:, cs[h]]
    k_of = lambda h: hm_ref[:, M_WIDTH + h * M_DK:M_WIDTH + (h + 1) * M_DK]
    v_of = lambda h: hm_ref[:, 2 * M_WIDTH + h * M_DK:2 * M_WIDTH + (h + 1) * M_DK]
    m0 = [m_prev[:, h:h + 1] for h in heads]
    n0 = [n_ref[0, h:h + 1, :] for h in heads]
    logit = [jnp.where(causal, a_t[h:h + 1, :], -jnp.inf) for h in heads]
    mx = [jnp.maximum(jnp.max(logit[h], axis=1, keepdims=True), m0[h]) for h in heads]
    w0 = [jnp.exp(m0[h] - mx[h]) for h in heads]
    s = [_dot_nt(q_of(h), k_of(h)) * (jnp.exp(logit[h] - mx[h]) * K_SCALE) for h in heads]
    num = [_dot(s[h].astype(BF16), v_of(h)) + w0[h] * _dot(q_of(h), c_ref[0, h].astype(BF16))
           for h in heads]
    den = [jnp.sum(s[h], axis=1, keepdims=True)
           + w0[h] * jnp.sum(q_of(h).astype(F32) * n0[h], axis=1, keepdims=True) for h in heads]
    hh = [num[h] / jnp.maximum(jnp.abs(den[h]), jnp.exp(-(bsum[:, h:h + 1] + mx[h])))
          for h in heads]
    mu = [jnp.mean(hh[h], axis=1, keepdims=True) for h in heads]
    hc = [hh[h] - mu[h] for h in heads]
    var = [jnp.mean(hc[h] * hc[h], axis=1, keepdims=True) for h in heads]
    for h in heads:
        hn = hc[h] * lax.rsqrt(var[h] + LN_EPS) * mng_ref[lrow, cs[h]]
        gate = hm_ref[:, 3 * M_WIDTH + h * M_DK:3 * M_WIDTH + (h + 1) * M_DK].astype(F32)
        merged[:, cs[h]] = (hn * gate).astype(BF16)
    c_new, n_new = [], []
    for h in heads:
        mx_end = mx[h][TB - 1:TB, :]
        g_col = jnp.exp(a_all[:, h:h + 1] - mx_end) * K_SCALE
        g0 = jnp.exp(m0[h] - mx_end)
        kg = k_of(h).astype(F32) * g_col
        c_new.append(g0 * c_ref[0, h] + _dot_tn(kg.astype(BF16), v_of(h)))
        n_new.append(g0 * n0[h] + jnp.sum(kg, axis=0, keepdims=True))
        m_end = bsum[TB - 1:TB, h:h + 1] + mx_end
        m_new = jnp.where(lane_i == h, m_end, m_new)

    merged[:, M_WIDTH:M_WIDTH + C_WIDTH] = pc_ref[...]

    arow = lax.broadcasted_iota(jnp.int32, (A_BLK, 2 * A_BLK), 0)
    acol = lax.broadcasted_iota(jnp.int32, (A_BLK, 2 * A_BLK), 1)
    rel = A_BLK + arow - acol
    band = (rel >= 0) & (rel <= WINDOW)
    n_sub = TB // A_BLK
    kcat, vext, mask = [], [], []
    ones = jnp.ones((2 * A_BLK, HEAD_DIM), BF16)
    for sub in range(n_sub):
        r0 = sub * A_BLK
        if sub == 0:
            kcat.append(jnp.concatenate(
                [kprev[...], ha_ref[0:A_BLK, HA_K:HA_K + KV_WIDTH]], axis=0))
            vcat = jnp.concatenate([vprev[...], ha_ref[0:A_BLK, HA_V:HA_V + KV_WIDTH]], axis=0)
            mask.append(band & (acol >= jnp.where(j > 0, 0, A_BLK)))
        else:
            kcat.append(ha_ref[r0 - A_BLK:r0 + A_BLK, HA_K:HA_K + KV_WIDTH])
            vcat = ha_ref[r0 - A_BLK:r0 + A_BLK, HA_V:HA_V + KV_WIDTH]
            mask.append(band)
        vext.append([jnp.concatenate([vcat[:, kvh * HEAD_DIM:(kvh + 1) * HEAD_DIM], ones], axis=1)
                     for kvh in range(KV_HEADS)])
    units = [(sub, hq) for sub in range(n_sub) for hq in range(A_HEADS)]
    sink = {hq: sinks_ref[layer, hq] for hq in range(A_HEADS)}
    sc = {}
    for sub, hq in units:
        kvh = hq // GROUP
        qh = ha_ref[sub * A_BLK:(sub + 1) * A_BLK, hq * HEAD_DIM:(hq + 1) * HEAD_DIM]
        kh = kcat[sub][:, kvh * HEAD_DIM:(kvh + 1) * HEAD_DIM]
        sc[sub, hq] = jnp.where(mask[sub], _dot_nt(qh, kh) * A_SCALE, -jnp.inf)
    smax = {u: jnp.maximum(jnp.max(sc[u], axis=1, keepdims=True), sink[u[1]]) for u in units}
    pv = {u: _dot(jnp.exp(sc[u] - smax[u]).astype(BF16), vext[u[0]][u[1] // GROUP])
          for u in units}
    outs = {u: pv[u][:, 0:HEAD_DIM]
            / (pv[u][:, HEAD_DIM:HEAD_DIM + 1] + jnp.exp(sink[u[1]] - smax[u]))
            for u in units}
    for sub in range(n_sub):
        r0 = sub * A_BLK
        for hp in range(A_HEADS // 2):
            ao = jnp.concatenate([outs[sub, 2 * hp], outs[sub, 2 * hp + 1]], axis=1)
            gate = ha_ref[r0:r0 + A_BLK, HA_Z + hp * LANES:HA_Z + (hp + 1) * LANES].astype(F32)
            col = M_WIDTH + C_WIDTH + hp * LANES
            merged[r0:r0 + A_BLK, col:col + LANES] = (ao * gate).astype(BF16)
    merged_prev[...] = merged[...]

    @pl.when(step < n_blocks)
    def _():
        for h in heads:
            c_ref[0, h] = c_new[h]
        n_ref[0] = jnp.concatenate(n_new, axis=0)
        m_ref[0] = m_new
        kprev[...] = ha_ref[TB - A_BLK:TB, HA_K:HA_K + KV_WIDTH]
        vprev[...] = ha_ref[TB - A_BLK:TB, HA_V:HA_V + KV_WIDTH]
        k_ref[0] = ha_ref[TB - WINDOW:TB, HA_K:HA_K + KV_WIDTH].astype(F32)
        v_ref[0] = ha_ref[TB - WINDOW:TB, HA_V:HA_V + KV_WIDTH].astype(F32)


def _mixer(layer, sinks, pm, pc, pa, g, x2d, wout, prm, batch, seq):
    nb = seq // TB
    nblk = batch * nb
    mixed = lambda s: (jnp.minimum(s, nblk - 1), 0)
    done = lambda s: (jnp.maximum(s - 1, 0), 0)
    seq_of = lambda s: jnp.minimum(s, nblk - 1) // nb
    per_b3 = lambda s: (seq_of(s), 0, 0)
    state_shapes = [
        (batch, M_HEADS, M_DK, M_DK), (batch, M_HEADS, M_DK), (batch, 1, LANES),
        (batch, WINDOW, KV_WIDTH), (batch, WINDOW, KV_WIDTH)]
    params = [prm["gbias"], prm["mng"], prm["lng"], prm["lnb"]]
    return pl.pallas_call(
        functools.partial(_mixer_kernel, layer, nblk, nb),
        grid=(nblk + 1,),
        in_specs=[
            pl.BlockSpec(memory_space=pltpu.SMEM),
            pl.BlockSpec((TB, PM_W), mixed),
            pl.BlockSpec((TB, C_WIDTH), mixed),
            pl.BlockSpec((TB, HA_W), mixed),
            pl.BlockSpec((TB, LANES), mixed),
            pl.BlockSpec((TB, D_MODEL), done),
            pl.BlockSpec((None, D_MODEL, D_MODEL), lambda s: (layer, 0, 0)),
        ] + [_whole(a) for a in params],
        out_specs=[
            pl.BlockSpec((TB, D_MODEL), done),
            pl.BlockSpec((1, M_HEADS, M_DK, M_DK), lambda s: (seq_of(s), 0, 0, 0)),
            pl.BlockSpec((1, M_HEADS, M_DK), per_b3),
            pl.BlockSpec((1, 1, LANES), per_b3),
            pl.BlockSpec((1, WINDOW, KV_WIDTH), per_b3),
            pl.BlockSpec((1, WINDOW, KV_WIDTH), per_b3),
        ],
        out_shape=[jax.ShapeDtypeStruct((batch * seq, D_MODEL), F32)]
        + [jax.ShapeDtypeStruct(s, F32) for s in state_shapes],
        scratch_shapes=[
            pltpu.VMEM((A_BLK, KV_WIDTH), BF16),
            pltpu.VMEM((A_BLK, KV_WIDTH), BF16),
            pltpu.VMEM((TB, D_MODEL), BF16),
            pltpu.VMEM((TB, D_MODEL), BF16),
        ],
        compiler_params=pltpu.CompilerParams(
            dimension_semantics=("arbitrary",), vmem_limit_bytes=VMEM_LIMIT),
        name="mixer_prompt",
    )(sinks, pm, pc, pa, g, x2d, wout, *params)


def _sample_gates(hs_ref, gbias_row, m_ref):
    gb = hs_ref[:, MAIN_W:GATE_END] + gbias_row
    lf = pltpu.roll(_log_sigmoid(gb), LANES - M_HEADS, axis=1)
    m0 = m_ref[...]
    m_new = jnp.maximum(lf + m0, gb)
    wgt = jnp.exp(gb - m_new)
    w0 = jnp.exp(lf + m0 - m_new)
    return m_new, wgt, w0


def _sample_mixer_kernel(layer, hs_ref, c0_ref, c1_ref, c2_ref, c3_ref, n_ref, m_ref, conv_ref,
                         ckt_ref, cvt_ref, gbias_ref, mng_ref, convw_ref, convb_ref, clng_ref,
                         clnb_ref, sink_ref,
                         merged_ref, nout_ref, mout_ref, convout_ref, kout_ref, vout_ref):
    lrow = slice(layer, layer + 1)
    c_heads = (c0_ref, c1_ref, c2_ref, c3_ref)
    m_new, wgt, w0 = _sample_gates(hs_ref, gbias_ref[lrow, :], m_ref)
    einv = jnp.exp(-m_new)
    mout_ref[...] = m_new
    for h in range(M_HEADS):
        cs = slice(h * M_DK, (h + 1) * M_DK)
        q = hs_ref[:, cs]
        k = hs_ref[:, M_WIDTH + h * M_DK:M_WIDTH + (h + 1) * M_DK] * K_SCALE
        v = hs_ref[:, 2 * M_WIDTH + h * M_DK:2 * M_WIDTH + (h + 1) * M_DK]
        q_t = q.T
        wh = wgt[:, h:h + 1]
        w0h = w0[:, h:h + 1]
        n0 = n_ref[:, h, :]
        s = jnp.sum(q * k, axis=1, keepdims=True) * wh
        qc = jnp.concatenate(
            [jnp.sum(q_t[:, b:b + 1] * c_heads[h][b, 0], axis=0, keepdims=True)
             for b in range(SB)],
            axis=0)
        num = s * v + w0h * qc
        den = s + w0h * jnp.sum(q * n0, axis=1, keepdims=True)
        hh = num / jnp.maximum(jnp.abs(den), einv[:, h:h + 1])
        nout_ref[:, h, :] = w0h * n0 + wh * k
        mu = jnp.mean(hh, axis=1, keepdims=True)
        hc = hh - mu
        var = jnp.mean(hc * hc, axis=1, keepdims=True)
        hn = hc * lax.rsqrt(var + LN_EPS) * mng_ref[lrow, cs]
        o = hs_ref[:, 3 * M_WIDTH + h * M_DK:3 * M_WIDTH + (h + 1) * M_DK]
        z = hs_ref[:, 4 * M_WIDTH + h * M_DK:4 * M_WIDTH + (h + 1) * M_DK]
        merged_ref[:, cs] = hn * _sigmoid(o) * _silu(z)

    cu = hs_ref[:, HM_W:HM_W + C_WIDTH]
    cg = hs_ref[:, HM_W + C_WIDTH:HM_W + 2 * C_WIDTH]
    cz = hs_ref[:, HM_W + 2 * C_WIDTH:HM_W + 3 * C_WIDTH]
    a = cu * _sigmoid(cg)
    yc = a * convw_ref[layer, CONV_BUF:CONV_K, :] + convb_ref[lrow, :]
    for t in range(CONV_BUF):
        yc = yc + conv_ref[t] * convw_ref[layer, t:t + 1, :]
    for t in range(CONV_BUF - 1):
        convout_ref[t] = conv_ref[t + 1]
    convout_ref[CONV_BUF - 1] = a
    yc = _layer_norm(yc, clng_ref[lrow, :], clnb_ref[lrow, :])
    merged_ref[:, M_WIDTH:M_WIDTH + C_WIDTH] = _silu(yc) * _silu(cz)

    ab = HM_W + HC_W
    lane = lax.broadcasted_iota(jnp.int32, (1, LANES), 1)
    rows_b = lax.broadcasted_iota(jnp.int32, (A_HEADS * SB, LANES), 0) % SB
    wcol = lax.broadcasted_iota(jnp.int32, (KV_WIDTH, WINDOW), 1)
    knew = hs_ref[:, ab + HA_K:ab + HA_K + KV_WIDTH]
    vnew = hs_ref[:, ab + HA_V:ab + HA_V + KV_WIDTH]
    q_parts, scn_parts, sink_parts = [], [], []
    for hq in range(A_HEADS):
        kvh = hq // GROUP
        t, u = hq // 2, hq % 2
        tile = hs_ref[:, ab + t * LANES:ab + (t + 1) * LANES]
        if u != kvh:
            tile = pltpu.roll(tile, HEAD_DIM, axis=1)
        qh = jnp.where((lane // HEAD_DIM) == kvh, tile, 0.0)
        q_parts.append(qh)
        scn_parts.append(jnp.sum(qh * knew, axis=1, keepdims=True))
        sink_parts.append(jnp.broadcast_to(sink_ref[lrow, hq:hq + 1], (SB, 1)))
    q_all = jnp.concatenate(q_parts, axis=0).astype(BF16)
    sc_new = jnp.concatenate(scn_parts, axis=0) * A_SCALE
    sink_all = jnp.concatenate(sink_parts, axis=0)
    sc = jnp.zeros((A_HEADS * SB, WINDOW), F32)
    for b in range(SB):
        sc = jnp.where(rows_b == b, _dot(q_all, ckt_ref[b].astype(BF16)), sc)
    sc = sc * A_SCALE
    smax = jnp.maximum(jnp.maximum(jnp.max(sc, axis=1, keepdims=True), sc_new), sink_all)
    e = jnp.exp(sc - smax)
    e_new = jnp.exp(sc_new - smax)
    denom = jnp.sum(e, axis=1, keepdims=True) + e_new + jnp.exp(sink_all - smax)
    p = (e / denom).astype(BF16)
    ob = (e_new / denom) * jnp.concatenate([vnew] * A_HEADS, axis=0)
    for b in range(SB):
        ob = ob + jnp.where(rows_b == b, _dot_nt(p, cvt_ref[b].astype(BF16)), 0.0)
    tiles = []
    for t in range(A_HEADS // 2):
        halves = []
        for u in range(2):
            hq = 2 * t + u
            x = ob[hq * SB:(hq + 1) * SB, :]
            if u != hq // GROUP:
                x = pltpu.roll(x, HEAD_DIM, axis=1)
            halves.append(x)
        tiles.append(jnp.where(lane < HEAD_DIM, halves[0], halves[1]))
    az = hs_ref[:, ab + HA_Z:ab + HA_Z + A_WIDTH]
    merged_ref[:, M_WIDTH + C_WIDTH:D_MODEL] = jnp.concatenate(tiles, axis=1) * _silu(az)
    knew_t = knew.T
    vnew_t = vnew.T
    for b in range(SB):
        kout_ref[b] = jnp.where(wcol == WINDOW - 1, knew_t[:, b:b + 1],
                                pltpu.roll(ckt_ref[b], WINDOW - 1, axis=1))
        vout_ref[b] = jnp.where(wcol == WINDOW - 1, vnew_t[:, b:b + 1],
                                pltpu.roll(cvt_ref[b], WINDOW - 1, axis=1))


def _sample_mixer(layer, hs, c, n, m, conv, ckt, cvt, prm):
    nb = hs.shape[0]
    row2 = lambda i: (i, 0)

    def state_specs(lead, pre):
        return [
            pl.BlockSpec(lead + (SB, M_HEADS, M_DK), lambda i: pre + (i, 0, 0)),
            pl.BlockSpec(lead + (SB, LANES), lambda i: pre + (i, 0)),
            pl.BlockSpec(lead + (CONV_BUF, SB, C_WIDTH), lambda i: pre + (0, i, 0)),
            pl.BlockSpec(lead + (SB, KV_WIDTH, WINDOW), lambda i: pre + (i, 0, 0)),
            pl.BlockSpec(lead + (SB, KV_WIDTH, WINDOW), lambda i: pre + (i, 0, 0)),
        ]

    c_specs = [pl.BlockSpec((None, SB, 1, M_DK, M_DK), lambda i, h=h: (layer, i, h, 0, 0))
               for h in range(M_HEADS)]
    state_shapes = [
        (nb, M_HEADS, M_DK), (nb, LANES),
        (CONV_BUF, nb, C_WIDTH), (nb, KV_WIDTH, WINDOW), (nb, KV_WIDTH, WINDOW)]
    params = [prm["gbias"], prm["mng"], prm["convw"], prm["convb"], prm["clng"], prm["clnb"],
              prm["sink_row"]]
    return pl.pallas_call(
        functools.partial(_sample_mixer_kernel, layer),
        grid=(nb // SB,),
        in_specs=([pl.BlockSpec((SB, PROJ_W), row2)] + c_specs + state_specs((None,), (layer,))
                  + [_whole(a) for a in params]),
        out_specs=[pl.BlockSpec((SB, D_MODEL), row2)] + state_specs((), ()),
        out_shape=[jax.ShapeDtypeStruct((nb, D_MODEL), F32)]
        + [jax.ShapeDtypeStruct(s, F32) for s in state_shapes],
        compiler_params=pltpu.CompilerParams(
            dimension_semantics=("arbitrary",), vmem_limit_bytes=VMEM_LIMIT),
        name="mixer_sample",
    )(hs, *([c] * M_HEADS), n, m, conv, ckt, cvt, *params)


def _update_c_kernel(hs_ref, c_ref, m_ref, gbias_ref, cout_ref):
    layer = pl.program_id(0)
    _, wgt, w0 = _sample_gates(hs_ref, gbias_ref[pl.ds(layer, 1), :], m_ref)
    for h in range(M_HEADS):
        k_t = (hs_ref[:, M_WIDTH + h * M_DK:M_WIDTH + (h + 1) * M_DK] * K_SCALE).T
        v = hs_ref[:, 2 * M_WIDTH + h * M_DK:2 * M_WIDTH + (h + 1) * M_DK]
        gv = wgt[:, h:h + 1] * v
        w0h = w0[:, h:h + 1]
        for b in range(SB):
            cout_ref[b, h] = w0h[b:b + 1, :] * c_ref[b, h] + k_t[:, b:b + 1] * gv[b:b + 1, :]


def _update_c(hs_all, c, m, prm):
    depth, nb = hs_all.shape[:2]
    per = lambda l, i: (l, i, 0)
    c_spec = pl.BlockSpec((None, SB, M_HEADS, M_DK, M_DK), lambda l, i: (l, i, 0, 0, 0))
    return pl.pallas_call(
        _update_c_kernel,
        grid=(depth, nb // SB),
        in_specs=[pl.BlockSpec((None, SB, PROJ_W), per), c_spec,
                  pl.BlockSpec((None, SB, LANES), per), _whole(prm["gbias"])],
        out_specs=c_spec,
        out_shape=jax.ShapeDtypeStruct(c.shape, F32),
        compiler_params=pltpu.CompilerParams(
            dimension_semantics=("arbitrary", "arbitrary"), vmem_limit_bytes=VMEM_LIMIT),
        name="update_matrix_memory",
    )(hs_all, c, m, prm["gbias"])


def _outproj_sample_kernel(layer, mg_ref, x_ref, wout_ref, lng_ref, lnb_ref, y_ref):
    lrow = slice(layer, layer + 1)
    y = _dot(mg_ref[...].astype(BF16), wout_ref[...])
    y_ref[...] = _layer_norm(ALPHA * x_ref[...] + y, lng_ref[lrow, :], lnb_ref[lrow, :])


def _outproj_sample(mg, xs, wout, prm, layer):
    nb = xs.shape[0]
    full = lambda shape: pl.BlockSpec(shape, lambda i: (0, 0))
    return pl.pallas_call(
        functools.partial(_outproj_sample_kernel, layer),
        grid=(1,),
        in_specs=[full((nb, D_MODEL)), full((nb, D_MODEL)),
                  pl.BlockSpec((None, D_MODEL, D_MODEL), lambda i: (layer, 0, 0)),
                  _whole(prm["lng"]), _whole(prm["lnb"])],
        out_specs=full((nb, D_MODEL)),
        out_shape=jax.ShapeDtypeStruct((nb, D_MODEL), F32),
        compiler_params=pltpu.CompilerParams(
            dimension_semantics=("arbitrary",), vmem_limit_bytes=VMEM_LIMIT),
        name="outproj_sample",
    )(mg, xs, wout, prm["lng"], prm["lnb"])


def _cast_kernel(w_ref, o_ref):
    o_ref[...] = w_ref[...].astype(BF16)


def _cast_w_out(w_out):
    depth, d, n = w_out.shape
    rows = 512
    return pl.pallas_call(
        _cast_kernel,
        grid=(depth, d // rows),
        in_specs=[pl.BlockSpec((None, rows, n), lambda l, i: (l, i, 0))],
        out_specs=pl.BlockSpec((None, rows, n), lambda l, i: (l, i, 0)),
        out_shape=jax.ShapeDtypeStruct((depth, d, n), BF16),
        compiler_params=pltpu.CompilerParams(dimension_semantics=("arbitrary", "arbitrary")),
        name="cast_w_out",
    )(w_out)


def _lane_rows(*mats):
    v = jnp.concatenate([a.astype(F32) for a in mats], axis=1)
    return jnp.pad(v, ((0, 0), (0, LANES - v.shape[1])))


def _cache_rows_by_window(cache):
    depth, nb = cache.shape[:2]
    return jnp.transpose(cache, (0, 1, 3, 4, 2)).reshape(depth, nb, KV_WIDTH, WINDOW)


def _cache_window_major(cache_t):
    depth, nb = cache_t.shape[:2]
    return jnp.transpose(cache_t.reshape(depth, nb, KV_HEADS, HEAD_DIM, WINDOW), (0, 1, 4, 2, 3))


def kernel(x_prompt, x_sample, state_C, state_n, state_m, state_conv, cache_k, cache_v, w_in, w_out,
           b_igate, b_fgate, m_norm_g, conv_w, conv_b, conv_ln_g, conv_ln_b, sinks, ln_g, ln_b):
    batch, seq, _ = x_prompt.shape
    nb = x_sample.shape[0]
    xp = x_prompt.reshape(batch * seq, D_MODEL)
    xs = x_sample.reshape(nb, D_MODEL)
    wt = jnp.swapaxes(w_in, 1, 2)
    wout = _cast_w_out(w_out)
    prm = {
        "gbias": _lane_rows(b_igate, b_fgate),
        "sink_row": _lane_rows(sinks),
        "mng": m_norm_g, "convw": conv_w, "convb": conv_b, "clng": conv_ln_g, "clnb": conv_ln_b,
        "lng": ln_g, "lnb": ln_b,
    }
    m_in = jnp.pad(state_m, ((0, 0), (0, 0), (0, LANES - M_HEADS)))
    sconv = jnp.swapaxes(state_conv, 1, 2)
    ckt = _cache_rows_by_window(cache_k)
    cvt = _cache_rows_by_window(cache_v)
    st_p = []
    st_s = []
    hs_all = []
    for l in range(DEPTH):
        pm, pc, pa, g, conv_p, hs = _inproj(xp, xs, wt, prm, l, batch)
        xp, c_p, n_p, m_p, k_p, v_p = _mixer(l, sinks, pm, pc, pa, g, xp, wout, prm, batch, seq)
        st_p.append((c_p, n_p, m_p, conv_p, k_p, v_p))
        mg, *states = _sample_mixer(l, hs, state_C, state_n, m_in, sconv, ckt, cvt, prm)
        st_s.append(states)
        hs_all.append(hs)
        xs = _outproj_sample(mg, xs, wout, prm, l)
    c_s = _update_c(jnp.stack(hs_all, axis=0), state_C, m_in, prm)

    def stacked(per_layer):
        return [jnp.stack(a, axis=0) for a in zip(*per_layer)]

    c_p, n_p, m_p, conv_p, k_p, v_p = stacked(st_p)
    n_s, m_s, conv_s, k_s, v_s = stacked(st_s)
    return (xp.reshape(batch, seq, D_MODEL), xs.reshape(nb, 1, D_MODEL),
            c_p, n_p, m_p.reshape(DEPTH, batch, LANES)[:, :, :M_HEADS], conv_p,
            k_p.reshape(DEPTH, batch, WINDOW, KV_HEADS, HEAD_DIM),
            v_p.reshape(DEPTH, batch, WINDOW, KV_HEADS, HEAD_DIM),
            c_s, n_s, m_s[:, :, :M_HEADS], jnp.swapaxes(conv_s, 1, 2),
            _cache_window_major(k_s), _cache_window_major(v_s))
```

```python
import functools

import jax
import jax.numpy as jnp
from jax import lax
from jax.experimental import pallas as pl
from jax.experimental.pallas import tpu as pltpu

F32 = jnp.float32
BF16 = jnp.bfloat16

D_MODEL = 2048
DEPTH = 2
M_WIDTH = 1024
M_HEADS = 4
M_DK = 256
C_WIDTH = 512
CONV_K = 31
CONV_BUF = CONV_K - 1
A_WIDTH = 512
HEAD_DIM = 64
A_HEADS = 8
KV_HEADS = 2
GROUP = A_HEADS // KV_HEADS
KV_WIDTH = KV_HEADS * HEAD_DIM
WINDOW = 128
ALPHA = (2 * DEPTH) ** 0.25
LN_EPS = 1e-5
K_SCALE = M_DK ** -0.5
A_SCALE = HEAD_DIM ** -0.5

LANES = 128
SUBLANES = 8
BF16_SUBLANES = 2 * SUBLANES

HM_W = 5 * M_WIDTH
HC_W = 3 * C_WIDTH
HA_W = 2 * A_WIDTH + 2 * KV_WIDTH
MAIN_W = HM_W + HC_W + HA_W
GATE_END = MAIN_W + LANES
PROJ_W = GATE_END
GATE_OFF = 4 * M_WIDTH
N_GATES = 2 * M_HEADS
W_CHUNK = 256
W_SLOTS = 4
PLAIN_CHUNKS = GATE_OFF // W_CHUNK
MAIN_CHUNKS = MAIN_W // W_CHUNK
HA_K = A_WIDTH
HA_V = A_WIDTH + KV_WIDTH
HA_Z = A_WIDTH + 2 * KV_WIDTH

TM_PROJ = 256
TB = 256
A_BLK = 128
CONV_HALO = 32
SB = 8
VMEM_LIMIT = 56 * 1024 * 1024
INPROJ_VMEM_LIMIT = 60 * 1024 * 1024


def _sigmoid(x):
    return 1.0 / (1.0 + jnp.exp(-x))


def _silu(x):
    return x * _sigmoid(x)


def _log_sigmoid(x):
    return jnp.minimum(x, 0.0) - jnp.log1p(jnp.exp(-jnp.abs(x)))


def _layer_norm(x, g, b):
    mu = jnp.mean(x, axis=-1, keepdims=True)
    xc = x - mu
    var = jnp.mean(xc * xc, axis=-1, keepdims=True)
    return xc * lax.rsqrt(var + LN_EPS) * g + b


def _dot(a, b):
    return jnp.dot(a, b, preferred_element_type=F32)


def _dot_nt(a, b):
    return lax.dot_general(a, b, (((1,), (1,)), ((), ())), preferred_element_type=F32)


def _dot_tn(a, b):
    return lax.dot_general(a, b, (((0,), (0,)), ((), ())), preferred_element_type=F32)


def _whole(arr):
    zeros = (0,) * arr.ndim
    return pl.BlockSpec(arr.shape, lambda *_: zeros)


PM_W = 4 * M_WIDTH
PA_QKV = A_WIDTH + 2 * KV_WIDTH
QKV_CHUNK = 512
SHIFTED_ROWS = CONV_HALO + TM_PROJ - SUBLANES


def _stage_rows(src_hbm, dst_ref, stage, sem, n_chunks, src_row):
    def chunk_copy(c, slot):
        return pltpu.make_async_copy(
            src_hbm.at[pl.ds(pl.multiple_of(src_row(c), SUBLANES), W_CHUNK), :],
            stage.at[slot], sem.at[slot])

    for c in range(W_SLOTS - 1):
        chunk_copy(c, c).start()

    def body(c, carry):
        slot = c % W_SLOTS
        chunk_copy(c, slot).wait()
        ahead = c + W_SLOTS - 1

        @pl.when(ahead < n_chunks)
        def _():
            chunk_copy(ahead, ahead % W_SLOTS).start()

        dst_ref[pl.ds(pl.multiple_of(c * W_CHUNK, W_CHUNK), W_CHUNK), :] = stage[slot].astype(BF16)
        return carry

    lax.fori_loop(0, n_chunks, body, 0)


def _stage_w_in(w_hbm, wt_ref, stage, sem):
    _stage_rows(w_hbm, wt_ref, stage, sem, MAIN_CHUNKS,
                lambda c: jnp.where(c < PLAIN_CHUNKS, c * W_CHUNK, c * W_CHUNK + N_GATES))
    stage[0, 0:LANES, :] = jnp.zeros((LANES, D_MODEL), F32)
    gate_copy = pltpu.make_async_copy(
        w_hbm.at[GATE_OFF:GATE_OFF + N_GATES, :], stage.at[0, 0:N_GATES, :], sem.at[0])
    gate_copy.start()
    gate_copy.wait()
    wt_ref[MAIN_W:GATE_END, :] = stage[0, 0:LANES, :].astype(BF16)


def _inproj_kernel(layer, blocks_per_seq, x_ref, xs_ref, w_hbm, wo_in_ref, convw_ref, convb_ref,
                   clng_ref, clnb_ref, pm_ref, pc_ref, pa_ref, g_ref, conv_ref, hs_ref, wo_ref,
                   wt_ref, stage, sem, xbuf, abuf, sbuf, ybuf, zbuf):
    i = pl.program_id(0)
    lrow = slice(layer, layer + 1)

    n_sample = xs_ref.shape[0]

    @pl.when(i % blocks_per_seq == 0)
    def _():
        abuf[0:CONV_HALO, :] = jnp.zeros((CONV_HALO, C_WIDTH), F32)

    xbuf[0:TM_PROJ, :] = x_ref[...].astype(BF16)
    wo_ref[...] = wo_in_ref[...].astype(BF16)

    def project_block(n_extra):
        n_rows = TM_PROJ + n_extra

        def proj(c0, c1):
            full = _dot_nt(xbuf[0:n_rows, :], wt_ref[c0:c1, :])
            if n_extra:
                hs_ref[:, c0:c1] = full[TM_PROJ:n_rows, :]
                return full[0:TM_PROJ, :]
            return full

        for c in range(0, 3 * M_WIDTH, QKV_CHUNK):
            pm_ref[:, c:c + QKV_CHUNK] = proj(c, c + QKV_CHUNK).astype(BF16)
        for h in range(M_HEADS):
            o = proj(3 * M_WIDTH + h * M_DK, 3 * M_WIDTH + (h + 1) * M_DK)
            z = proj(4 * M_WIDTH + h * M_DK, 4 * M_WIDTH + (h + 1) * M_DK)
            pm_ref[:, 3 * M_WIDTH + h * M_DK:3 * M_WIDTH + (h + 1) * M_DK] = (
                _sigmoid(o) * _silu(z)).astype(BF16)

        cu = proj(HM_W, HM_W + C_WIDTH)
        cg = proj(HM_W + C_WIDTH, HM_W + 2 * C_WIDTH)
        abuf[CONV_HALO:CONV_HALO + TM_PROJ, :] = cu * _sigmoid(cg)
        zbuf[...] = _silu(proj(HM_W + 2 * C_WIDTH, HM_W + 3 * C_WIDTH))
        for p in range(1, SUBLANES):
            sbuf[p - 1] = abuf[p:p + SHIFTED_ROWS, :]
        tap0 = CONV_HALO - CONV_BUF
        for r in range(0, TM_PROJ, 128):
            for c in range(0, C_WIDTH, LANES):
                acc = jnp.zeros((128, LANES), F32)
                for t in range(CONV_K):
                    p = (tap0 + t) % SUBLANES
                    base = r + tap0 + t - p
                    src = abuf if p == 0 else sbuf.at[p - 1]
                    acc = acc + (src[base:base + 128, c:c + LANES]
                                 * convw_ref[layer, t:t + 1, c:c + LANES])
                ybuf[r:r + 128, c:c + LANES] = acc + convb_ref[lrow, c:c + LANES]
        conv_ref[0] = abuf[CONV_HALO + TM_PROJ - CONV_BUF:CONV_HALO + TM_PROJ, :]
        abuf[0:CONV_HALO, :] = abuf[TM_PROJ:TM_PROJ + CONV_HALO, :]
        yc = _layer_norm(ybuf[...], clng_ref[lrow, :], clnb_ref[lrow, :])
        pc_ref[...] = (_silu(yc) * zbuf[...]).astype(BF16)

        base = HM_W + HC_W
        pa_ref[:, 0:PA_QKV] = proj(base, base + PA_QKV).astype(BF16)
        pa_ref[:, PA_QKV:HA_W] = _silu(proj(base + PA_QKV, base + HA_W)).astype(BF16)
        g_ref[...] = proj(MAIN_W, GATE_END)

    @pl.when(i == 0)
    def _():
        _stage_w_in(w_hbm.at[layer], wt_ref, stage, sem)
        xbuf[TM_PROJ:TM_PROJ + n_sample, :] = xs_ref[...].astype(BF16)
        project_block(n_sample)

    @pl.when(i > 0)
    def _():
        project_block(0)


def _inproj(x2d, xs, w_in_t, w_out, prm, layer, batch):
    m = x2d.shape[0]
    ns = xs.shape[0]
    nblk = m // TM_PROJ
    assert D_MODEL % (nblk * BF16_SUBLANES) == 0
    wo_rows = D_MODEL // nblk
    row = lambda i: (i, 0)
    once = lambda i: (0, 0)
    params = [prm["convw"], prm["convb"], prm["clng"], prm["clnb"]]
    return pl.pallas_call(
        functools.partial(_inproj_kernel, layer, nblk // batch),
        grid=(nblk,),
        in_specs=[
            pl.BlockSpec((TM_PROJ, D_MODEL), row),
            pl.BlockSpec((ns, D_MODEL), once),
            pl.BlockSpec(memory_space=pl.ANY),
            pl.BlockSpec((None, wo_rows, D_MODEL), lambda i: (layer, i, 0)),
        ] + [_whole(a) for a in params],
        out_specs=[
            pl.BlockSpec((TM_PROJ, PM_W), row),
            pl.BlockSpec((TM_PROJ, C_WIDTH), row),
            pl.BlockSpec((TM_PROJ, HA_W), row),
            pl.BlockSpec((TM_PROJ, LANES), row),
            pl.BlockSpec((1, CONV_BUF, C_WIDTH), lambda i: (i // (nblk // batch), 0, 0)),
            pl.BlockSpec((ns, PROJ_W), once),
            pl.BlockSpec((wo_rows, D_MODEL), row),
        ],
        out_shape=[
            jax.ShapeDtypeStruct((m, PM_W), BF16),
            jax.ShapeDtypeStruct((m, C_WIDTH), BF16),
            jax.ShapeDtypeStruct((m, HA_W), BF16),
            jax.ShapeDtypeStruct((m, LANES), F32),
            jax.ShapeDtypeStruct((batch, CONV_BUF, C_WIDTH), F32),
            jax.ShapeDtypeStruct((ns, PROJ_W), F32),
            jax.ShapeDtypeStruct((D_MODEL, D_MODEL), BF16),
        ],
        scratch_shapes=[
            pltpu.VMEM((PROJ_W, D_MODEL), BF16),
            pltpu.VMEM((W_SLOTS, W_CHUNK, D_MODEL), F32),
            pltpu.SemaphoreType.DMA((W_SLOTS,)),
            pltpu.VMEM((TM_PROJ + ns, D_MODEL), BF16),
            pltpu.VMEM((CONV_HALO + TM_PROJ, C_WIDTH), F32),
            pltpu.VMEM((SUBLANES - 1, SHIFTED_ROWS, C_WIDTH), F32),
            pltpu.VMEM((TM_PROJ, C_WIDTH), F32),
            pltpu.VMEM((TM_PROJ, C_WIDTH), F32),
        ],
        compiler_params=pltpu.CompilerParams(
            dimension_semantics=("arbitrary",), vmem_limit_bytes=INPROJ_VMEM_LIMIT),
        name="inproj_prompt",
    )(x2d, xs, w_in_t, w_out, *params)


def _cumsum_rows(x):
    n = x.shape[0]
    row = lax.broadcasted_iota(jnp.int32, x.shape, 0)
    s = 1
    while s < n:
        x = x + jnp.where(row >= s, pltpu.roll(x, s, axis=0), 0.0)
        s *= 2
    return x


def _mixer_kernel(layer, n_blocks, blocks_per_seq, sinks_ref, hm_ref, pc_ref, ha_ref, g_ref,
                  xprev_ref, wout_ref, gbias_ref, mng_ref, lng_ref, lnb_ref,
                  y_ref, c_ref, n_ref, m_ref, k_ref, v_ref,
                  kprev, vprev, merged, merged_prev):
    step = pl.program_id(0)
    j = jnp.minimum(step, n_blocks - 1) % blocks_per_seq
    lrow = slice(layer, layer + 1)

    @pl.when(step == 0)
    def _():
        merged_prev[...] = jnp.zeros_like(merged_prev)

    @pl.when(j == 0)
    def _():
        c_ref[...] = jnp.zeros_like(c_ref)
        n_ref[...] = jnp.zeros_like(n_ref)
        m_ref[...] = jnp.zeros_like(m_ref)
        kprev[...] = jnp.zeros_like(kprev)
        vprev[...] = jnp.zeros_like(vprev)

    y = _dot(merged_prev[...], wout_ref[...])
    y_ref[...] = _layer_norm(ALPHA * xprev_ref[...] + y, lng_ref[lrow, :], lnb_ref[lrow, :])

    gb = g_ref[...] + gbias_ref[lrow, :]
    bsum = _cumsum_rows(_log_sigmoid(gb))
    bsum = pltpu.roll(bsum, LANES - M_HEADS, axis=1)
    a_all = gb - bsum
    a_t = a_all.T
    m_prev = m_ref[0]
    row_i = lax.broadcasted_iota(jnp.int32, (TB, TB), 0)
    col_i = lax.broadcasted_iota(jnp.int32, (TB, TB), 1)
    causal = col_i <= row_i
    lane_i = lax.broadcasted_iota(jnp.int32, (1, LANES), 1)
    m_new = m_prev

    heads = range(M_HEADS)
    cs = [slice(h * M_DK, (h + 1) * M_DK) for h in heads]
    q_of = lambda h: hm_ref[:, cs[h]]
    k_of = lambda h: hm_ref[:, M_WIDTH + h * M_DK:M_WIDTH + (h + 1) * M_DK]
    v_of = lambda h: hm_ref[:, 2 * M_WIDTH + h * M_DK:2 * M_WIDTH + (h + 1) * M_DK]
    m0 = [m_prev[:, h:h + 1] for h in heads]
    n0 = [n_ref[0, h:h + 1, :] for h in heads]
    logit = [jnp.where(causal, a_t[h:h + 1, :], -jnp.inf) for h in heads]
    mx = [jnp.maximum(jnp.max(logit[h], axis=1, keepdims=True), m0[h]) for h in heads]
    w0 = [jnp.exp(m0[h] - mx[h]) for h in heads]
    s = [_dot_nt(q_of(h), k_of(h)) * (jnp.exp(logit[h] - mx[h]) * K_SCALE) for h in heads]
    num = [_dot(s[h].astype(BF16), v_of(h)) + w0[h] * _dot(q_of(h), c_ref[0, h].astype(BF16))
           for h in heads]
    den = [jnp.sum(s[h], axis=1, keepdims=True)
           + w0[h] * jnp.sum(q_of(h).astype(F32) * n0[h], axis=1, keepdims=True) for h in heads]
    hh = [num[h] / jnp.maximum(jnp.abs(den[h]), jnp.exp(-(bsum[:, h:h + 1] + mx[h])))
          for h in heads]
    mu = [jnp.mean(hh[h], axis=1, keepdims=True) for h in heads]
    hc = [hh[h] - mu[h] for h in heads]
    var = [jnp.mean(hc[h] * hc[h], axis=1, keepdims=True) for h in heads]
    for h in heads:
        hn = hc[h] * lax.rsqrt(var[h] + LN_EPS) * mng_ref[lrow, cs[h]]
        gate = hm_ref[:, 3 * M_WIDTH + h * M_DK:3 * M_WIDTH + (h + 1) * M_DK].astype(F32)
        merged[:, cs[h]] = (hn * gate).astype(BF16)
    c_new, n_new = [], []
    for h in heads:
        mx_end = mx[h][TB - 1:TB, :]
        g_col = jnp.exp(a_all[:, h:h + 1] - mx_end) * K_SCALE
        g0 = jnp.exp(m0[h] - mx_end)
        kg = k_of(h).astype(F32) * g_col
        c_new.append(g0 * c_ref[0, h] + _dot_tn(kg.astype(BF16), v_of(h)))
        n_new.append(g0 * n0[h] + jnp.sum(kg, axis=0, keepdims=True))
        m_end = bsum[TB - 1:TB, h:h + 1] + mx_end
        m_new = jnp.where(lane_i == h, m_end, m_new)

    merged[:, M_WIDTH:M_WIDTH + C_WIDTH] = pc_ref[...]

    arow = lax.broadcasted_iota(jnp.int32, (A_BLK, 2 * A_BLK), 0)
    acol = lax.broadcasted_iota(jnp.int32, (A_BLK, 2 * A_BLK), 1)
    rel = A_BLK + arow - acol
    band = (rel >= 0) & (rel <= WINDOW)
    n_sub = TB // A_BLK
    kcat, vext, mask = [], [], []
    ones = jnp.ones((2 * A_BLK, HEAD_DIM), BF16)
    for sub in range(n_sub):
        r0 = sub * A_BLK
        if sub == 0:
            kcat.append(jnp.concatenate(
                [kprev[...], ha_ref[0:A_BLK, HA_K:HA_K + KV_WIDTH]], axis=0))
            vcat = jnp.concatenate([vprev[...], ha_ref[0:A_BLK, HA_V:HA_V + KV_WIDTH]], axis=0)
            mask.append(band & (acol >= jnp.where(j > 0, 0, A_BLK)))
        else:
            kcat.append(ha_ref[r0 - A_BLK:r0 + A_BLK, HA_K:HA_K + KV_WIDTH])
            vcat = ha_ref[r0 - A_BLK:r0 + A_BLK, HA_V:HA_V + KV_WIDTH]
            mask.append(band)
        vext.append([jnp.concatenate([vcat[:, kvh * HEAD_DIM:(kvh + 1) * HEAD_DIM], ones], axis=1)
                     for kvh in range(KV_HEADS)])
    units = [(sub, hq) for sub in range(n_sub) for hq in range(A_HEADS)]
    sink = {hq: sinks_ref[layer, hq] for hq in range(A_HEADS)}
    sc = {}
    for sub, hq in units:
        kvh = hq // GROUP
        qh = ha_ref[sub * A_BLK:(sub + 1) * A_BLK, hq * HEAD_DIM:(hq + 1) * HEAD_DIM]
        kh = kcat[sub][:, kvh * HEAD_DIM:(kvh + 1) * HEAD_DIM]
        sc[sub, hq] = jnp.where(mask[sub], _dot_nt(qh, kh) * A_SCALE, -jnp.inf)
    smax = {u: jnp.maximum(jnp.max(sc[u], axis=1, keepdims=True), sink[u[1]]) for u in units}
    pv = {u: _dot(jnp.exp(sc[u] - smax[u]).astype(BF16), vext[u[0]][u[1] // GROUP])
          for u in units}
    outs = {u: pv[u][:, 0:HEAD_DIM]
            / (pv[u][:, HEAD_DIM:HEAD_DIM + 1] + jnp.exp(sink[u[1]] - smax[u]))
            for u in units}
    for sub in range(n_sub):
        r0 = sub * A_BLK
        for hp in range(A_HEADS // 2):
            ao = jnp.concatenate([outs[sub, 2 * hp], outs[sub, 2 * hp + 1]], axis=1)
            gate = ha_ref[r0:r0 + A_BLK, HA_Z + hp * LANES:HA_Z + (hp + 1) * LANES].astype(F32)
            col = M_WIDTH + C_WIDTH + hp * LANES
            merged[r0:r0 + A_BLK, col:col + LANES] = (ao * gate).astype(BF16)
    merged_prev[...] = merged[...]

    @pl.when(step < n_blocks)
    def _():
        for h in heads:
            c_ref[0, h] = c_new[h]
        n_ref[0] = jnp.concatenate(n_new, axis=0)
        m_ref[0] = m_new
        kprev[...] = ha_ref[TB - A_BLK:TB, HA_K:HA_K + KV_WIDTH]
        vprev[...] = ha_ref[TB - A_BLK:TB, HA_V:HA_V + KV_WIDTH]
        k_ref[0] = ha_ref[TB - WINDOW:TB, HA_K:HA_K + KV_WIDTH].astype(F32)
        v_ref[0] = ha_ref[TB - WINDOW:TB, HA_V:HA_V + KV_WIDTH].astype(F32)


def _mixer(layer, sinks, pm, pc, pa, g, x2d, wout, prm, batch, seq):
    nb = seq // TB
    nblk = batch * nb
    mixed = lambda s: (jnp.minimum(s, nblk - 1), 0)
    done = lambda s: (jnp.maximum(s - 1, 0), 0)
    seq_of = lambda s: jnp.minimum(s, nblk - 1) // nb
    per_b3 = lambda s: (seq_of(s), 0, 0)
    state_shapes = [
        (batch, M_HEADS, M_DK, M_DK), (batch, M_HEADS, M_DK), (batch, 1, LANES),
        (batch, WINDOW, KV_WIDTH), (batch, WINDOW, KV_WIDTH)]
    params = [prm["gbias"], prm["mng"], prm["lng"], prm["lnb"]]
    return pl.pallas_call(
        functools.partial(_mixer_kernel, layer, nblk, nb),
        grid=(nblk + 1,),
        in_specs=[
            pl.BlockSpec(memory_space=pltpu.SMEM),
            pl.BlockSpec((TB, PM_W), mixed),
            pl.BlockSpec((TB, C_WIDTH), mixed),
            pl.BlockSpec((TB, HA_W), mixed),
            pl.BlockSpec((TB, LANES), mixed),
            pl.BlockSpec((TB, D_MODEL), done),
            pl.BlockSpec((D_MODEL, D_MODEL), lambda s: (0, 0)),
        ] + [_whole(a) for a in params],
        out_specs=[
            pl.BlockSpec((TB, D_MODEL), done),
            pl.BlockSpec((1, M_HEADS, M_DK, M_DK), lambda s: (seq_of(s), 0, 0, 0)),
            pl.BlockSpec((1, M_HEADS, M_DK), per_b3),
            pl.BlockSpec((1, 1, LANES), per_b3),
            pl.BlockSpec((1, WINDOW, KV_WIDTH), per_b3),
            pl.BlockSpec((1, WINDOW, KV_WIDTH), per_b3),
        ],
        out_shape=[jax.ShapeDtypeStruct((batch * seq, D_MODEL), F32)]
        + [jax.ShapeDtypeStruct(s, F32) for s in state_shapes],
        scratch_shapes=[
            pltpu.VMEM((A_BLK, KV_WIDTH), BF16),
            pltpu.VMEM((A_BLK, KV_WIDTH), BF16),
            pltpu.VMEM((TB, D_MODEL), BF16),
            pltpu.VMEM((TB, D_MODEL), BF16),
        ],
        compiler_params=pltpu.CompilerParams(
            dimension_semantics=("arbitrary",), vmem_limit_bytes=VMEM_LIMIT),
        name="mixer_prompt",
    )(sinks, pm, pc, pa, g, x2d, wout, *params)


def _sample_gates(hs_ref, gbias_row, m_ref):
    gb = hs_ref[:, MAIN_W:GATE_END] + gbias_row
    lf = pltpu.roll(_log_sigmoid(gb), LANES - M_HEADS, axis=1)
    m0 = m_ref[...]
    m_new = jnp.maximum(lf + m0, gb)
    wgt = jnp.exp(gb - m_new)
    w0 = jnp.exp(lf + m0 - m_new)
    return m_new, wgt, w0


def _sample_mixer_kernel(layer, hs_ref, c0_ref, c1_ref, c2_ref, c3_ref, n_ref, m_ref, conv_ref,
                         ckt_ref, cvt_ref, gbias_ref, mng_ref, convw_ref, convb_ref, clng_ref,
                         clnb_ref, sink_ref,
                         merged_ref, nout_ref, mout_ref, convout_ref, kout_ref, vout_ref):
    lrow = slice(layer, layer + 1)
    c_heads = (c0_ref, c1_ref, c2_ref, c3_ref)
    m_new, wgt, w0 = _sample_gates(hs_ref, gbias_ref[lrow, :], m_ref)
    einv = jnp.exp(-m_new)
    mout_ref[...] = m_new
    for h in range(M_HEADS):
        cs = slice(h * M_DK, (h + 1) * M_DK)
        q = hs_ref[:, cs]
        k = hs_ref[:, M_WIDTH + h * M_DK:M_WIDTH + (h + 1) * M_DK] * K_SCALE
        v = hs_ref[:, 2 * M_WIDTH + h * M_DK:2 * M_WIDTH + (h + 1) * M_DK]
        q_t = q.T
        wh = wgt[:, h:h + 1]
        w0h = w0[:, h:h + 1]
        n0 = n_ref[:, h, :]
        s = jnp.sum(q * k, axis=1, keepdims=True) * wh
        qc = jnp.concatenate(
            [jnp.sum(q_t[:, b:b + 1] * c_heads[h][b, 0], axis=0, keepdims=True)
             for b in range(SB)],
            axis=0)
        num = s * v + w0h * qc
        den = s + w0h * jnp.sum(q * n0, axis=1, keepdims=True)
        hh = num / jnp.maximum(jnp.abs(den), einv[:, h:h + 1])
        nout_ref[:, h, :] = w0h * n0 + wh * k
        mu = jnp.mean(hh, axis=1, keepdims=True)
        hc = hh - mu
        var = jnp.mean(hc * hc, axis=1, keepdims=True)
        hn = hc * lax.rsqrt(var + LN_EPS) * mng_ref[lrow, cs]
        o = hs_ref[:, 3 * M_WIDTH + h * M_DK:3 * M_WIDTH + (h + 1) * M_DK]
        z = hs_ref[:, 4 * M_WIDTH + h * M_DK:4 * M_WIDTH + (h + 1) * M_DK]
        merged_ref[:, cs] = hn * _sigmoid(o) * _silu(z)

    cu = hs_ref[:, HM_W:HM_W + C_WIDTH]
    cg = hs_ref[:, HM_W + C_WIDTH:HM_W + 2 * C_WIDTH]
    cz = hs_ref[:, HM_W + 2 * C_WIDTH:HM_W + 3 * C_WIDTH]
    a = cu * _sigmoid(cg)
    yc = a * convw_ref[layer, CONV_BUF:CONV_K, :] + convb_ref[lrow, :]
    for t in range(CONV_BUF):
        yc = yc + conv_ref[t] * convw_ref[layer, t:t + 1, :]
    for t in range(CONV_BUF - 1):
        convout_ref[t] = conv_ref[t + 1]
    convout_ref[CONV_BUF - 1] = a
    yc = _layer_norm(yc, clng_ref[lrow, :], clnb_ref[lrow, :])
    merged_ref[:, M_WIDTH:M_WIDTH + C_WIDTH] = _silu(yc) * _silu(cz)

    ab = HM_W + HC_W
    lane = lax.broadcasted_iota(jnp.int32, (1, LANES), 1)
    rows_b = lax.broadcasted_iota(jnp.int32, (A_HEADS * SB, LANES), 0) % SB
    wcol = lax.broadcasted_iota(jnp.int32, (KV_WIDTH, WINDOW), 1)
    knew = hs_ref[:, ab + HA_K:ab + HA_K + KV_WIDTH]
    vnew = hs_ref[:, ab + HA_V:ab + HA_V + KV_WIDTH]
    q_parts, scn_parts, sink_parts = [], [], []
    for hq in range(A_HEADS):
        kvh = hq // GROUP
        t, u = hq // 2, hq % 2
        tile = hs_ref[:, ab + t * LANES:ab + (t + 1) * LANES]
        if u != kvh:
            tile = pltpu.roll(tile, HEAD_DIM, axis=1)
        qh = jnp.where((lane // HEAD_DIM) == kvh, tile, 0.0)
        q_parts.append(qh)
        scn_parts.append(jnp.sum(qh * knew, axis=1, keepdims=True))
        sink_parts.append(jnp.broadcast_to(sink_ref[lrow, hq:hq + 1], (SB, 1)))
    q_all = jnp.concatenate(q_parts, axis=0).astype(BF16)
    sc_new = jnp.concatenate(scn_parts, axis=0) * A_SCALE
    sink_all = jnp.concatenate(sink_parts, axis=0)
    sc = jnp.zeros((A_HEADS * SB, WINDOW), F32)
    for b in range(SB):
        sc = jnp.where(rows_b == b, _dot(q_all, ckt_ref[b].astype(BF16)), sc)
    sc = sc * A_SCALE
    smax = jnp.maximum(jnp.maximum(jnp.max(sc, axis=1, keepdims=True), sc_new), sink_all)
    e = jnp.exp(sc - smax)
    e_new = jnp.exp(sc_new - smax)
    denom = jnp.sum(e, axis=1, keepdims=True) + e_new + jnp.exp(sink_all - smax)
    p = (e / denom).astype(BF16)
    ob = (e_new / denom) * jnp.concatenate([vnew] * A_HEADS, axis=0)
    for b in range(SB):
        ob = ob + jnp.where(rows_b == b, _dot_nt(p, cvt_ref[b].astype(BF16)), 0.0)
    tiles = []
    for t in range(A_HEADS // 2):
        halves = []
        for u in range(2):
            hq = 2 * t + u
            x = ob[hq * SB:(hq + 1) * SB, :]
            if u != hq // GROUP:
                x = pltpu.roll(x, HEAD_DIM, axis=1)
            halves.append(x)
        tiles.append(jnp.where(lane < HEAD_DIM, halves[0], halves[1]))
    az = hs_ref[:, ab + HA_Z:ab + HA_Z + A_WIDTH]
    merged_ref[:, M_WIDTH + C_WIDTH:D_MODEL] = jnp.concatenate(tiles, axis=1) * _silu(az)
    knew_t = knew.T
    vnew_t = vnew.T
    for b in range(SB):
        kout_ref[b] = jnp.where(wcol == WINDOW - 1, knew_t[:, b:b + 1],
                                pltpu.roll(ckt_ref[b], WINDOW - 1, axis=1))
        vout_ref[b] = jnp.where(wcol == WINDOW - 1, vnew_t[:, b:b + 1],
                                pltpu.roll(cvt_ref[b], WINDOW - 1, axis=1))


def _sample_mixer(layer, hs, c, n, m, conv, ckt, cvt, prm):
    nb = hs.shape[0]
    row2 = lambda i: (i, 0)

    def state_specs(lead, pre):
        return [
            pl.BlockSpec(lead + (SB, M_HEADS, M_DK), lambda i: pre + (i, 0, 0)),
            pl.BlockSpec(lead + (SB, LANES), lambda i: pre + (i, 0)),
            pl.BlockSpec(lead + (CONV_BUF, SB, C_WIDTH), lambda i: pre + (0, i, 0)),
            pl.BlockSpec(lead + (SB, KV_WIDTH, WINDOW), lambda i: pre + (i, 0, 0)),
            pl.BlockSpec(lead + (SB, KV_WIDTH, WINDOW), lambda i: pre + (i, 0, 0)),
        ]

    c_specs = [pl.BlockSpec((None, SB, 1, M_DK, M_DK), lambda i, h=h: (layer, i, h, 0, 0))
               for h in range(M_HEADS)]
    state_shapes = [
        (nb, M_HEADS, M_DK), (nb, LANES),
        (CONV_BUF, nb, C_WIDTH), (nb, KV_WIDTH, WINDOW), (nb, KV_WIDTH, WINDOW)]
    params = [prm["gbias"], prm["mng"], prm["convw"], prm["convb"], prm["clng"], prm["clnb"],
              prm["sink_row"]]
    return pl.pallas_call(
        functools.partial(_sample_mixer_kernel, layer),
        grid=(nb // SB,),
        in_specs=([pl.BlockSpec((SB, PROJ_W), row2)] + c_specs + state_specs((None,), (layer,))
                  + [_whole(a) for a in params]),
        out_specs=[pl.BlockSpec((SB, D_MODEL), row2)] + state_specs((), ()),
        out_shape=[jax.ShapeDtypeStruct((nb, D_MODEL), F32)]
        + [jax.ShapeDtypeStruct(s, F32) for s in state_shapes],
        compiler_params=pltpu.CompilerParams(
            dimension_semantics=("arbitrary",), vmem_limit_bytes=VMEM_LIMIT),
        name="mixer_sample",
    )(hs, *([c] * M_HEADS), n, m, conv, ckt, cvt, *params)


def _update_c_kernel(hs_ref, c_ref, m_ref, gbias_ref, cout_ref):
    layer = pl.program_id(0)
    _, wgt, w0 = _sample_gates(hs_ref, gbias_ref[pl.ds(layer, 1), :], m_ref)
    for h in range(M_HEADS):
        k_t = (hs_ref[:, M_WIDTH + h * M_DK:M_WIDTH + (h + 1) * M_DK] * K_SCALE).T
        v = hs_ref[:, 2 * M_WIDTH + h * M_DK:2 * M_WIDTH + (h + 1) * M_DK]
        gv = wgt[:, h:h + 1] * v
        w0h = w0[:, h:h + 1]
        for b in range(SB):
            cout_ref[b, h] = w0h[b:b + 1, :] * c_ref[b, h] + k_t[:, b:b + 1] * gv[b:b + 1, :]


def _update_c(hs_all, c, m, prm):
    depth, nb = hs_all.shape[:2]
    per = lambda l, i: (l, i, 0)
    c_spec = pl.BlockSpec((None, SB, M_HEADS, M_DK, M_DK), lambda l, i: (l, i, 0, 0, 0))
    return pl.pallas_call(
        _update_c_kernel,
        grid=(depth, nb // SB),
        in_specs=[pl.BlockSpec((None, SB, PROJ_W), per), c_spec,
                  pl.BlockSpec((None, SB, LANES), per), _whole(prm["gbias"])],
        out_specs=c_spec,
        out_shape=jax.ShapeDtypeStruct(c.shape, F32),
        compiler_params=pltpu.CompilerParams(
            dimension_semantics=("arbitrary", "arbitrary"), vmem_limit_bytes=VMEM_LIMIT),
        name="update_matrix_memory",
    )(hs_all, c, m, prm["gbias"])


def _outproj_sample_kernel(layer, mg_ref, x_ref, wout_ref, lng_ref, lnb_ref, y_ref):
    lrow = slice(layer, layer + 1)
    y = _dot(mg_ref[...].astype(BF16), wout_ref[...])
    y_ref[...] = _layer_norm(ALPHA * x_ref[...] + y, lng_ref[lrow, :], lnb_ref[lrow, :])


def _outproj_sample(mg, xs, wout, prm, layer):
    nb = xs.shape[0]
    full = lambda shape: pl.BlockSpec(shape, lambda i: (0, 0))
    return pl.pallas_call(
        functools.partial(_outproj_sample_kernel, layer),
        grid=(1,),
        in_specs=[full((nb, D_MODEL)), full((nb, D_MODEL)),
                  full((D_MODEL, D_MODEL)),
                  _whole(prm["lng"]), _whole(prm["lnb"])],
        out_specs=full((nb, D_MODEL)),
        out_shape=jax.ShapeDtypeStruct((nb, D_MODEL), F32),
        compiler_params=pltpu.CompilerParams(
            dimension_semantics=("arbitrary",), vmem_limit_bytes=VMEM_LIMIT),
        name="outproj_sample",
    )(mg, xs, wout, prm["lng"], prm["lnb"])


def _lane_rows(*mats):
    v = jnp.concatenate([a.astype(F32) for a in mats], axis=1)
    return jnp.pad(v, ((0, 0), (0, LANES - v.shape[1])))


def _cache_rows_by_window(cache):
    depth, nb = cache.shape[:2]
    return jnp.transpose(cache, (0, 1, 3, 4, 2)).reshape(depth, nb, KV_WIDTH, WINDOW)


def _cache_window_major(cache_t):
    depth, nb = cache_t.shape[:2]
    return jnp.transpose(cache_t.reshape(depth, nb, KV_HEADS, HEAD_DIM, WINDOW), (0, 1, 4, 2, 3))


def kernel(x_prompt, x_sample, state_C, state_n, state_m, state_conv, cache_k, cache_v, w_in, w_out,
           b_igate, b_fgate, m_norm_g, conv_w, conv_b, conv_ln_g, conv_ln_b, sinks, ln_g, ln_b):
    batch, seq, _ = x_prompt.shape
    nb = x_sample.shape[0]
    xp = x_prompt.reshape(batch * seq, D_MODEL)
    xs = x_sample.reshape(nb, D_MODEL)
    wt = jnp.swapaxes(w_in, 1, 2)
    prm = {
        "gbias": _lane_rows(b_igate, b_fgate),
        "sink_row": _lane_rows(sinks),
        "mng": m_norm_g, "convw": conv_w, "convb": conv_b, "clng": conv_ln_g, "clnb": conv_ln_b,
        "lng": ln_g, "lnb": ln_b,
    }
    m_in = jnp.pad(state_m, ((0, 0), (0, 0), (0, LANES - M_HEADS)))
    sconv = jnp.swapaxes(state_conv, 1, 2)
    ckt = _cache_rows_by_window(cache_k)
    cvt = _cache_rows_by_window(cache_v)
    st_p = []
    st_s = []
    hs_all = []
    for l in range(DEPTH):
        pm, pc, pa, g, conv_p, hs, wout = _inproj(xp, xs, wt, w_out, prm, l, batch)
        xp, c_p, n_p, m_p, k_p, v_p = _mixer(l, sinks, pm, pc, pa, g, xp, wout, prm, batch, seq)
        st_p.append((c_p, n_p, m_p, conv_p, k_p, v_p))
        mg, *states = _sample_mixer(l, hs, state_C, state_n, m_in, sconv, ckt, cvt, prm)
        st_s.append(states)
        hs_all.append(hs)
        xs = _outproj_sample(mg, xs, wout, prm, l)
    c_s = _update_c(jnp.stack(hs_all, axis=0), state_C, m_in, prm)

    def stacked(per_layer):
        return [jnp.stack(a, axis=0) for a in zip(*per_layer)]

    c_p, n_p, m_p, conv_p, k_p, v_p = stacked(st_p)
    n_s, m_s, conv_s, k_s, v_s = stacked(st_s)
    return (xp.reshape(batch, seq, D_MODEL), xs.reshape(nb, 1, D_MODEL),
            c_p, n_p, m_p.reshape(DEPTH, batch, LANES)[:, :, :M_HEADS], conv_p,
            k_p.reshape(DEPTH, batch, WINDOW, KV_HEADS, HEAD_DIM),
            v_p.reshape(DEPTH, batch, WINDOW, KV_HEADS, HEAD_DIM),
            c_s, n_s, m_s[:, :, :M_HEADS], jnp.swapaxes(conv_s, 1, 2),
            _cache_window_major(k_s), _cache_window_major(v_s))
```

```python
import functools

import jax
import jax.numpy as jnp
from jax import lax
from jax.experimental import pallas as pl
from jax.experimental.pallas import tpu as pltpu

F32 = jnp.float32
BF16 = jnp.bfloat16

D_MODEL = 2048
DEPTH = 2
M_WIDTH = 1024
M_HEADS = 4
M_DK = 256
C_WIDTH = 512
CONV_K = 31
CONV_BUF = CONV_K - 1
A_WIDTH = 512
HEAD_DIM = 64
A_HEADS = 8
KV_HEADS = 2
GROUP = A_HEADS // KV_HEADS
KV_WIDTH = KV_HEADS * HEAD_DIM
WINDOW = 128
ALPHA = (2 * DEPTH) ** 0.25
LN_EPS = 1e-5
K_SCALE = M_DK ** -0.5
A_SCALE = HEAD_DIM ** -0.5

LANES = 128
SUBLANES = 8
BF16_SUBLANES = 2 * SUBLANES

HM_W = 5 * M_WIDTH
HC_W = 3 * C_WIDTH
HA_W = 2 * A_WIDTH + 2 * KV_WIDTH
MAIN_W = HM_W + HC_W + HA_W
GATE_END = MAIN_W + LANES
PROJ_W = GATE_END
GATE_OFF = 4 * M_WIDTH
N_GATES = 2 * M_HEADS
W_CHUNK = 256
W_SLOTS = 4
PLAIN_CHUNKS = GATE_OFF // W_CHUNK
MAIN_CHUNKS = MAIN_W // W_CHUNK
HA_K = A_WIDTH
HA_V = A_WIDTH + KV_WIDTH
HA_Z = A_WIDTH + 2 * KV_WIDTH

TM_PROJ = 256
TB = 256
A_BLK = 128
CONV_HALO = 32
SB = 8
VMEM_LIMIT = 56 * 1024 * 1024
INPROJ_VMEM_LIMIT = 60 * 1024 * 1024


def _sigmoid(x):
    return 1.0 / (1.0 + jnp.exp(-x))


def _silu(x):
    return x * _sigmoid(x)


def _log_sigmoid(x):
    return jnp.minimum(x, 0.0) - jnp.log1p(jnp.exp(-jnp.abs(x)))


def _layer_norm(x, g, b):
    mu = jnp.mean(x, axis=-1, keepdims=True)
    xc = x - mu
    var = jnp.mean(xc * xc, axis=-1, keepdims=True)
    return xc * lax.rsqrt(var + LN_EPS) * g + b


def _dot(a, b):
    return jnp.dot(a, b, preferred_element_type=F32)


def _dot_nt(a, b):
    return lax.dot_general(a, b, (((1,), (1,)), ((), ())), preferred_element_type=F32)


def _dot_tn(a, b):
    return lax.dot_general(a, b, (((0,), (0,)), ((), ())), preferred_element_type=F32)


def _whole(arr):
    zeros = (0,) * arr.ndim
    return pl.BlockSpec(arr.shape, lambda *_: zeros)


PM_W = 4 * M_WIDTH
PA_QKV = A_WIDTH + 2 * KV_WIDTH
QKV_CHUNK = 512
SHIFTED_ROWS = CONV_HALO + TM_PROJ - SUBLANES


def _stage_rows(src_hbm, dst_ref, stage, sem, n_chunks, src_row):
    def chunk_copy(c, slot):
        return pltpu.make_async_copy(
            src_hbm.at[pl.ds(pl.multiple_of(src_row(c), SUBLANES), W_CHUNK), :],
            stage.at[slot], sem.at[slot])

    for c in range(W_SLOTS - 1):
        chunk_copy(c, c).start()

    def body(c, carry):
        slot = c % W_SLOTS
        chunk_copy(c, slot).wait()
        ahead = c + W_SLOTS - 1

        @pl.when(ahead < n_chunks)
        def _():
            chunk_copy(ahead, ahead % W_SLOTS).start()

        dst_ref[pl.ds(pl.multiple_of(c * W_CHUNK, W_CHUNK), W_CHUNK), :] = stage[slot].astype(BF16)
        return carry

    lax.fori_loop(0, n_chunks, body, 0)


def _stage_w_in(w_hbm, wt_ref, stage, sem):
    _stage_rows(w_hbm, wt_ref, stage, sem, MAIN_CHUNKS,
                lambda c: jnp.where(c < PLAIN_CHUNKS, c * W_CHUNK, c * W_CHUNK + N_GATES))
    stage[0, 0:LANES, :] = jnp.zeros((LANES, D_MODEL), F32)
    gate_copy = pltpu.make_async_copy(
        w_hbm.at[GATE_OFF:GATE_OFF + N_GATES, :], stage.at[0, 0:N_GATES, :], sem.at[0])
    gate_copy.start()
    gate_copy.wait()
    wt_ref[MAIN_W:GATE_END, :] = stage[0, 0:LANES, :].astype(BF16)


def _inproj_kernel(layer, blocks_per_seq, x_ref, xs_ref, w_hbm, wo_in_ref, convw_ref, convb_ref,
                   clng_ref, clnb_ref, pm_ref, pc_ref, pa_ref, g_ref, conv_ref, hs_ref, wo_ref,
                   wt_ref, stage, sem, xbuf, abuf, sbuf, ybuf, zbuf):
    i = pl.program_id(0)
    lrow = slice(layer, layer + 1)

    n_sample = xs_ref.shape[0]

    @pl.when(i % blocks_per_seq == 0)
    def _():
        abuf[0:CONV_HALO, :] = jnp.zeros((CONV_HALO, C_WIDTH), F32)

    xbuf[0:TM_PROJ, :] = x_ref[...].astype(BF16)
    wo_ref[...] = wo_in_ref[...].astype(BF16)

    def project_block(n_extra):
        n_rows = TM_PROJ + n_extra

        def proj(c0, c1):
            full = _dot_nt(xbuf[0:n_rows, :], wt_ref[c0:c1, :])
            if n_extra:
                hs_ref[:, c0:c1] = full[TM_PROJ:n_rows, :]
                return full[0:TM_PROJ, :]
            return full

        for c in range(0, 3 * M_WIDTH, QKV_CHUNK):
            pm_ref[:, c:c + QKV_CHUNK] = proj(c, c + QKV_CHUNK).astype(BF16)
        for h in range(M_HEADS):
            o = proj(3 * M_WIDTH + h * M_DK, 3 * M_WIDTH + (h + 1) * M_DK)
            z = proj(4 * M_WIDTH + h * M_DK, 4 * M_WIDTH + (h + 1) * M_DK)
            pm_ref[:, 3 * M_WIDTH + h * M_DK:3 * M_WIDTH + (h + 1) * M_DK] = (
                _sigmoid(o) * _silu(z)).astype(BF16)

        cu = proj(HM_W, HM_W + C_WIDTH)
        cg = proj(HM_W + C_WIDTH, HM_W + 2 * C_WIDTH)
        abuf[CONV_HALO:CONV_HALO + TM_PROJ, :] = cu * _sigmoid(cg)
        zbuf[...] = _silu(proj(HM_W + 2 * C_WIDTH, HM_W + 3 * C_WIDTH))
        for p in range(1, SUBLANES):
            sbuf[p - 1] = abuf[p:p + SHIFTED_ROWS, :]
        tap0 = CONV_HALO - CONV_BUF
        for r in range(0, TM_PROJ, 128):
            for c in range(0, C_WIDTH, LANES):
                acc = jnp.zeros((128, LANES), F32)
                for t in range(CONV_K):
                    p = (tap0 + t) % SUBLANES
                    base = r + tap0 + t - p
                    src = abuf if p == 0 else sbuf.at[p - 1]
                    acc = acc + (src[base:base + 128, c:c + LANES]
                                 * convw_ref[layer, t:t + 1, c:c + LANES])
                ybuf[r:r + 128, c:c + LANES] = acc + convb_ref[lrow, c:c + LANES]
        conv_ref[0] = abuf[CONV_HALO + TM_PROJ - CONV_BUF:CONV_HALO + TM_PROJ, :]
        abuf[0:CONV_HALO, :] = abuf[TM_PROJ:TM_PROJ + CONV_HALO, :]
        yc = _layer_norm(ybuf[...], clng_ref[lrow, :], clnb_ref[lrow, :])
        pc_ref[...] = (_silu(yc) * zbuf[...]).astype(BF16)

        base = HM_W + HC_W
        pa_ref[:, 0:PA_QKV] = proj(base, base + PA_QKV).astype(BF16)
        pa_ref[:, PA_QKV:HA_W] = _silu(proj(base + PA_QKV, base + HA_W)).astype(BF16)
        g_ref[...] = proj(MAIN_W, GATE_END)

    @pl.when(i == 0)
    def _():
        _stage_w_in(w_hbm.at[layer], wt_ref, stage, sem)
        xbuf[TM_PROJ:TM_PROJ + n_sample, :] = xs_ref[...].astype(BF16)
        project_block(n_sample)

    @pl.when(i > 0)
    def _():
        project_block(0)


def _inproj(x2d, xs, w_in_t, w_out, prm, layer, batch):
    m = x2d.shape[0]
    ns = xs.shape[0]
    nblk = m // TM_PROJ
    assert D_MODEL % (nblk * BF16_SUBLANES) == 0
    wo_rows = D_MODEL // nblk
    row = lambda i: (i, 0)
    once = lambda i: (0, 0)
    params = [prm["convw"], prm["convb"], prm["clng"], prm["clnb"]]
    return pl.pallas_call(
        functools.partial(_inproj_kernel, layer, nblk // batch),
        grid=(nblk,),
        in_specs=[
            pl.BlockSpec((TM_PROJ, D_MODEL), row),
            pl.BlockSpec((ns, D_MODEL), once),
            pl.BlockSpec(memory_space=pl.ANY),
            pl.BlockSpec((None, wo_rows, D_MODEL), lambda i: (layer, i, 0)),
        ] + [_whole(a) for a in params],
        out_specs=[
            pl.BlockSpec((TM_PROJ, PM_W), row),
            pl.BlockSpec((TM_PROJ, C_WIDTH), row),
            pl.BlockSpec((TM_PROJ, HA_W), row),
            pl.BlockSpec((TM_PROJ, LANES), row),
            pl.BlockSpec((1, CONV_BUF, C_WIDTH), lambda i: (i // (nblk // batch), 0, 0)),
            pl.BlockSpec((ns, PROJ_W), once),
            pl.BlockSpec((wo_rows, D_MODEL), row),
        ],
        out_shape=[
            jax.ShapeDtypeStruct((m, PM_W), BF16),
            jax.ShapeDtypeStruct((m, C_WIDTH), BF16),
            jax.ShapeDtypeStruct((m, HA_W), BF16),
            jax.ShapeDtypeStruct((m, LANES), F32),
            jax.ShapeDtypeStruct((batch, CONV_BUF, C_WIDTH), F32),
            jax.ShapeDtypeStruct((ns, PROJ_W), F32),
            jax.ShapeDtypeStruct((D_MODEL, D_MODEL), BF16),
        ],
        scratch_shapes=[
            pltpu.VMEM((PROJ_W, D_MODEL), BF16),
            pltpu.VMEM((W_SLOTS, W_CHUNK, D_MODEL), F32),
            pltpu.SemaphoreType.DMA((W_SLOTS,)),
            pltpu.VMEM((TM_PROJ + ns, D_MODEL), BF16),
            pltpu.VMEM((CONV_HALO + TM_PROJ, C_WIDTH), F32),
            pltpu.VMEM((SUBLANES - 1, SHIFTED_ROWS, C_WIDTH), F32),
            pltpu.VMEM((TM_PROJ, C_WIDTH), F32),
            pltpu.VMEM((TM_PROJ, C_WIDTH), F32),
        ],
        compiler_params=pltpu.CompilerParams(
            dimension_semantics=("arbitrary",), vmem_limit_bytes=INPROJ_VMEM_LIMIT),
        name="inproj_prompt",
    )(x2d, xs, w_in_t, w_out, *params)


def _cumsum_rows(x):
    n = x.shape[0]
    row = lax.broadcasted_iota(jnp.int32, x.shape, 0)
    s = 1
    while s < n:
        x = x + jnp.where(row >= s, pltpu.roll(x, s, axis=0), 0.0)
        s *= 2
    return x


def _update_c_unit(unit, groups, ks_ref, vs_ref, gs_ref, ms_ref, gbias_ref, cs_ref, cs_out_ref):
    layer = unit // (groups * M_HEADS)
    h = unit % M_HEADS
    _, wgt, w0 = _sample_gates(gs_ref[...], gbias_ref[pl.ds(layer, 1), :], ms_ref[...])
    lane = lax.broadcasted_iota(jnp.int32, (SB, LANES), 1)
    wh = jnp.sum(jnp.where(lane == h, wgt, 0.0), axis=1, keepdims=True)
    w0h = jnp.sum(jnp.where(lane == h, w0, 0.0), axis=1, keepdims=True)
    k_t = (ks_ref[...] * K_SCALE).T
    gv = wh * vs_ref[...]
    for b in range(SB):
        cs_out_ref[b] = w0h[b:b + 1, :] * cs_ref[b] + k_t[:, b:b + 1] * gv[b:b + 1, :]


def _mixer_kernel(layer, n_blocks, blocks_per_seq, n_units, groups, sinks_ref, hm_ref, pc_ref,
                  ha_ref, g_ref, xprev_ref, wout_ref, gbias_ref, mng_ref, lng_ref, lnb_ref, *refs):
    sample_in, refs = (refs[:5], refs[5:]) if n_units else ((), refs)
    y_ref, c_ref, n_ref, m_ref, k_ref, v_ref = refs[:6]
    sample_out, refs = (refs[6:7], refs[7:]) if n_units else ((), refs[6:])
    kprev, vprev, merged, merged_prev = refs
    step = pl.program_id(0)
    j = jnp.minimum(step, n_blocks - 1) % blocks_per_seq
    lrow = slice(layer, layer + 1)

    @pl.when(step == 0)
    def _():
        merged_prev[...] = jnp.zeros_like(merged_prev)

    @pl.when(j == 0)
    def _():
        c_ref[...] = jnp.zeros_like(c_ref)
        n_ref[...] = jnp.zeros_like(n_ref)
        m_ref[...] = jnp.zeros_like(m_ref)
        kprev[...] = jnp.zeros_like(kprev)
        vprev[...] = jnp.zeros_like(vprev)

    y = _dot(merged_prev[...], wout_ref[...])
    y_ref[...] = _layer_norm(ALPHA * xprev_ref[...] + y, lng_ref[lrow, :], lnb_ref[lrow, :])

    gb = g_ref[...] + gbias_ref[lrow, :]
    bsum = _cumsum_rows(_log_sigmoid(gb))
    bsum = pltpu.roll(bsum, LANES - M_HEADS, axis=1)
    a_all = gb - bsum
    a_t = a_all.T
    m_prev = m_ref[0]
    row_i = lax.broadcasted_iota(jnp.int32, (TB, TB), 0)
    col_i = lax.broadcasted_iota(jnp.int32, (TB, TB), 1)
    causal = col_i <= row_i
    lane_i = lax.broadcasted_iota(jnp.int32, (1, LANES), 1)
    m_new = m_prev

    heads = range(M_HEADS)
    cs = [slice(h * M_DK, (h + 1) * M_DK) for h in heads]
    q_of = lambda h: hm_ref[:, cs[h]]
    k_of = lambda h: hm_ref[:, M_WIDTH + h * M_DK:M_WIDTH + (h + 1) * M_DK]
    v_of = lambda h: hm_ref[:, 2 * M_WIDTH + h * M_DK:2 * M_WIDTH + (h + 1) * M_DK]
    m0 = [m_prev[:, h:h + 1] for h in heads]
    n0 = [n_ref[0, h:h + 1, :] for h in heads]
    logit = [jnp.where(causal, a_t[h:h + 1, :], -jnp.inf) for h in heads]
    mx = [jnp.maximum(jnp.max(logit[h], axis=1, keepdims=True), m0[h]) for h in heads]
    w0 = [jnp.exp(m0[h] - mx[h]) for h in heads]
    s = [_dot_nt(q_of(h), k_of(h)) * (jnp.exp(logit[h] - mx[h]) * K_SCALE) for h in heads]
    num = [_dot(s[h].astype(BF16), v_of(h)) + w0[h] * _dot(q_of(h), c_ref[0, h].astype(BF16))
           for h in heads]
    den = [jnp.sum(s[h], axis=1, keepdims=True)
           + w0[h] * jnp.sum(q_of(h).astype(F32) * n0[h], axis=1, keepdims=True) for h in heads]
    hh = [num[h] / jnp.maximum(jnp.abs(den[h]), jnp.exp(-(bsum[:, h:h + 1] + mx[h])))
          for h in heads]
    mu = [jnp.mean(hh[h], axis=1, keepdims=True) for h in heads]
    hc = [hh[h] - mu[h] for h in heads]
    var = [jnp.mean(hc[h] * hc[h], axis=1, keepdims=True) for h in heads]
    for h in heads:
        hn = hc[h] * lax.rsqrt(var[h] + LN_EPS) * mng_ref[lrow, cs[h]]
        gate = hm_ref[:, 3 * M_WIDTH + h * M_DK:3 * M_WIDTH + (h + 1) * M_DK].astype(F32)
        merged[:, cs[h]] = (hn * gate).astype(BF16)
    c_new, n_new = [], []
    for h in heads:
        mx_end = mx[h][TB - 1:TB, :]
        g_col = jnp.exp(a_all[:, h:h + 1] - mx_end) * K_SCALE
        g0 = jnp.exp(m0[h] - mx_end)
        kg = k_of(h).astype(F32) * g_col
        c_new.append(g0 * c_ref[0, h] + _dot_tn(kg.astype(BF16), v_of(h)))
        n_new.append(g0 * n0[h] + jnp.sum(kg, axis=0, keepdims=True))
        m_end = bsum[TB - 1:TB, h:h + 1] + mx_end
        m_new = jnp.where(lane_i == h, m_end, m_new)

    merged[:, M_WIDTH:M_WIDTH + C_WIDTH] = pc_ref[...]

    arow = lax.broadcasted_iota(jnp.int32, (A_BLK, 2 * A_BLK), 0)
    acol = lax.broadcasted_iota(jnp.int32, (A_BLK, 2 * A_BLK), 1)
    rel = A_BLK + arow - acol
    band = (rel >= 0) & (rel <= WINDOW)
    n_sub = TB // A_BLK
    kcat, vext, mask = [], [], []
    ones = jnp.ones((2 * A_BLK, HEAD_DIM), BF16)
    for sub in range(n_sub):
        r0 = sub * A_BLK
        if sub == 0:
            kcat.append(jnp.concatenate(
                [kprev[...], ha_ref[0:A_BLK, HA_K:HA_K + KV_WIDTH]], axis=0))
            vcat = jnp.concatenate([vprev[...], ha_ref[0:A_BLK, HA_V:HA_V + KV_WIDTH]], axis=0)
            mask.append(band & (acol >= jnp.where(j > 0, 0, A_BLK)))
        else:
            kcat.append(ha_ref[r0 - A_BLK:r0 + A_BLK, HA_K:HA_K + KV_WIDTH])
            vcat = ha_ref[r0 - A_BLK:r0 + A_BLK, HA_V:HA_V + KV_WIDTH]
            mask.append(band)
        vext.append([jnp.concatenate([vcat[:, kvh * HEAD_DIM:(kvh + 1) * HEAD_DIM], ones], axis=1)
                     for kvh in range(KV_HEADS)])
    units = [(sub, hq) for sub in range(n_sub) for hq in range(A_HEADS)]
    sink = {hq: sinks_ref[layer, hq] for hq in range(A_HEADS)}
    sc = {}
    for sub, hq in units:
        kvh = hq // GROUP
        qh = ha_ref[sub * A_BLK:(sub + 1) * A_BLK, hq * HEAD_DIM:(hq + 1) * HEAD_DIM]
        kh = kcat[sub][:, kvh * HEAD_DIM:(kvh + 1) * HEAD_DIM]
        sc[sub, hq] = jnp.where(mask[sub], _dot_nt(qh, kh) * A_SCALE, -jnp.inf)
    smax = {u: jnp.maximum(jnp.max(sc[u], axis=1, keepdims=True), sink[u[1]]) for u in units}
    pv = {u: _dot(jnp.exp(sc[u] - smax[u]).astype(BF16), vext[u[0]][u[1] // GROUP])
          for u in units}
    outs = {u: pv[u][:, 0:HEAD_DIM]
            / (pv[u][:, HEAD_DIM:HEAD_DIM + 1] + jnp.exp(sink[u[1]] - smax[u]))
            for u in units}
    for sub in range(n_sub):
        r0 = sub * A_BLK
        for hp in range(A_HEADS // 2):
            ao = jnp.concatenate([outs[sub, 2 * hp], outs[sub, 2 * hp + 1]], axis=1)
            gate = ha_ref[r0:r0 + A_BLK, HA_Z + hp * LANES:HA_Z + (hp + 1) * LANES].astype(F32)
            col = M_WIDTH + C_WIDTH + hp * LANES
            merged[r0:r0 + A_BLK, col:col + LANES] = (ao * gate).astype(BF16)
    merged_prev[...] = merged[...]

    @pl.when(step < n_blocks)
    def _():
        for h in heads:
            c_ref[0, h] = c_new[h]
        n_ref[0] = jnp.concatenate(n_new, axis=0)
        m_ref[0] = m_new
        kprev[...] = ha_ref[TB - A_BLK:TB, HA_K:HA_K + KV_WIDTH]
        vprev[...] = ha_ref[TB - A_BLK:TB, HA_V:HA_V + KV_WIDTH]
        k_ref[0] = ha_ref[TB - WINDOW:TB, HA_K:HA_K + KV_WIDTH].astype(F32)
        v_ref[0] = ha_ref[TB - WINDOW:TB, HA_V:HA_V + KV_WIDTH].astype(F32)

    if n_units:
        @pl.when(step < n_units)
        def _():
            ks_ref, vs_ref, gs_ref, ms_ref, cs_ref = sample_in
            _update_c_unit(step, groups, ks_ref, vs_ref, gs_ref, ms_ref, gbias_ref, cs_ref,
                           sample_out[0])


def _mixer(layer, sinks, pm, pc, pa, g, x2d, wout, prm, batch, seq, sample=None):
    nb = seq // TB
    nblk = batch * nb
    n_units, groups, extra_in, extra_specs, extra_out, extra_shape = 0, 0, [], [], [], []
    if sample is not None:
        hs_all, c_s, m_s = sample
        depth, ns = hs_all.shape[:2]
        groups = ns // SB
        n_units = depth * groups * M_HEADS
        assert n_units <= nblk and MAIN_W % LANES == 0

        def unit(s):
            u = jnp.minimum(s, n_units - 1)
            return u // (groups * M_HEADS), (u // M_HEADS) % groups, u % M_HEADS

        def hs_cols(width, first):
            return pl.BlockSpec((None, SB, width),
                                lambda s: unit(s)[:2] + (first // width + unit(s)[2],))

        c_spec = pl.BlockSpec((None, SB, None, M_DK, M_DK), lambda s: unit(s) + (0, 0))
        extra_in = [hs_all, hs_all, hs_all, m_s, c_s]
        extra_specs = [
            hs_cols(M_DK, M_WIDTH), hs_cols(M_DK, 2 * M_WIDTH),
            pl.BlockSpec((None, SB, LANES), lambda s: unit(s)[:2] + (MAIN_W // LANES,)),
            pl.BlockSpec((None, SB, LANES), lambda s: unit(s)[:2] + (0,)),
            c_spec]
        extra_out = [c_spec]
        extra_shape = [jax.ShapeDtypeStruct(c_s.shape, F32)]
    mixed = lambda s: (jnp.minimum(s, nblk - 1), 0)
    done = lambda s: (jnp.maximum(s - 1, 0), 0)
    seq_of = lambda s: jnp.minimum(s, nblk - 1) // nb
    per_b3 = lambda s: (seq_of(s), 0, 0)
    state_shapes = [
        (batch, M_HEADS, M_DK, M_DK), (batch, M_HEADS, M_DK), (batch, 1, LANES),
        (batch, WINDOW, KV_WIDTH), (batch, WINDOW, KV_WIDTH)]
    params = [prm["gbias"], prm["mng"], prm["lng"], prm["lnb"]]
    return pl.pallas_call(
        functools.partial(_mixer_kernel, layer, nblk, nb, n_units, groups),
        grid=(nblk + 1,),
        in_specs=[
            pl.BlockSpec(memory_space=pltpu.SMEM),
            pl.BlockSpec((TB, PM_W), mixed),
            pl.BlockSpec((TB, C_WIDTH), mixed),
            pl.BlockSpec((TB, HA_W), mixed),
            pl.BlockSpec((TB, LANES), mixed),
            pl.BlockSpec((TB, D_MODEL), done),
            pl.BlockSpec((D_MODEL, D_MODEL), lambda s: (0, 0)),
        ] + [_whole(a) for a in params] + extra_specs,
        out_specs=[
            pl.BlockSpec((TB, D_MODEL), done),
            pl.BlockSpec((1, M_HEADS, M_DK, M_DK), lambda s: (seq_of(s), 0, 0, 0)),
            pl.BlockSpec((1, M_HEADS, M_DK), per_b3),
            pl.BlockSpec((1, 1, LANES), per_b3),
            pl.BlockSpec((1, WINDOW, KV_WIDTH), per_b3),
            pl.BlockSpec((1, WINDOW, KV_WIDTH), per_b3),
        ] + extra_out,
        out_shape=[jax.ShapeDtypeStruct((batch * seq, D_MODEL), F32)]
        + [jax.ShapeDtypeStruct(s, F32) for s in state_shapes] + extra_shape,
        scratch_shapes=[
            pltpu.VMEM((A_BLK, KV_WIDTH), BF16),
            pltpu.VMEM((A_BLK, KV_WIDTH), BF16),
            pltpu.VMEM((TB, D_MODEL), BF16),
            pltpu.VMEM((TB, D_MODEL), BF16),
        ],
        compiler_params=pltpu.CompilerParams(
            dimension_semantics=("arbitrary",), vmem_limit_bytes=VMEM_LIMIT),
        name="mixer_prompt",
    )(sinks, pm, pc, pa, g, x2d, wout, *params, *extra_in)


def _sample_gates(gates, gbias_row, m0):
    gb = gates + gbias_row
    lf = pltpu.roll(_log_sigmoid(gb), LANES - M_HEADS, axis=1)
    m_new = jnp.maximum(lf + m0, gb)
    wgt = jnp.exp(gb - m_new)
    w0 = jnp.exp(lf + m0 - m_new)
    return m_new, wgt, w0


def _sample_mixer_kernel(layer, hs_ref, c0_ref, c1_ref, c2_ref, c3_ref, n_ref, m_ref, conv_ref,
                         ckt_ref, cvt_ref, gbias_ref, mng_ref, convw_ref, convb_ref, clng_ref,
                         clnb_ref, sink_ref,
                         merged_ref, nout_ref, mout_ref, convout_ref, kout_ref, vout_ref):
    lrow = slice(layer, layer + 1)
    c_heads = (c0_ref, c1_ref, c2_ref, c3_ref)
    m_new, wgt, w0 = _sample_gates(hs_ref[:, MAIN_W:GATE_END], gbias_ref[lrow, :], m_ref[...])
    einv = jnp.exp(-m_new)
    mout_ref[...] = m_new
    for h in range(M_HEADS):
        cs = slice(h * M_DK, (h + 1) * M_DK)
        q = hs_ref[:, cs]
        k = hs_ref[:, M_WIDTH + h * M_DK:M_WIDTH + (h + 1) * M_DK] * K_SCALE
        v = hs_ref[:, 2 * M_WIDTH + h * M_DK:2 * M_WIDTH + (h + 1) * M_DK]
        q_t = q.T
        wh = wgt[:, h:h + 1]
        w0h = w0[:, h:h + 1]
        n0 = n_ref[:, h, :]
        s = jnp.sum(q * k, axis=1, keepdims=True) * wh
        qc = jnp.concatenate(
            [jnp.sum(q_t[:, b:b + 1] * c_heads[h][b, 0], axis=0, keepdims=True)
             for b in range(SB)],
            axis=0)
        num = s * v + w0h * qc
        den = s + w0h * jnp.sum(q * n0, axis=1, keepdims=True)
        hh = num / jnp.maximum(jnp.abs(den), einv[:, h:h + 1])
        nout_ref[:, h, :] = w0h * n0 + wh * k
        mu = jnp.mean(hh, axis=1, keepdims=True)
        hc = hh - mu
        var = jnp.mean(hc * hc, axis=1, keepdims=True)
        hn = hc * lax.rsqrt(var + LN_EPS) * mng_ref[lrow, cs]
        o = hs_ref[:, 3 * M_WIDTH + h * M_DK:3 * M_WIDTH + (h + 1) * M_DK]
        z = hs_ref[:, 4 * M_WIDTH + h * M_DK:4 * M_WIDTH + (h + 1) * M_DK]
        merged_ref[:, cs] = hn * _sigmoid(o) * _silu(z)

    cu = hs_ref[:, HM_W:HM_W + C_WIDTH]
    cg = hs_ref[:, HM_W + C_WIDTH:HM_W + 2 * C_WIDTH]
    cz = hs_ref[:, HM_W + 2 * C_WIDTH:HM_W + 3 * C_WIDTH]
    a = cu * _sigmoid(cg)
    yc = a * convw_ref[layer, CONV_BUF:CONV_K, :] + convb_ref[lrow, :]
    for t in range(CONV_BUF):
        yc = yc + conv_ref[t] * convw_ref[layer, t:t + 1, :]
    for t in range(CONV_BUF - 1):
        convout_ref[t] = conv_ref[t + 1]
    convout_ref[CONV_BUF - 1] = a
    yc = _layer_norm(yc, clng_ref[lrow, :], clnb_ref[lrow, :])
    merged_ref[:, M_WIDTH:M_WIDTH + C_WIDTH] = _silu(yc) * _silu(cz)

    ab = HM_W + HC_W
    lane = lax.broadcasted_iota(jnp.int32, (1, LANES), 1)
    rows_b = lax.broadcasted_iota(jnp.int32, (A_HEADS * SB, LANES), 0) % SB
    wcol = lax.broadcasted_iota(jnp.int32, (KV_WIDTH, WINDOW), 1)
    knew = hs_ref[:, ab + HA_K:ab + HA_K + KV_WIDTH]
    vnew = hs_ref[:, ab + HA_V:ab + HA_V + KV_WIDTH]
    q_parts, scn_parts, sink_parts = [], [], []
    for hq in range(A_HEADS):
        kvh = hq // GROUP
        t, u = hq // 2, hq % 2
        tile = hs_ref[:, ab + t * LANES:ab + (t + 1) * LANES]
        if u != kvh:
            tile = pltpu.roll(tile, HEAD_DIM, axis=1)
        qh = jnp.where((lane // HEAD_DIM) == kvh, tile, 0.0)
        q_parts.append(qh)
        scn_parts.append(jnp.sum(qh * knew, axis=1, keepdims=True))
        sink_parts.append(jnp.broadcast_to(sink_ref[lrow, hq:hq + 1], (SB, 1)))
    q_all = jnp.concatenate(q_parts, axis=0).astype(BF16)
    sc_new = jnp.concatenate(scn_parts, axis=0) * A_SCALE
    sink_all = jnp.concatenate(sink_parts, axis=0)
    sc = jnp.zeros((A_HEADS * SB, WINDOW), F32)
    for b in range(SB):
        sc = jnp.where(rows_b == b, _dot(q_all, ckt_ref[b].astype(BF16)), sc)
    sc = sc * A_SCALE
    smax = jnp.maximum(jnp.maximum(jnp.max(sc, axis=1, keepdims=True), sc_new), sink_all)
    e = jnp.exp(sc - smax)
    e_new = jnp.exp(sc_new - smax)
    denom = jnp.sum(e, axis=1, keepdims=True) + e_new + jnp.exp(sink_all - smax)
    p = (e / denom).astype(BF16)
    ob = (e_new / denom) * jnp.concatenate([vnew] * A_HEADS, axis=0)
    for b in range(SB):
        ob = ob + jnp.where(rows_b == b, _dot_nt(p, cvt_ref[b].astype(BF16)), 0.0)
    tiles = []
    for t in range(A_HEADS // 2):
        halves = []
        for u in range(2):
            hq = 2 * t + u
            x = ob[hq * SB:(hq + 1) * SB, :]
            if u != hq // GROUP:
                x = pltpu.roll(x, HEAD_DIM, axis=1)
            halves.append(x)
        tiles.append(jnp.where(lane < HEAD_DIM, halves[0], halves[1]))
    az = hs_ref[:, ab + HA_Z:ab + HA_Z + A_WIDTH]
    merged_ref[:, M_WIDTH + C_WIDTH:D_MODEL] = jnp.concatenate(tiles, axis=1) * _silu(az)
    knew_t = knew.T
    vnew_t = vnew.T
    for b in range(SB):
        kout_ref[b] = jnp.where(wcol == WINDOW - 1, knew_t[:, b:b + 1],
                                pltpu.roll(ckt_ref[b], WINDOW - 1, axis=1))
        vout_ref[b] = jnp.where(wcol == WINDOW - 1, vnew_t[:, b:b + 1],
                                pltpu.roll(cvt_ref[b], WINDOW - 1, axis=1))


def _sample_mixer(layer, hs, c, n, m, conv, ckt, cvt, prm):
    nb = hs.shape[0]
    row2 = lambda i: (i, 0)

    def state_specs(lead, pre):
        return [
            pl.BlockSpec(lead + (SB, M_HEADS, M_DK), lambda i: pre + (i, 0, 0)),
            pl.BlockSpec(lead + (SB, LANES), lambda i: pre + (i, 0)),
            pl.BlockSpec(lead + (CONV_BUF, SB, C_WIDTH), lambda i: pre + (0, i, 0)),
            pl.BlockSpec(lead + (SB, KV_WIDTH, WINDOW), lambda i: pre + (i, 0, 0)),
            pl.BlockSpec(lead + (SB, KV_WIDTH, WINDOW), lambda i: pre + (i, 0, 0)),
        ]

    c_specs = [pl.BlockSpec((None, SB, 1, M_DK, M_DK), lambda i, h=h: (layer, i, h, 0, 0))
               for h in range(M_HEADS)]
    state_shapes = [
        (nb, M_HEADS, M_DK), (nb, LANES),
        (CONV_BUF, nb, C_WIDTH), (nb, KV_WIDTH, WINDOW), (nb, KV_WIDTH, WINDOW)]
    params = [prm["gbias"], prm["mng"], prm["convw"], prm["convb"], prm["clng"], prm["clnb"],
              prm["sink_row"]]
    return pl.pallas_call(
        functools.partial(_sample_mixer_kernel, layer),
        grid=(nb // SB,),
        in_specs=([pl.BlockSpec((SB, PROJ_W), row2)] + c_specs + state_specs((None,), (layer,))
                  + [_whole(a) for a in params]),
        out_specs=[pl.BlockSpec((SB, D_MODEL), row2)] + state_specs((), ()),
        out_shape=[jax.ShapeDtypeStruct((nb, D_MODEL), F32)]
        + [jax.ShapeDtypeStruct(s, F32) for s in state_shapes],
        compiler_params=pltpu.CompilerParams(
            dimension_semantics=("arbitrary",), vmem_limit_bytes=VMEM_LIMIT),
        name="mixer_sample",
    )(hs, *([c] * M_HEADS), n, m, conv, ckt, cvt, *params)


def _outproj_sample_kernel(layer, mg_ref, x_ref, wout_ref, lng_ref, lnb_ref, y_ref):
    lrow = slice(layer, layer + 1)
    y = _dot(mg_ref[...].astype(BF16), wout_ref[...])
    y_ref[...] = _layer_norm(ALPHA * x_ref[...] + y, lng_ref[lrow, :], lnb_ref[lrow, :])


def _outproj_sample(mg, xs, wout, prm, layer):
    nb = xs.shape[0]
    full = lambda shape: pl.BlockSpec(shape, lambda i: (0, 0))
    return pl.pallas_call(
        functools.partial(_outproj_sample_kernel, layer),
        grid=(1,),
        in_specs=[full((nb, D_MODEL)), full((nb, D_MODEL)),
                  full((D_MODEL, D_MODEL)),
                  _whole(prm["lng"]), _whole(prm["lnb"])],
        out_specs=full((nb, D_MODEL)),
        out_shape=jax.ShapeDtypeStruct((nb, D_MODEL), F32),
        compiler_params=pltpu.CompilerParams(
            dimension_semantics=("arbitrary",), vmem_limit_bytes=VMEM_LIMIT),
        name="outproj_sample",
    )(mg, xs, wout, prm["lng"], prm["lnb"])


def _lane_rows(*mats):
    v = jnp.concatenate([a.astype(F32) for a in mats], axis=1)
    return jnp.pad(v, ((0, 0), (0, LANES - v.shape[1])))


def _cache_rows_by_window(cache):
    depth, nb = cache.shape[:2]
    return jnp.transpose(cache, (0, 1, 3, 4, 2)).reshape(depth, nb, KV_WIDTH, WINDOW)


def _cache_window_major(cache_t):
    depth, nb = cache_t.shape[:2]
    return jnp.transpose(cache_t.reshape(depth, nb, KV_HEADS, HEAD_DIM, WINDOW), (0, 1, 4, 2, 3))


def kernel(x_prompt, x_sample, state_C, state_n, state_m, state_conv, cache_k, cache_v, w_in, w_out,
           b_igate, b_fgate, m_norm_g, conv_w, conv_b, conv_ln_g, conv_ln_b, sinks, ln_g, ln_b):
    batch, seq, _ = x_prompt.shape
    nb = x_sample.shape[0]
    xp = x_prompt.reshape(batch * seq, D_MODEL)
    xs = x_sample.reshape(nb, D_MODEL)
    wt = jnp.swapaxes(w_in, 1, 2)
    prm = {
        "gbias": _lane_rows(b_igate, b_fgate),
        "sink_row": _lane_rows(sinks),
        "mng": m_norm_g, "convw": conv_w, "convb": conv_b, "clng": conv_ln_g, "clnb": conv_ln_b,
        "lng": ln_g, "lnb": ln_b,
    }
    m_in = jnp.pad(state_m, ((0, 0), (0, 0), (0, LANES - M_HEADS)))
    sconv = jnp.swapaxes(state_conv, 1, 2)
    ckt = _cache_rows_by_window(cache_k)
    cvt = _cache_rows_by_window(cache_v)
    st_p = []
    st_s = []
    hs_all = []
    for l in range(DEPTH):
        pm, pc, pa, g, conv_p, hs, wout = _inproj(xp, xs, wt, w_out, prm, l, batch)
        hs_all.append(hs)
        sample = (jnp.stack(hs_all, axis=0), state_C, m_in) if l == DEPTH - 1 else None
        xp, c_p, n_p, m_p, k_p, v_p, *c_s = _mixer(
            l, sinks, pm, pc, pa, g, xp, wout, prm, batch, seq, sample)
        st_p.append((c_p, n_p, m_p, conv_p, k_p, v_p))
        mg, *states = _sample_mixer(l, hs, state_C, state_n, m_in, sconv, ckt, cvt, prm)
        st_s.append(states)
        xs = _outproj_sample(mg, xs, wout, prm, l)
    c_s, = c_s

    def stacked(per_layer):
        return [jnp.stack(a, axis=0) for a in zip(*per_layer)]

    c_p, n_p, m_p, conv_p, k_p, v_p = stacked(st_p)
    n_s, m_s, conv_s, k_s, v_s = stacked(st_s)
    return (xp.reshape(batch, seq, D_MODEL), xs.reshape(nb, 1, D_MODEL),
            c_p, n_p, m_p.reshape(DEPTH, batch, LANES)[:, :, :M_HEADS], conv_p,
            k_p.reshape(DEPTH, batch, WINDOW, KV_HEADS, HEAD_DIM),
            v_p.reshape(DEPTH, batch, WINDOW, KV_HEADS, HEAD_DIM),
            c_s, n_s, m_s[:, :, :M_HEADS], jnp.swapaxes(conv_s, 1, 2),
            _cache_window_major(k_s), _cache_window_major(v_s))
```

```python
import functools

import jax
import jax.numpy as jnp
from jax import lax
from jax.experimental import pallas as pl
from jax.experimental.pallas import tpu as pltpu

F32 = jnp.float32
BF16 = jnp.bfloat16

D_MODEL = 2048
DEPTH = 2
M_WIDTH = 1024
M_HEADS = 4
M_DK = 256
C_WIDTH = 512
CONV_K = 31
CONV_BUF = CONV_K - 1
A_WIDTH = 512
HEAD_DIM = 64
A_HEADS = 8
KV_HEADS = 2
GROUP = A_HEADS // KV_HEADS
KV_WIDTH = KV_HEADS * HEAD_DIM
WINDOW = 128
ALPHA = (2 * DEPTH) ** 0.25
LN_EPS = 1e-5
K_SCALE = M_DK ** -0.5
A_SCALE = HEAD_DIM ** -0.5

LANES = 128
SUBLANES = 8
BF16_SUBLANES = 2 * SUBLANES

HM_W = 5 * M_WIDTH
HC_W = 3 * C_WIDTH
HA_W = 2 * A_WIDTH + 2 * KV_WIDTH
MAIN_W = HM_W + HC_W + HA_W
GATE_END = MAIN_W + LANES
PROJ_W = GATE_END
GATE_OFF = 4 * M_WIDTH
N_GATES = 2 * M_HEADS
W_CHUNK = 256
W_SLOTS = 4
PLAIN_CHUNKS = GATE_OFF // W_CHUNK
MAIN_CHUNKS = MAIN_W // W_CHUNK
HA_K = A_WIDTH
HA_V = A_WIDTH + KV_WIDTH
HA_Z = A_WIDTH + 2 * KV_WIDTH

TM_PROJ = 256
TB = 256
A_BLK = 128
CONV_HALO = 32
SB = 8
VMEM_LIMIT = 56 * 1024 * 1024
INPROJ_VMEM_LIMIT = 60 * 1024 * 1024


def _sigmoid(x):
    return 1.0 / (1.0 + jnp.exp(-x))


def _silu(x):
    return x * _sigmoid(x)


def _log_sigmoid(x):
    return jnp.minimum(x, 0.0) - jnp.log1p(jnp.exp(-jnp.abs(x)))


def _layer_norm(x, g, b):
    mu = jnp.mean(x, axis=-1, keepdims=True)
    xc = x - mu
    var = jnp.mean(xc * xc, axis=-1, keepdims=True)
    return xc * lax.rsqrt(var + LN_EPS) * g + b


def _dot(a, b):
    return jnp.dot(a, b, preferred_element_type=F32)


def _dot_nt(a, b):
    return lax.dot_general(a, b, (((1,), (1,)), ((), ())), preferred_element_type=F32)


def _dot_tn(a, b):
    return lax.dot_general(a, b, (((0,), (0,)), ((), ())), preferred_element_type=F32)


def _whole(arr):
    zeros = (0,) * arr.ndim
    return pl.BlockSpec(arr.shape, lambda *_: zeros)


PM_W = 4 * M_WIDTH
PA_QKV = A_WIDTH + 2 * KV_WIDTH
QKV_CHUNK = 512
SHIFTED_ROWS = CONV_HALO + TM_PROJ - SUBLANES


def _stage_rows(src_hbm, dst_ref, stage, sem, n_chunks, src_row):
    def chunk_copy(c, slot):
        return pltpu.make_async_copy(
            src_hbm.at[pl.ds(pl.multiple_of(src_row(c), SUBLANES), W_CHUNK), :],
            stage.at[slot], sem.at[slot])

    for c in range(W_SLOTS - 1):
        chunk_copy(c, c).start()

    def body(c, carry):
        slot = c % W_SLOTS
        chunk_copy(c, slot).wait()
        ahead = c + W_SLOTS - 1

        @pl.when(ahead < n_chunks)
        def _():
            chunk_copy(ahead, ahead % W_SLOTS).start()

        dst_ref[pl.ds(pl.multiple_of(c * W_CHUNK, W_CHUNK), W_CHUNK), :] = stage[slot].astype(BF16)
        return carry

    lax.fori_loop(0, n_chunks, body, 0)


def _stage_w_in(w_hbm, wt_ref, stage, sem):
    _stage_rows(w_hbm, wt_ref, stage, sem, MAIN_CHUNKS,
                lambda c: jnp.where(c < PLAIN_CHUNKS, c * W_CHUNK, c * W_CHUNK + N_GATES))
    stage[0, 0:LANES, :] = jnp.zeros((LANES, D_MODEL), F32)
    gate_copy = pltpu.make_async_copy(
        w_hbm.at[GATE_OFF:GATE_OFF + N_GATES, :], stage.at[0, 0:N_GATES, :], sem.at[0])
    gate_copy.start()
    gate_copy.wait()
    wt_ref[MAIN_W:GATE_END, :] = stage[0, 0:LANES, :].astype(BF16)


def _inproj_kernel(layer, blocks_per_seq, x_ref, xs_ref, w_hbm, wo_in_ref, convw_ref, convb_ref,
                   clng_ref, clnb_ref, pm_ref, pc_ref, pa_ref, g_ref, conv_ref, hs_ref, wo_ref,
                   wt_ref, stage, sem, xbuf, abuf, sbuf, ybuf, zbuf):
    i = pl.program_id(0)
    lrow = slice(layer, layer + 1)

    n_sample = xs_ref.shape[0]

    @pl.when(i % blocks_per_seq == 0)
    def _():
        abuf[0:CONV_HALO, :] = jnp.zeros((CONV_HALO, C_WIDTH), F32)

    xbuf[0:TM_PROJ, :] = x_ref[...].astype(BF16)
    wo_ref[...] = wo_in_ref[...].astype(BF16)

    def project_block(n_extra):
        n_rows = TM_PROJ + n_extra

        def proj(c0, c1):
            full = _dot_nt(xbuf[0:n_rows, :], wt_ref[c0:c1, :])
            if n_extra:
                hs_ref[:, c0:c1] = full[TM_PROJ:n_rows, :]
                return full[0:TM_PROJ, :]
            return full

        for c in range(0, 3 * M_WIDTH, QKV_CHUNK):
            pm_ref[:, c:c + QKV_CHUNK] = proj(c, c + QKV_CHUNK).astype(BF16)
        for h in range(M_HEADS):
            o = proj(3 * M_WIDTH + h * M_DK, 3 * M_WIDTH + (h + 1) * M_DK)
            z = proj(4 * M_WIDTH + h * M_DK, 4 * M_WIDTH + (h + 1) * M_DK)
            pm_ref[:, 3 * M_WIDTH + h * M_DK:3 * M_WIDTH + (h + 1) * M_DK] = (
                _sigmoid(o) * _silu(z)).astype(BF16)

        cu = proj(HM_W, HM_W + C_WIDTH)
        cg = proj(HM_W + C_WIDTH, HM_W + 2 * C_WIDTH)
        abuf[CONV_HALO:CONV_HALO + TM_PROJ, :] = cu * _sigmoid(cg)
        zbuf[...] = _silu(proj(HM_W + 2 * C_WIDTH, HM_W + 3 * C_WIDTH))
        for p in range(1, SUBLANES):
            sbuf[p - 1] = abuf[p:p + SHIFTED_ROWS, :]
        tap0 = CONV_HALO - CONV_BUF
        for r in range(0, TM_PROJ, 128):
            for c in range(0, C_WIDTH, LANES):
                acc = jnp.zeros((128, LANES), F32)
                for t in range(CONV_K):
                    p = (tap0 + t) % SUBLANES
                    base = r + tap0 + t - p
                    src = abuf if p == 0 else sbuf.at[p - 1]
                    acc = acc + (src[base:base + 128, c:c + LANES]
                                 * convw_ref[layer, t:t + 1, c:c + LANES])
                ybuf[r:r + 128, c:c + LANES] = acc + convb_ref[lrow, c:c + LANES]
        conv_ref[0] = abuf[CONV_HALO + TM_PROJ - CONV_BUF:CONV_HALO + TM_PROJ, :]
        abuf[0:CONV_HALO, :] = abuf[TM_PROJ:TM_PROJ + CONV_HALO, :]
        yc = _layer_norm(ybuf[...], clng_ref[lrow, :], clnb_ref[lrow, :])
        pc_ref[...] = (_silu(yc) * zbuf[...]).astype(BF16)

        base = HM_W + HC_W
        pa_ref[:, 0:PA_QKV] = proj(base, base + PA_QKV).astype(BF16)
        pa_ref[:, PA_QKV:HA_W] = _silu(proj(base + PA_QKV, base + HA_W)).astype(BF16)
        g_ref[...] = proj(MAIN_W, GATE_END)

    @pl.when(i == 0)
    def _():
        _stage_w_in(w_hbm.at[layer], wt_ref, stage, sem)
        xbuf[TM_PROJ:TM_PROJ + n_sample, :] = xs_ref[...].astype(BF16)
        project_block(n_sample)

    @pl.when(i > 0)
    def _():
        project_block(0)


def _inproj(x2d, xs, w_in_t, w_out, prm, layer, batch):
    m = x2d.shape[0]
    ns = xs.shape[0]
    nblk = m // TM_PROJ
    assert D_MODEL % (nblk * BF16_SUBLANES) == 0
    wo_rows = D_MODEL // nblk
    row = lambda i: (i, 0)
    once = lambda i: (0, 0)
    params = [prm["convw"], prm["convb"], prm["clng"], prm["clnb"]]
    return pl.pallas_call(
        functools.partial(_inproj_kernel, layer, nblk // batch),
        grid=(nblk,),
        in_specs=[
            pl.BlockSpec((TM_PROJ, D_MODEL), row),
            pl.BlockSpec((ns, D_MODEL), once),
            pl.BlockSpec(memory_space=pl.ANY),
            pl.BlockSpec((None, wo_rows, D_MODEL), lambda i: (layer, i, 0)),
        ] + [_whole(a) for a in params],
        out_specs=[
            pl.BlockSpec((TM_PROJ, PM_W), row),
            pl.BlockSpec((TM_PROJ, C_WIDTH), row),
            pl.BlockSpec((TM_PROJ, HA_W), row),
            pl.BlockSpec((TM_PROJ, LANES), row),
            pl.BlockSpec((1, CONV_BUF, C_WIDTH), lambda i: (i // (nblk // batch), 0, 0)),
            pl.BlockSpec((ns, PROJ_W), once),
            pl.BlockSpec((wo_rows, D_MODEL), row),
        ],
        out_shape=[
            jax.ShapeDtypeStruct((m, PM_W), BF16),
            jax.ShapeDtypeStruct((m, C_WIDTH), BF16),
            jax.ShapeDtypeStruct((m, HA_W), BF16),
            jax.ShapeDtypeStruct((m, LANES), F32),
            jax.ShapeDtypeStruct((batch, CONV_BUF, C_WIDTH), F32),
            jax.ShapeDtypeStruct((ns, PROJ_W), F32),
            jax.ShapeDtypeStruct((D_MODEL, D_MODEL), BF16),
        ],
        scratch_shapes=[
            pltpu.VMEM((PROJ_W, D_MODEL), BF16),
            pltpu.VMEM((W_SLOTS, W_CHUNK, D_MODEL), F32),
            pltpu.SemaphoreType.DMA((W_SLOTS,)),
            pltpu.VMEM((TM_PROJ + ns, D_MODEL), BF16),
            pltpu.VMEM((CONV_HALO + TM_PROJ, C_WIDTH), F32),
            pltpu.VMEM((SUBLANES - 1, SHIFTED_ROWS, C_WIDTH), F32),
            pltpu.VMEM((TM_PROJ, C_WIDTH), F32),
            pltpu.VMEM((TM_PROJ, C_WIDTH), F32),
        ],
        compiler_params=pltpu.CompilerParams(
            dimension_semantics=("arbitrary",), vmem_limit_bytes=INPROJ_VMEM_LIMIT),
        name="inproj_prompt",
    )(x2d, xs, w_in_t, w_out, *params)


def _cumsum_rows(x):
    n = x.shape[0]
    row = lax.broadcasted_iota(jnp.int32, x.shape, 0)
    s = 1
    while s < n:
        x = x + jnp.where(row >= s, pltpu.roll(x, s, axis=0), 0.0)
        s *= 2
    return x


def _update_c_unit(unit, groups, ks_ref, vs_ref, gs_ref, ms_ref, gbias_ref, cs_ref, cs_out_ref):
    layer = unit // (groups * M_HEADS)
    h = unit % M_HEADS
    _, wgt, w0 = _sample_gates(gs_ref[...], gbias_ref[pl.ds(layer, 1), :], ms_ref[...])
    lane = lax.broadcasted_iota(jnp.int32, (SB, LANES), 1)
    wh = jnp.sum(jnp.where(lane == h, wgt, 0.0), axis=1, keepdims=True)
    w0h = jnp.sum(jnp.where(lane == h, w0, 0.0), axis=1, keepdims=True)
    k_t = (ks_ref[...] * K_SCALE).T
    gv = wh * vs_ref[...]
    for b in range(SB):
        cs_out_ref[b] = w0h[b:b + 1, :] * cs_ref[b] + k_t[:, b:b + 1] * gv[b:b + 1, :]


def _mixer_kernel(layer, n_blocks, blocks_per_seq, n_units, groups, sinks_ref, hm_ref, pc_ref,
                  ha_ref, g_ref, xprev_ref, wout_ref, gbias_ref, mng_ref, lng_ref, lnb_ref, *refs):
    sample_in, refs = (refs[:5], refs[5:]) if n_units else ((), refs)
    y_ref, c_ref, n_ref, m_ref, k_ref, v_ref = refs[:6]
    sample_out, refs = (refs[6:7], refs[7:]) if n_units else ((), refs[6:])
    kprev, vprev, merged, merged_prev = refs
    step = pl.program_id(0)
    j = jnp.minimum(step, n_blocks - 1) % blocks_per_seq
    lrow = slice(layer, layer + 1)

    @pl.when(step == 0)
    def _():
        merged_prev[...] = jnp.zeros_like(merged_prev)

    @pl.when(j == 0)
    def _():
        c_ref[...] = jnp.zeros_like(c_ref)
        n_ref[...] = jnp.zeros_like(n_ref)
        m_ref[...] = jnp.zeros_like(m_ref)
        kprev[...] = jnp.zeros_like(kprev)
        vprev[...] = jnp.zeros_like(vprev)

    y = _dot(merged_prev[...], wout_ref[...])
    y_ref[...] = _layer_norm(ALPHA * xprev_ref[...] + y, lng_ref[lrow, :], lnb_ref[lrow, :])

    if n_units:
        ks_ref, vs_ref, gs_ref, ms_ref, cs_ref = sample_in
        _update_c_unit(jnp.minimum(step, n_units - 1), groups, ks_ref, vs_ref, gs_ref, ms_ref,
                       gbias_ref, cs_ref, sample_out[0])

    gb = g_ref[...] + gbias_ref[lrow, :]
    bsum = _cumsum_rows(_log_sigmoid(gb))
    bsum = pltpu.roll(bsum, LANES - M_HEADS, axis=1)
    a_all = gb - bsum
    a_t = a_all.T
    m_prev = m_ref[0]
    row_i = lax.broadcasted_iota(jnp.int32, (TB, TB), 0)
    col_i = lax.broadcasted_iota(jnp.int32, (TB, TB), 1)
    causal = col_i <= row_i
    lane_i = lax.broadcasted_iota(jnp.int32, (1, LANES), 1)
    m_new = m_prev

    heads = range(M_HEADS)
    cs = [slice(h * M_DK, (h + 1) * M_DK) for h in heads]
    q_of = lambda h: hm_ref[:, cs[h]]
    k_of = lambda h: hm_ref[:, M_WIDTH + h * M_DK:M_WIDTH + (h + 1) * M_DK]
    v_of = lambda h: hm_ref[:, 2 * M_WIDTH + h * M_DK:2 * M_WIDTH + (h + 1) * M_DK]
    m0 = [m_prev[:, h:h + 1] for h in heads]
    n0 = [n_ref[0, h:h + 1, :] for h in heads]
    logit = [jnp.where(causal, a_t[h:h + 1, :], -jnp.inf) for h in heads]
    mx = [jnp.maximum(jnp.max(logit[h], axis=1, keepdims=True), m0[h]) for h in heads]
    w0 = [jnp.exp(m0[h] - mx[h]) for h in heads]
    s = [_dot_nt(q_of(h), k_of(h)) * (jnp.exp(logit[h] - mx[h]) * K_SCALE) for h in heads]
    num = [_dot(s[h].astype(BF16), v_of(h)) + w0[h] * _dot(q_of(h), c_ref[0, h].astype(BF16))
           for h in heads]
    den = [jnp.sum(s[h], axis=1, keepdims=True)
           + w0[h] * jnp.sum(q_of(h).astype(F32) * n0[h], axis=1, keepdims=True) for h in heads]
    hh = [num[h] / jnp.maximum(jnp.abs(den[h]), jnp.exp(-(bsum[:, h:h + 1] + mx[h])))
          for h in heads]
    mu = [jnp.mean(hh[h], axis=1, keepdims=True) for h in heads]
    hc = [hh[h] - mu[h] for h in heads]
    var = [jnp.mean(hc[h] * hc[h], axis=1, keepdims=True) for h in heads]
    for h in heads:
        hn = hc[h] * lax.rsqrt(var[h] + LN_EPS) * mng_ref[lrow, cs[h]]
        gate = hm_ref[:, 3 * M_WIDTH + h * M_DK:3 * M_WIDTH + (h + 1) * M_DK].astype(F32)
        merged[:, cs[h]] = (hn * gate).astype(BF16)
    c_new, n_new = [], []
    for h in heads:
        mx_end = mx[h][TB - 1:TB, :]
        g_col = jnp.exp(a_all[:, h:h + 1] - mx_end) * K_SCALE
        g0 = jnp.exp(m0[h] - mx_end)
        kg = k_of(h).astype(F32) * g_col
        c_new.append(g0 * c_ref[0, h] + _dot_tn(kg.astype(BF16), v_of(h)))
        n_new.append(g0 * n0[h] + jnp.sum(kg, axis=0, keepdims=True))
        m_end = bsum[TB - 1:TB, h:h + 1] + mx_end
        m_new = jnp.where(lane_i == h, m_end, m_new)

    merged[:, M_WIDTH:M_WIDTH + C_WIDTH] = pc_ref[...]

    arow = lax.broadcasted_iota(jnp.int32, (A_BLK, 2 * A_BLK), 0)
    acol = lax.broadcasted_iota(jnp.int32, (A_BLK, 2 * A_BLK), 1)
    rel = A_BLK + arow - acol
    band = (rel >= 0) & (rel <= WINDOW)
    n_sub = TB // A_BLK
    kcat, vext, mask = [], [], []
    ones = jnp.ones((2 * A_BLK, HEAD_DIM), BF16)
    for sub in range(n_sub):
        r0 = sub * A_BLK
        if sub == 0:
            kcat.append(jnp.concatenate(
                [kprev[...], ha_ref[0:A_BLK, HA_K:HA_K + KV_WIDTH]], axis=0))
            vcat = jnp.concatenate([vprev[...], ha_ref[0:A_BLK, HA_V:HA_V + KV_WIDTH]], axis=0)
            mask.append(band & (acol >= jnp.where(j > 0, 0, A_BLK)))
        else:
            kcat.append(ha_ref[r0 - A_BLK:r0 + A_BLK, HA_K:HA_K + KV_WIDTH])
            vcat = ha_ref[r0 - A_BLK:r0 + A_BLK, HA_V:HA_V + KV_WIDTH]
            mask.append(band)
        vext.append([jnp.concatenate([vcat[:, kvh * HEAD_DIM:(kvh + 1) * HEAD_DIM], ones], axis=1)
                     for kvh in range(KV_HEADS)])
    units = [(sub, hq) for sub in range(n_sub) for hq in range(A_HEADS)]
    sink = {hq: sinks_ref[layer, hq] for hq in range(A_HEADS)}
    sc = {}
    for sub, hq in units:
        kvh = hq // GROUP
        qh = ha_ref[sub * A_BLK:(sub + 1) * A_BLK, hq * HEAD_DIM:(hq + 1) * HEAD_DIM]
        kh = kcat[sub][:, kvh * HEAD_DIM:(kvh + 1) * HEAD_DIM]
        sc[sub, hq] = jnp.where(mask[sub], _dot_nt(qh, kh) * A_SCALE, -jnp.inf)
    smax = {u: jnp.maximum(jnp.max(sc[u], axis=1, keepdims=True), sink[u[1]]) for u in units}
    pv = {u: _dot(jnp.exp(sc[u] - smax[u]).astype(BF16), vext[u[0]][u[1] // GROUP])
          for u in units}
    outs = {u: pv[u][:, 0:HEAD_DIM]
            / (pv[u][:, HEAD_DIM:HEAD_DIM + 1] + jnp.exp(sink[u[1]] - smax[u]))
            for u in units}
    for sub in range(n_sub):
        r0 = sub * A_BLK
        for hp in range(A_HEADS // 2):
            ao = jnp.concatenate([outs[sub, 2 * hp], outs[sub, 2 * hp + 1]], axis=1)
            gate = ha_ref[r0:r0 + A_BLK, HA_Z + hp * LANES:HA_Z + (hp + 1) * LANES].astype(F32)
            col = M_WIDTH + C_WIDTH + hp * LANES
            merged[r0:r0 + A_BLK, col:col + LANES] = (ao * gate).astype(BF16)
    merged_prev[...] = merged[...]

    @pl.when(step < n_blocks)
    def _():
        for h in heads:
            c_ref[0, h] = c_new[h]
        n_ref[0] = jnp.concatenate(n_new, axis=0)
        m_ref[0] = m_new
        kprev[...] = ha_ref[TB - A_BLK:TB, HA_K:HA_K + KV_WIDTH]
        vprev[...] = ha_ref[TB - A_BLK:TB, HA_V:HA_V + KV_WIDTH]
        k_ref[0] = ha_ref[TB - WINDOW:TB, HA_K:HA_K + KV_WIDTH].astype(F32)
        v_ref[0] = ha_ref[TB - WINDOW:TB, HA_V:HA_V + KV_WIDTH].astype(F32)


def _mixer(layer, sinks, pm, pc, pa, g, x2d, wout, prm, batch, seq, sample=None):
    nb = seq // TB
    nblk = batch * nb
    n_units, groups, extra_in, extra_specs, extra_out, extra_shape = 0, 0, [], [], [], []
    if sample is not None:
        hs_all, c_s, m_s = sample
        depth, ns = hs_all.shape[:2]
        groups = ns // SB
        n_units = depth * groups * M_HEADS
        assert n_units <= nblk and MAIN_W % LANES == 0

        def unit(s):
            u = jnp.minimum(s, n_units - 1)
            return u // (groups * M_HEADS), (u // M_HEADS) % groups, u % M_HEADS

        def hs_cols(width, first):
            return pl.BlockSpec((None, SB, width),
                                lambda s: unit(s)[:2] + (first // width + unit(s)[2],))

        c_spec = pl.BlockSpec((None, SB, None, M_DK, M_DK), lambda s: unit(s) + (0, 0))
        extra_in = [hs_all, hs_all, hs_all, m_s, c_s]
        extra_specs = [
            hs_cols(M_DK, M_WIDTH), hs_cols(M_DK, 2 * M_WIDTH),
            pl.BlockSpec((None, SB, LANES), lambda s: unit(s)[:2] + (MAIN_W // LANES,)),
            pl.BlockSpec((None, SB, LANES), lambda s: unit(s)[:2] + (0,)),
            c_spec]
        extra_out = [c_spec]
        extra_shape = [jax.ShapeDtypeStruct(c_s.shape, F32)]
    mixed = lambda s: (jnp.minimum(s, nblk - 1), 0)
    done = lambda s: (jnp.maximum(s - 1, 0), 0)
    seq_of = lambda s: jnp.minimum(s, nblk - 1) // nb
    per_b3 = lambda s: (seq_of(s), 0, 0)
    state_shapes = [
        (batch, M_HEADS, M_DK, M_DK), (batch, M_HEADS, M_DK), (batch, 1, LANES),
        (batch, WINDOW, KV_WIDTH), (batch, WINDOW, KV_WIDTH)]
    params = [prm["gbias"], prm["mng"], prm["lng"], prm["lnb"]]
    return pl.pallas_call(
        functools.partial(_mixer_kernel, layer, nblk, nb, n_units, groups),
        grid=(nblk + 1,),
        in_specs=[
            pl.BlockSpec(memory_space=pltpu.SMEM),
            pl.BlockSpec((TB, PM_W), mixed),
            pl.BlockSpec((TB, C_WIDTH), mixed),
            pl.BlockSpec((TB, HA_W), mixed),
            pl.BlockSpec((TB, LANES), mixed),
            pl.BlockSpec((TB, D_MODEL), done),
            pl.BlockSpec((D_MODEL, D_MODEL), lambda s: (0, 0)),
        ] + [_whole(a) for a in params] + extra_specs,
        out_specs=[
            pl.BlockSpec((TB, D_MODEL), done),
            pl.BlockSpec((1, M_HEADS, M_DK, M_DK), lambda s: (seq_of(s), 0, 0, 0)),
            pl.BlockSpec((1, M_HEADS, M_DK), per_b3),
            pl.BlockSpec((1, 1, LANES), per_b3),
            pl.BlockSpec((1, WINDOW, KV_WIDTH), per_b3),
            pl.BlockSpec((1, WINDOW, KV_WIDTH), per_b3),
        ] + extra_out,
        out_shape=[jax.ShapeDtypeStruct((batch * seq, D_MODEL), F32)]
        + [jax.ShapeDtypeStruct(s, F32) for s in state_shapes] + extra_shape,
        scratch_shapes=[
            pltpu.VMEM((A_BLK, KV_WIDTH), BF16),
            pltpu.VMEM((A_BLK, KV_WIDTH), BF16),
            pltpu.VMEM((TB, D_MODEL), BF16),
            pltpu.VMEM((TB, D_MODEL), BF16),
        ],
        compiler_params=pltpu.CompilerParams(
            dimension_semantics=("arbitrary",), vmem_limit_bytes=VMEM_LIMIT),
        name="mixer_prompt",
    )(sinks, pm, pc, pa, g, x2d, wout, *params, *extra_in)


def _sample_gates(gates, gbias_row, m0):
    gb = gates + gbias_row
    lf = pltpu.roll(_log_sigmoid(gb), LANES - M_HEADS, axis=1)
    m_new = jnp.maximum(lf + m0, gb)
    wgt = jnp.exp(gb - m_new)
    w0 = jnp.exp(lf + m0 - m_new)
    return m_new, wgt, w0


def _sample_mixer_kernel(layer, hs_ref, c0_ref, c1_ref, c2_ref, c3_ref, n_ref, m_ref, conv_ref,
                         ckt_ref, cvt_ref, gbias_ref, mng_ref, convw_ref, convb_ref, clng_ref,
                         clnb_ref, sink_ref,
                         merged_ref, nout_ref, mout_ref, convout_ref, kout_ref, vout_ref):
    lrow = slice(layer, layer + 1)
    c_heads = (c0_ref, c1_ref, c2_ref, c3_ref)
    m_new, wgt, w0 = _sample_gates(hs_ref[:, MAIN_W:GATE_END], gbias_ref[lrow, :], m_ref[...])
    einv = jnp.exp(-m_new)
    mout_ref[...] = m_new
    for h in range(M_HEADS):
        cs = slice(h * M_DK, (h + 1) * M_DK)
        q = hs_ref[:, cs]
        k = hs_ref[:, M_WIDTH + h * M_DK:M_WIDTH + (h + 1) * M_DK] * K_SCALE
        v = hs_ref[:, 2 * M_WIDTH + h * M_DK:2 * M_WIDTH + (h + 1) * M_DK]
        q_t = q.T
        wh = wgt[:, h:h + 1]
        w0h = w0[:, h:h + 1]
        n0 = n_ref[:, h, :]
        s = jnp.sum(q * k, axis=1, keepdims=True) * wh
        qc = jnp.concatenate(
            [jnp.sum(q_t[:, b:b + 1] * c_heads[h][b, 0], axis=0, keepdims=True)
             for b in range(SB)],
            axis=0)
        num = s * v + w0h * qc
        den = s + w0h * jnp.sum(q * n0, axis=1, keepdims=True)
        hh = num / jnp.maximum(jnp.abs(den), einv[:, h:h + 1])
        nout_ref[:, h, :] = w0h * n0 + wh * k
        mu = jnp.mean(hh, axis=1, keepdims=True)
        hc = hh - mu
        var = jnp.mean(hc * hc, axis=1, keepdims=True)
        hn = hc * lax.rsqrt(var + LN_EPS) * mng_ref[lrow, cs]
        o = hs_ref[:, 3 * M_WIDTH + h * M_DK:3 * M_WIDTH + (h + 1) * M_DK]
        z = hs_ref[:, 4 * M_WIDTH + h * M_DK:4 * M_WIDTH + (h + 1) * M_DK]
        merged_ref[:, cs] = hn * _sigmoid(o) * _silu(z)

    cu = hs_ref[:, HM_W:HM_W + C_WIDTH]
    cg = hs_ref[:, HM_W + C_WIDTH:HM_W + 2 * C_WIDTH]
    cz = hs_ref[:, HM_W + 2 * C_WIDTH:HM_W + 3 * C_WIDTH]
    a = cu * _sigmoid(cg)
    yc = a * convw_ref[layer, CONV_BUF:CONV_K, :] + convb_ref[lrow, :]
    for t in range(CONV_BUF):
        yc = yc + conv_ref[t] * convw_ref[layer, t:t + 1, :]
    for t in range(CONV_BUF - 1):
        convout_ref[t] = conv_ref[t + 1]
    convout_ref[CONV_BUF - 1] = a
    yc = _layer_norm(yc, clng_ref[lrow, :], clnb_ref[lrow, :])
    merged_ref[:, M_WIDTH:M_WIDTH + C_WIDTH] = _silu(yc) * _silu(cz)

    ab = HM_W + HC_W
    lane = lax.broadcasted_iota(jnp.int32, (1, LANES), 1)
    rows_b = lax.broadcasted_iota(jnp.int32, (A_HEADS * SB, LANES), 0) % SB
    wcol = lax.broadcasted_iota(jnp.int32, (KV_WIDTH, WINDOW), 1)
    knew = hs_ref[:, ab + HA_K:ab + HA_K + KV_WIDTH]
    vnew = hs_ref[:, ab + HA_V:ab + HA_V + KV_WIDTH]
    q_parts, scn_parts, sink_parts = [], [], []
    for hq in range(A_HEADS):
        kvh = hq // GROUP
        t, u = hq // 2, hq % 2
        tile = hs_ref[:, ab + t * LANES:ab + (t + 1) * LANES]
        if u != kvh:
            tile = pltpu.roll(tile, HEAD_DIM, axis=1)
        qh = jnp.where((lane // HEAD_DIM) == kvh, tile, 0.0)
        q_parts.append(qh)
        scn_parts.append(jnp.sum(qh * knew, axis=1, keepdims=True))
        sink_parts.append(jnp.broadcast_to(sink_ref[lrow, hq:hq + 1], (SB, 1)))
    q_all = jnp.concatenate(q_parts, axis=0).astype(BF16)
    sc_new = jnp.concatenate(scn_parts, axis=0) * A_SCALE
    sink_all = jnp.concatenate(sink_parts, axis=0)
    sc = jnp.zeros((A_HEADS * SB, WINDOW), F32)
    for b in range(SB):
        sc = jnp.where(rows_b == b, _dot(q_all, ckt_ref[b].astype(BF16)), sc)
    sc = sc * A_SCALE
    smax = jnp.maximum(jnp.maximum(jnp.max(sc, axis=1, keepdims=True), sc_new), sink_all)
    e = jnp.exp(sc - smax)
    e_new = jnp.exp(sc_new - smax)
    denom = jnp.sum(e, axis=1, keepdims=True) + e_new + jnp.exp(sink_all - smax)
    p = (e / denom).astype(BF16)
    ob = (e_new / denom) * jnp.concatenate([vnew] * A_HEADS, axis=0)
    for b in range(SB):
        ob = ob + jnp.where(rows_b == b, _dot_nt(p, cvt_ref[b].astype(BF16)), 0.0)
    tiles = []
    for t in range(A_HEADS // 2):
        halves = []
        for u in range(2):
            hq = 2 * t + u
            x = ob[hq * SB:(hq + 1) * SB, :]
            if u != hq // GROUP:
                x = pltpu.roll(x, HEAD_DIM, axis=1)
            halves.append(x)
        tiles.append(jnp.where(lane < HEAD_DIM, halves[0], halves[1]))
    az = hs_ref[:, ab + HA_Z:ab + HA_Z + A_WIDTH]
    merged_ref[:, M_WIDTH + C_WIDTH:D_MODEL] = jnp.concatenate(tiles, axis=1) * _silu(az)
    knew_t = knew.T
    vnew_t = vnew.T
    for b in range(SB):
        kout_ref[b] = jnp.where(wcol == WINDOW - 1, knew_t[:, b:b + 1],
                                pltpu.roll(ckt_ref[b], WINDOW - 1, axis=1))
        vout_ref[b] = jnp.where(wcol == WINDOW - 1, vnew_t[:, b:b + 1],
                                pltpu.roll(cvt_ref[b], WINDOW - 1, axis=1))


def _sample_mixer(layer, hs, c, n, m, conv, ckt, cvt, prm):
    nb = hs.shape[0]
    row2 = lambda i: (i, 0)

    def state_specs(lead, pre):
        return [
            pl.BlockSpec(lead + (SB, M_HEADS, M_DK), lambda i: pre + (i, 0, 0)),
            pl.BlockSpec(lead + (SB, LANES), lambda i: pre + (i, 0)),
            pl.BlockSpec(lead + (CONV_BUF, SB, C_WIDTH), lambda i: pre + (0, i, 0)),
            pl.BlockSpec(lead + (SB, KV_WIDTH, WINDOW), lambda i: pre + (i, 0, 0)),
            pl.BlockSpec(lead + (SB, KV_WIDTH, WINDOW), lambda i: pre + (i, 0, 0)),
        ]

    c_specs = [pl.BlockSpec((None, SB, 1, M_DK, M_DK), lambda i, h=h: (layer, i, h, 0, 0))
               for h in range(M_HEADS)]
    state_shapes = [
        (nb, M_HEADS, M_DK), (nb, LANES),
        (CONV_BUF, nb, C_WIDTH), (nb, KV_WIDTH, WINDOW), (nb, KV_WIDTH, WINDOW)]
    params = [prm["gbias"], prm["mng"], prm["convw"], prm["convb"], prm["clng"], prm["clnb"],
              prm["sink_row"]]
    return pl.pallas_call(
        functools.partial(_sample_mixer_kernel, layer),
        grid=(nb // SB,),
        in_specs=([pl.BlockSpec((SB, PROJ_W), row2)] + c_specs + state_specs((None,), (layer,))
                  + [_whole(a) for a in params]),
        out_specs=[pl.BlockSpec((SB, D_MODEL), row2)] + state_specs((), ()),
        out_shape=[jax.ShapeDtypeStruct((nb, D_MODEL), F32)]
        + [jax.ShapeDtypeStruct(s, F32) for s in state_shapes],
        compiler_params=pltpu.CompilerParams(
            dimension_semantics=("arbitrary",), vmem_limit_bytes=VMEM_LIMIT),
        name="mixer_sample",
    )(hs, *([c] * M_HEADS), n, m, conv, ckt, cvt, *params)


def _outproj_sample_kernel(layer, mg_ref, x_ref, wout_ref, lng_ref, lnb_ref, y_ref):
    lrow = slice(layer, layer + 1)
    y = _dot(mg_ref[...].astype(BF16), wout_ref[...])
    y_ref[...] = _layer_norm(ALPHA * x_ref[...] + y, lng_ref[lrow, :], lnb_ref[lrow, :])


def _outproj_sample(mg, xs, wout, prm, layer):
    nb = xs.shape[0]
    full = lambda shape: pl.BlockSpec(shape, lambda i: (0, 0))
    return pl.pallas_call(
        functools.partial(_outproj_sample_kernel, layer),
        grid=(1,),
        in_specs=[full((nb, D_MODEL)), full((nb, D_MODEL)),
                  full((D_MODEL, D_MODEL)),
                  _whole(prm["lng"]), _whole(prm["lnb"])],
        out_specs=full((nb, D_MODEL)),
        out_shape=jax.ShapeDtypeStruct((nb, D_MODEL), F32),
        compiler_params=pltpu.CompilerParams(
            dimension_semantics=("arbitrary",), vmem_limit_bytes=VMEM_LIMIT),
        name="outproj_sample",
    )(mg, xs, wout, prm["lng"], prm["lnb"])


def _lane_rows(*mats):
    v = jnp.concatenate([a.astype(F32) for a in mats], axis=1)
    return jnp.pad(v, ((0, 0), (0, LANES - v.shape[1])))


def _cache_rows_by_window(cache):
    depth, nb = cache.shape[:2]
    return jnp.transpose(cache, (0, 1, 3, 4, 2)).reshape(depth, nb, KV_WIDTH, WINDOW)


def _cache_window_major(cache_t):
    depth, nb = cache_t.shape[:2]
    return jnp.transpose(cache_t.reshape(depth, nb, KV_HEADS, HEAD_DIM, WINDOW), (0, 1, 4, 2, 3))


def kernel(x_prompt, x_sample, state_C, state_n, state_m, state_conv, cache_k, cache_v, w_in, w_out,
           b_igate, b_fgate, m_norm_g, conv_w, conv_b, conv_ln_g, conv_ln_b, sinks, ln_g, ln_b):
    batch, seq, _ = x_prompt.shape
    nb = x_sample.shape[0]
    xp = x_prompt.reshape(batch * seq, D_MODEL)
    xs = x_sample.reshape(nb, D_MODEL)
    wt = jnp.swapaxes(w_in, 1, 2)
    prm = {
        "gbias": _lane_rows(b_igate, b_fgate),
        "sink_row": _lane_rows(sinks),
        "mng": m_norm_g, "convw": conv_w, "convb": conv_b, "clng": conv_ln_g, "clnb": conv_ln_b,
        "lng": ln_g, "lnb": ln_b,
    }
    m_in = jnp.pad(state_m, ((0, 0), (0, 0), (0, LANES - M_HEADS)))
    sconv = jnp.swapaxes(state_conv, 1, 2)
    ckt = _cache_rows_by_window(cache_k)
    cvt = _cache_rows_by_window(cache_v)
    st_p = []
    st_s = []
    hs_all = []
    for l in range(DEPTH):
        pm, pc, pa, g, conv_p, hs, wout = _inproj(xp, xs, wt, w_out, prm, l, batch)
        hs_all.append(hs)
        sample = (jnp.stack(hs_all, axis=0), state_C, m_in) if l == DEPTH - 1 else None
        xp, c_p, n_p, m_p, k_p, v_p, *c_s = _mixer(
            l, sinks, pm, pc, pa, g, xp, wout, prm, batch, seq, sample)
        st_p.append((c_p, n_p, m_p, conv_p, k_p, v_p))
        mg, *states = _sample_mixer(l, hs, state_C, state_n, m_in, sconv, ckt, cvt, prm)
        st_s.append(states)
        xs = _outproj_sample(mg, xs, wout, prm, l)
    c_s, = c_s

    def stacked(per_layer):
        return [jnp.stack(a, axis=0) for a in zip(*per_layer)]

    c_p, n_p, m_p, conv_p, k_p, v_p = stacked(st_p)
    n_s, m_s, conv_s, k_s, v_s = stacked(st_s)
    return (xp.reshape(batch, seq, D_MODEL), xs.reshape(nb, 1, D_MODEL),
            c_p, n_p, m_p.reshape(DEPTH, batch, LANES)[:, :, :M_HEADS], conv_p,
            k_p.reshape(DEPTH, batch, WINDOW, KV_HEADS, HEAD_DIM),
            v_p.reshape(DEPTH, batch, WINDOW, KV_HEADS, HEAD_DIM),
            c_s, n_s, m_s[:, :, :M_HEADS], jnp.swapaxes(conv_s, 1, 2),
            _cache_window_major(k_s), _cache_window_major(v_s))
```
